```python
import jax
import jax.numpy as jnp
from jax import lax
import numpy as np

D_MODEL = 1024
BATCH = 4
SEQ = 4096
DEPTH = 1
DEC_BATCH = 128
DEC_SEQ = 4
PAST_LEN = 2048
PAGE_SIZE = 128

HEAD_DIM = 64
NSA_HEADS = D_MODEL // (2 * HEAD_DIM)
NSA_KV = 2
NSA_REP = NSA_HEADS // NSA_KV
SB_HEADS = D_MODEL // (2 * HEAD_DIM)
NSA_W = NSA_HEADS * HEAD_DIM
SB_W = SB_HEADS * HEAD_DIM
KV_W = NSA_KV * HEAD_DIM
CMP_BLK = 32
CMP_STRIDE = 16
SEL_BLK = 64
TOP_N = 16
WINDOW = 512
Q_BLK = 128
FORCE_BONUS = 1.0e4
ROPE_THETA = 500000.0
ROPE_DIMS = HEAD_DIM // 4
N_EXPERTS = 32
TOP_K = 4
D_FF = D_MODEL
SWIGLU_LIMIT = 7.0
SWIGLU_ALPHA = 1.702
RMS_EPS = 1e-5
IN_SIZES = (NSA_W, KV_W, KV_W, KV_W, KV_W, KV_W, KV_W, 3 * NSA_HEADS, SB_W, SB_W, SB_W, NSA_W, SB_W)
N_IN = sum(IN_SIZES)
WIN_BUF = min(WINDOW, PAST_LEN)

kernel_name = 'nsa_stickbreaking_gated_moe_step'


def rmsnorm(x, g):
    xf = x.astype(jnp.float32)
    y = xf * lax.rsqrt(jnp.mean(xf * xf, axis=-1, keepdims=True) + RMS_EPS)
    return (y * g.astype(jnp.float32)).astype(x.dtype)


def masked_softmax(s, mask):
    s = jnp.where(mask, s, -1e30)
    e = jnp.where(mask, jnp.exp(s - jnp.max(s, axis=-1, keepdims=True)), 0.0)
    return e / jnp.maximum(jnp.sum(e, axis=-1, keepdims=True), 1e-30)


def rope(x, pos):
    half = ROPE_DIMS // 2
    inv = ROPE_THETA ** (-jnp.arange(half, dtype=jnp.float32) / half)
    ang = pos.astype(jnp.float32)[:, None] * inv[None, :]
    cos = jnp.cos(ang)[:, None, :].astype(x.dtype)
    sin = jnp.sin(ang)[:, None, :].astype(x.dtype)
    x1 = x[..., :half]
    x2 = x[..., half:ROPE_DIMS]
    return jnp.concatenate([x1 * cos - x2 * sin, x2 * cos + x1 * sin, x[..., ROPE_DIMS:]], axis=-1)


def project(h, w, pos):
    B, L, _ = h.shape
    splits = np.cumsum(IN_SIZES)[:-1].tolist()
    (q, k_cmp, v_cmp, k_slc, v_slc, k_win, v_win, br_gate,
     sb_q, sb_k, sb_v, gate_a, gate_b) = jnp.split(h @ w, splits, axis=-1)
    kv = lambda t: t.reshape(B, L, NSA_KV, HEAD_DIM)
    sb = lambda t: t.reshape(B, L, SB_HEADS, HEAD_DIM)
    grp = lambda t: t.reshape(B, L, NSA_KV, NSA_REP, HEAD_DIM)
    qh = q.reshape(B, L, NSA_HEADS, HEAD_DIM)
    return {
        'q_cmp': grp(qh),
        'q_rot': grp(rope(qh, pos)),
        'k_cmp': kv(k_cmp), 'v_cmp': kv(v_cmp),
        'k_slc': rope(kv(k_slc), pos), 'v_slc': kv(v_slc),
        'k_win': rope(kv(k_win), pos), 'v_win': kv(v_win),
        'br_gate': jax.nn.sigmoid(br_gate).reshape(B, L, 3, NSA_KV, NSA_REP, 1),
        'sb_q': sb(sb_q), 'sb_k': sb(sb_k), 'sb_v': sb(sb_v),
        'gate_a': jax.nn.sigmoid(gate_a), 'gate_b': jax.nn.sigmoid(gate_b),
    }


def compress(kv, w):
    B, L = kv.shape[:2]
    c = kv.reshape(B, L // CMP_STRIDE, CMP_STRIDE, NSA_KV, HEAD_DIM)
    blocks = jnp.concatenate([c[:, :-1], c[:, 1:]], axis=2)
    return jnp.einsum('bcigd,ide->bcge', blocks, w)


def sel_blocks(kv):
    B, L = kv.shape[:2]
    return kv.reshape(B, L // SEL_BLK, SEL_BLK, NSA_KV, HEAD_DIM).transpose(0, 3, 1, 2, 4)


def nsa_global(q_c, q_r, kc, vc, ks_blocks, vs_blocks, q_pos):
    B, nq = q_c.shape[:2]
    n_cmp = kc.shape[1]
    n_sel = ks_blocks.shape[2]
    scale = HEAD_DIM ** -0.5
    c_end = jnp.arange(n_cmp) * CMP_STRIDE + (CMP_BLK - 1)
    cmask = (c_end[None, :] <= q_pos[:, None])[None, :, None, None, :]
    p_cmp = masked_softmax(jnp.einsum('bqgrd,bcgd->bqgrc', q_c, kc).astype(jnp.float32) * scale, cmask)
    o_cmp = jnp.einsum('bqgrc,bcgd->bqgrd', p_cmp.astype(vc.dtype), vc)
    imp = jnp.pad(p_cmp.sum(axis=3), ((0, 0), (0, 0), (0, 0), (1, 1)))
    chunk = 0.5 * (imp[..., 1:] + imp[..., :-1])
    imp_sel = chunk.reshape(B, nq, NSA_KV, n_sel, SEL_BLK // CMP_STRIDE).sum(-1)
    blk = jnp.arange(n_sel)[None, :]
    cur = (q_pos // SEL_BLK)[:, None]
    forced = (blk == 0) | (blk == cur) | (blk == cur - 1)
    score = jnp.where((blk <= cur)[None, :, None, :],
                      imp_sel + jnp.where(forced, FORCE_BONUS, 0.0)[None, :, None, :], -1e9)
    _, idx = lax.top_k(score, min(TOP_N, n_sel))
    idx = idx.transpose(0, 2, 1, 3)
    take = jax.vmap(jax.vmap(lambda blocks, ix: blocks[ix]))
    nk = idx.shape[-1] * SEL_BLK
    k_sel = take(ks_blocks, idx).reshape(B, NSA_KV, nq, nk, HEAD_DIM)
    v_sel = take(vs_blocks, idx).reshape(B, NSA_KV, nq, nk, HEAD_DIM)
    k_pos = (idx[..., None] * SEL_BLK + jnp.arange(SEL_BLK)).reshape(B, NSA_KV, nq, nk)
    smask = (k_pos <= q_pos[None, None, :, None])[:, :, :, None, :]
    p_sel = masked_softmax(jnp.einsum('bqgrd,bgqkd->bgqrk', q_r, k_sel).astype(jnp.float32) * scale, smask)
    o_sel = jnp.einsum('bgqrk,bgqkd->bqgrd', p_sel.astype(v_sel.dtype), v_sel)
    return o_cmp, o_sel


def band(t, nb):
    B = t.shape[0]
    tp = jnp.pad(t, ((0, 0), (WINDOW, 0), (0, 0), (0, 0))).reshape(B, nb + WINDOW // Q_BLK, Q_BLK, NSA_KV, HEAD_DIM)
    return jnp.concatenate([tp[:, i:i + nb] for i in range(WINDOW // Q_BLK + 1)], axis=2)


def window_attn(q, k, v, q_pos, k_pos):
    s = jnp.einsum('bnqgrd,bnkgd->bnqgrk', q, k).astype(jnp.float32) * HEAD_DIM ** -0.5
    diff = q_pos[:, :, None] - k_pos[:, None, :]
    mask = (diff >= 0) & (diff < WINDOW) & (k_pos[:, None, :] >= 0)
    p = masked_softmax(s, mask[None, :, :, None, None, :])
    return jnp.einsum('bnqgrk,bnkgd->bnqgrd', p.astype(v.dtype), v)


def stick_breaking(q, k, v, q_pos, k_pos):
    z = jnp.einsum('bqhd,bkhd->bhqk', q, k).astype(jnp.float32) * HEAD_DIM ** -0.5
    mask = (k_pos[None, :] < q_pos[:, None])[None, None]
    sp = jnp.where(mask, jax.nn.softplus(z), 0.0)
    after = lax.cumsum(sp, axis=3, reverse=True) - sp
    a = jnp.where(mask, jnp.exp(jax.nn.log_sigmoid(z) - after), 0.0)
    return jnp.einsum('bhqk,bkhd->bqhd', a.astype(v.dtype), v)


def merge(pr, o_cmp, o_slc, o_win, o_sb, w):
    B, L = o_cmp.shape[:2]
    g = pr['br_gate']
    o_a = (g[:, :, 0] * o_cmp + g[:, :, 1] * o_slc + g[:, :, 2] * o_win).reshape(B, L, NSA_W)
    o_b = o_sb.reshape(B, L, SB_W)
    return jnp.concatenate([pr['gate_a'] * o_a, pr['gate_b'] * o_b], axis=-1) @ w


def moe(h, w_router, b_router, w_gu, b_gu, w_dn, b_dn):
    logits = (h @ w_router + b_router).astype(jnp.float32)
    top_v, top_i = lax.top_k(logits, TOP_K)
    wts = jax.nn.softmax(top_v, axis=-1)
    gate = jnp.einsum('tk,tke->te', wts, jax.nn.one_hot(top_i, N_EXPERTS, dtype=jnp.float32)).astype(h.dtype)
    out = jnp.zeros_like(h)
    for e in range(N_EXPERTS):
        gu = h @ w_gu[e] + b_gu[e]
        g = jnp.minimum(gu[:, 0::2], SWIGLU_LIMIT)
        u = jnp.clip(gu[:, 1::2], -SWIGLU_LIMIT, SWIGLU_LIMIT)
        act = (u + 1.0) * g * jax.nn.sigmoid(SWIGLU_ALPHA * g)
        out = out + gate[:, e:e + 1] * (act @ w_dn[e] + b_dn[e])
    return out


def gather_pages(pool, page_table):
    g = pool[page_table]
    return g.reshape(g.shape[0], g.shape[1] * g.shape[2], *g.shape[3:])


def setup_inputs(seed: int = 0) -> dict:
    key = jax.random.key(seed)
    ks = jax.random.split(key, 20)
    n_pages = PAST_LEN // PAGE_SIZE
    n_phys = (5 * DEC_BATCH * n_pages) // 4

    def nrm(k, shape, scale=1.0):
        return scale * jax.random.normal(k, shape, dtype=jnp.float32)

    page_table = jax.random.permutation(ks[6], n_phys)[:DEC_BATCH * n_pages].reshape(DEC_BATCH, n_pages).astype(jnp.int32)
    return {
        'x_prompt': nrm(ks[0], (BATCH, SEQ, D_MODEL)),
        'x_sample': nrm(ks[1], (DEC_BATCH, DEC_SEQ, D_MODEL)),
        'cache_nsa_cmp': nrm(ks[2], (DEPTH, n_phys, PAGE_SIZE, 2, NSA_KV, HEAD_DIM)),
        'cache_nsa_slc': nrm(ks[3], (DEPTH, n_phys, PAGE_SIZE, 2, NSA_KV, HEAD_DIM)),
        'cache_sb': nrm(ks[4], (DEPTH, n_phys, PAGE_SIZE, 2, SB_HEADS, HEAD_DIM)),
        'state_nsa_win': nrm(ks[5], (DEPTH, DEC_BATCH, WIN_BUF, 2, NSA_KV, HEAD_DIM)),
        'page_table': page_table,
        'g_mix': 1.0 + nrm(ks[7], (DEPTH, D_MODEL), 0.02),
        'w_in': nrm(ks[8], (DEPTH, D_MODEL, N_IN), D_MODEL ** -0.5),
        'w_ck': nrm(ks[9], (DEPTH, CMP_BLK, HEAD_DIM, HEAD_DIM), (CMP_BLK * HEAD_DIM) ** -0.5),
        'w_cv': nrm(ks[10], (DEPTH, CMP_BLK, HEAD_DIM, HEAD_DIM), (CMP_BLK * HEAD_DIM) ** -0.5),
        'w_out': nrm(ks[11], (DEPTH, NSA_W + SB_W, D_MODEL), (NSA_W + SB_W) ** -0.5),
        'g_ffn': 1.0 + nrm(ks[12], (DEPTH, D_MODEL), 0.02),
        'w_router': nrm(ks[13], (DEPTH, D_MODEL, N_EXPERTS), D_MODEL ** -0.5),
        'b_router': nrm(ks[14], (DEPTH, N_EXPERTS), 0.01),
        'w_gu': nrm(ks[15], (DEPTH, N_EXPERTS, D_MODEL, 2 * D_FF), D_MODEL ** -0.5),
        'b_gu': nrm(ks[16], (DEPTH, N_EXPERTS, 2 * D_FF), 0.01),
        'w_dn': nrm(ks[17], (DEPTH, N_EXPERTS, D_FF, D_MODEL), D_FF ** -0.5),
        'b_dn': nrm(ks[18], (DEPTH, N_EXPERTS, D_MODEL), 0.01),
        'g_final': 1.0 + nrm(ks[19], (D_MODEL,), 0.02),
    }


def reference(x_prompt, x_sample, cache_nsa_cmp, cache_nsa_slc, cache_sb, state_nsa_win, page_table,
              g_mix, w_in, w_ck, w_cv, w_out, g_ffn, w_router, b_router, w_gu, b_gu, w_dn, b_dn, g_final):
    B, S, _ = x_prompt.shape
    BD, DS, _ = x_sample.shape
    nb = S // Q_BLK
    L_s = PAST_LEN + DS
    L_pad = -(-L_s // SEL_BLK) * SEL_BLK
    pos_p = jnp.arange(S, dtype=jnp.int32)
    pos_s = PAST_LEN + jnp.arange(DS, dtype=jnp.int32)
    kpos_band = jnp.arange(nb, dtype=jnp.int32)[:, None] * Q_BLK - WINDOW + jnp.arange((WINDOW // Q_BLK + 1) * Q_BLK, dtype=jnp.int32)[None, :]
    kpos_w = PAST_LEN - WIN_BUF + jnp.arange(WIN_BUF + DS, dtype=jnp.int32)
    pad_rows = ((0, 0), (0, L_pad - L_s), (0, 0), (0, 0), (0, 0))

    def to_blocks(t):
        return t.reshape(B, nb, Q_BLK, *t.shape[2:]).swapaxes(0, 1)

    def from_blocks(t):
        return t.swapaxes(0, 1).reshape(B, S, *t.shape[3:])

    xp, xs = x_prompt, x_sample
    cmp_p, slc_p, sb_p, win_p = [], [], [], []
    cmp_s, slc_s, sb_s, win_s = [], [], [], []
    for l in range(DEPTH):
        pr = project(rmsnorm(xp, g_mix[l]), w_in[l], pos_p)
        kc = compress(pr['k_cmp'], w_ck[l])
        vc = compress(pr['v_cmp'], w_cv[l])
        ksb = sel_blocks(pr['k_slc'])
        vsb = sel_blocks(pr['v_slc'])
        o_cmp, o_slc = lax.map(lambda a: nsa_global(a[0], a[1], kc, vc, ksb, vsb, a[2]),
                               (to_blocks(pr['q_cmp']), to_blocks(pr['q_rot']), pos_p.reshape(nb, Q_BLK)))
        o_cmp, o_slc = from_blocks(o_cmp), from_blocks(o_slc)
        o_win = window_attn(pr['q_rot'].reshape(B, nb, Q_BLK, NSA_KV, NSA_REP, HEAD_DIM),
                            band(pr['k_win'], nb), band(pr['v_win'], nb),
                            pos_p.reshape(nb, Q_BLK), kpos_band).reshape(B, S, NSA_KV, NSA_REP, HEAD_DIM)
        o_sb = from_blocks(lax.map(lambda a: stick_breaking(a[0], pr['sb_k'], pr['sb_v'], a[1], pos_p),
                                   (to_blocks(pr['sb_q']), pos_p.reshape(nb, Q_BLK))))
        xp = xp + merge(pr, o_cmp, o_slc, o_win, o_sb, w_out[l])
        cmp_p.append(jnp.stack([pr['k_cmp'], pr['v_cmp']], axis=2))
        slc_p.append(jnp.stack([pr['k_slc'], pr['v_slc']], axis=2))
        sb_p.append(jnp.stack([pr['sb_k'], pr['sb_v']], axis=2))
        wrows = jnp.stack([pr['k_win'], pr['v_win']], axis=2)
        win_p.append(jnp.pad(wrows, ((0, 0), (max(WIN_BUF - S, 0), 0), (0, 0), (0, 0), (0, 0)))[:, -WIN_BUF:])

        pl = project(rmsnorm(xs, g_mix[l]), w_in[l], pos_s)
        row_cmp = jnp.stack([pl['k_cmp'], pl['v_cmp']], axis=2)
        row_slc = jnp.stack([pl['k_slc'], pl['v_slc']], axis=2)
        row_sb = jnp.stack([pl['sb_k'], pl['sb_v']], axis=2)
        all_cmp = jnp.pad(jnp.concatenate([gather_pages(cache_nsa_cmp[l], page_table), row_cmp], axis=1), pad_rows)
        all_slc = jnp.pad(jnp.concatenate([gather_pages(cache_nsa_slc[l], page_table), row_slc], axis=1), pad_rows)
        all_sb = jnp.concatenate([gather_pages(cache_sb[l], page_table), row_sb], axis=1)
        o_cmp_s, o_slc_s = nsa_global(pl['q_cmp'], pl['q_rot'],
                                      compress(all_cmp[:, :, 0], w_ck[l]), compress(all_cmp[:, :, 1], w_cv[l]),
                                      sel_blocks(all_slc[:, :, 0]), sel_blocks(all_slc[:, :, 1]), pos_s)
        win_all = jnp.concatenate([state_nsa_win[l], jnp.stack([pl['k_win'], pl['v_win']], axis=2)], axis=1)
        o_win_s = window_attn(pl['q_rot'][:, None], win_all[:, None, :, 0], win_all[:, None, :, 1],
                              pos_s[None], kpos_w[None])[:, 0]
        o_sb_s = stick_breaking(pl['sb_q'], all_sb[:, :, 0], all_sb[:, :, 1], pos_s, jnp.arange(L_s, dtype=jnp.int32))
        xs = xs + merge(pl, o_cmp_s, o_slc_s, o_win_s, o_sb_s, w_out[l])
        cmp_s.append(row_cmp)
        slc_s.append(row_slc)
        sb_s.append(row_sb)
        win_s.append(win_all[:, DS:])

        hp = rmsnorm(xp, g_ffn[l]).reshape(B * S, D_MODEL)
        hs = rmsnorm(xs, g_ffn[l]).reshape(BD * DS, D_MODEL)
        f = moe(jnp.concatenate([hp, hs], axis=0), w_router[l], b_router[l], w_gu[l], b_gu[l], w_dn[l], b_dn[l])
        xp = xp + f[:B * S].reshape(B, S, D_MODEL)
        xs = xs + f[B * S:].reshape(BD, DS, D_MODEL)

    y_prompt = rmsnorm(xp, g_final)
    y_sample = rmsnorm(xs, g_final)
    new_cmp_p = jnp.stack(cmp_p, axis=0)
    new_slc_p = jnp.stack(slc_p, axis=0)
    new_sb_p = jnp.stack(sb_p, axis=0)
    new_win_p = jnp.stack(win_p, axis=0)
    new_cmp_s = jnp.stack(cmp_s, axis=0)
    new_slc_s = jnp.stack(slc_s, axis=0)
    new_sb_s = jnp.stack(sb_s, axis=0)
    new_win_s = jnp.stack(win_s, axis=0)
    return (y_prompt, y_sample, new_cmp_p, new_slc_p, new_sb_p, new_win_p, new_cmp_s, new_slc_s, new_sb_s, new_win_s)
```

```python
import functools

import numpy as np
import jax
import jax.numpy as jnp
from jax import lax
from jax.experimental import pallas as pl
from jax.experimental.pallas import tpu as pltpu

D_MODEL = 1024
HEAD_DIM = 64
NSA_HEADS = 8
NSA_KV = 2
NSA_REP = NSA_HEADS // NSA_KV
SB_HEADS = 8
NSA_W = NSA_HEADS * HEAD_DIM
SB_W = SB_HEADS * HEAD_DIM
KV_W = NSA_KV * HEAD_DIM
CMP_BLK = 32
CMP_STRIDE = 16
SEL_BLK = 64
TOP_N = 16
WINDOW = 512
FORCE_BONUS = 1.0e4
ROPE_THETA = 500000.0
ROPE_DIMS = HEAD_DIM // 4
N_EXPERTS = 32
TOP_K = 4
D_FF = D_MODEL
SWIGLU_LIMIT = 7.0
SWIGLU_ALPHA = 1.702
RMS_EPS = 1e-5
PAGE_SIZE = 128
SCALE = HEAD_DIM ** -0.5

LANES = 128
VMEM_LIMIT_BYTES = 56 * 1024 * 1024

BF16 = jnp.bfloat16
F32 = jnp.float32

_OFF = dict(q=0, k_cmp=512, v_cmp=640, k_slc=768, v_slc=896, k_win=1024, v_win=1152, br=1280,
            sb_q=1304, sb_k=1816, sb_v=2328, gate_a=2840, gate_b=3352, end=3864)
_HEAD_PERM = [j + 4 * half for j in range(4) for half in range(2)]
_C_Q, _C_SBQ, _C_CMP, _C_SLC, _C_WIN, _C_SBKV, _C_GATE, _C_BR, _NP = 0, 512, 1024, 1280, 1536, 1792, 2816, 3840, 3968


def _cparams(sem):
    return pltpu.CompilerParams(dimension_semantics=sem, vmem_limit_bytes=VMEM_LIMIT_BYTES)


def _prep_w_in(w):
    def head_perm(base):
        return [w[:, base + h * HEAD_DIM: base + (h + 1) * HEAD_DIM] for h in _HEAD_PERM]
    parts = [c * SCALE for c in head_perm(_OFF['q'])]
    parts.append(w[:, _OFF['sb_q']:_OFF['sb_k']] * SCALE)
    parts.append(w[:, _OFF['k_cmp']:_OFF['br']])
    parts.append(w[:, _OFF['sb_k']:_OFF['gate_a']])
    parts.extend(head_perm(_OFF['gate_a']))
    parts.append(w[:, _OFF['gate_b']:_OFF['end']])
    parts.append(w[:, _OFF['br']:_OFF['sb_q']])
    parts.append(jnp.zeros((w.shape[0], LANES - 3 * NSA_HEADS), w.dtype))
    return jnp.concatenate(parts, axis=1).astype(BF16)


def _rope_tables(pos):
    half = ROPE_DIMS // 2
    inv = ROPE_THETA ** (-jnp.arange(half, dtype=F32) / half)
    ang = pos.astype(F32)[:, None] * inv[None, :]
    cos = jnp.cos(ang)
    sin = jnp.sin(ang)
    n = pos.shape[0]
    one = jnp.ones((n, HEAD_DIM - ROPE_DIMS), F32)
    zero = jnp.zeros((n, HEAD_DIM - ROPE_DIMS), F32)
    z8 = jnp.zeros((n, half), F32)
    c = jnp.concatenate([cos, cos, one], axis=1)
    s1 = jnp.concatenate([-sin, z8, zero], axis=1)
    s2 = jnp.concatenate([z8, sin, zero], axis=1)
    tile2 = lambda t: jnp.concatenate([t, t], axis=1)
    return tile2(c), tile2(s1), tile2(s2)


def _rope_apply(x, c, s1, s2):
    outs = []
    for j in range(x.shape[1] // LANES):
        xb = x[:, j * LANES:(j + 1) * LANES]
        outs.append(xb * c + pltpu.roll(xb, LANES - 8, 1) * s1 + pltpu.roll(xb, 8, 1) * s2)
    return outs[0] if len(outs) == 1 else jnp.concatenate(outs, axis=1)


def _proj_kernel(x_ref, g_ref, w_ref, c_ref, s1_ref, s2_ref,
                 qc_ref, qr_ref, sbq_ref, cmp_ref, slc_ref, win_ref, sbkv_ref,
                 kslc_ref, vslc_ref, kwin_ref, vwin_ref, sbk_ref, sbv_ref, gate_ref, br_ref):
    x = x_ref[...]
    ms = jnp.mean(x * x, axis=-1, keepdims=True)
    h = (x * lax.rsqrt(ms + RMS_EPS)) * g_ref[...]
    hb = h.astype(BF16)
    c, s1, s2 = c_ref[...], s1_ref[...], s2_ref[...]

    def mm(c0, c1):
        return jnp.dot(hb, w_ref[:, c0:c1], preferred_element_type=F32)

    q = mm(_C_Q, _C_SBQ)
    qc_ref[...] = q.astype(BF16)
    qr_ref[...] = _rope_apply(q, c, s1, s2).astype(BF16)
    sbq_ref[...] = mm(_C_SBQ, _C_CMP).astype(BF16)
    cmp_ref[...] = mm(_C_CMP, _C_SLC)
    for c0, cache_ref, k_ref, v_ref in ((_C_SLC, slc_ref, kslc_ref, vslc_ref), (_C_WIN, win_ref, kwin_ref, vwin_ref)):
        z = mm(c0, c0 + 2 * KV_W)
        k = _rope_apply(z[:, :KV_W], c, s1, s2)
        v = z[:, KV_W:]
        cache_ref[:, :KV_W] = k
        cache_ref[:, KV_W:] = v
        k_ref[...] = k.astype(BF16)
        v_ref[...] = v.astype(BF16)
    z = mm(_C_SBKV, _C_GATE)
    sbkv_ref[...] = z
    sbk_ref[...] = z[:, :SB_W].astype(BF16)
    sbv_ref[...] = z[:, SB_W:].astype(BF16)
    gate_ref[...] = jax.nn.sigmoid(mm(_C_GATE, _C_BR)).astype(BF16)
    br_ref[...] = jax.nn.sigmoid(mm(_C_BR, _NP))


def _proj(x2d, g, w_perm, tabs, tm):
    t = x2d.shape[0]
    assert t % tm == 0
    row = lambda n: pl.BlockSpec((tm, n), lambda i: (i, 0))
    full = lambda a: pl.BlockSpec(a.shape, lambda i: (0,) * a.ndim)
    outs = [(512, BF16), (512, BF16), (512, BF16), (256, F32), (256, F32), (256, F32), (1024, F32),
            (128, BF16), (128, BF16), (128, BF16), (128, BF16), (512, BF16), (512, BF16), (1024, BF16), (128, F32)]
    return pl.pallas_call(
        _proj_kernel,
        grid=(t // tm,),
        in_specs=[row(D_MODEL), full(g), full(w_perm), row(LANES), row(LANES), row(LANES)],
        out_specs=[row(n) for n, _ in outs],
        out_shape=[jax.ShapeDtypeStruct((t, n), dt) for n, dt in outs],
        compiler_params=_cparams(("parallel",)),
        name="proj",
    )(x2d, g, w_perm, *tabs)


def _prep_w_cmp(w_ck, w_cv):
    eye2 = jnp.eye(NSA_KV, dtype=w_ck.dtype)
    bd = lambda w: jnp.einsum('ab,ide->iadbe', eye2, w).reshape(CMP_BLK, KV_W, KV_W)
    z = jnp.zeros((CMP_BLK, KV_W, KV_W), w_ck.dtype)
    w_all = jnp.concatenate([jnp.concatenate([bd(w_ck), z], axis=2),
                             jnp.concatenate([z, bd(w_cv)], axis=2)], axis=1)
    w_all = w_all.astype(BF16)
    return (w_all[:CMP_STRIDE].reshape(CMP_STRIDE * 2 * KV_W, 2 * KV_W),
            w_all[CMP_STRIDE:].reshape(CMP_STRIDE * 2 * KV_W, 2 * KV_W))


def _compress_rows(xb, w1, w2):
    n = xb.shape[0]
    y1 = jnp.dot(xb, w1, preferred_element_type=F32)
    y2 = jnp.dot(xb, w2, preferred_element_type=F32)
    y2s = pltpu.roll(y2, n - 1, 0)
    rid = lax.broadcasted_iota(jnp.int32, y2.shape, 0)
    return y1 + jnp.where(rid < n - 1, y2s, 0.0)


def _compress_kernel(x_ref, w1_ref, w2_ref, k_ref, v_ref):
    out = _compress_rows(x_ref[...].astype(BF16), w1_ref[...], w2_ref[...])
    k_ref[...] = out[:, :KV_W].astype(BF16)
    v_ref[...] = out[:, KV_W:].astype(BF16)


def _compress(cache2d, w1, w2, b, s):
    nc = s // CMP_STRIDE
    x = cache2d.reshape(b, nc, CMP_STRIDE * 2 * KV_W)
    full = lambda a: pl.BlockSpec(a.shape, lambda i: (0,) * a.ndim)
    blk = pl.BlockSpec((None, nc, KV_W), lambda i: (i, 0, 0))
    return pl.pallas_call(
        _compress_kernel,
        grid=(b,),
        in_specs=[pl.BlockSpec((None, nc, x.shape[2]), lambda i: (i, 0, 0)), full(w1), full(w2)],
        out_specs=[blk, blk],
        out_shape=[jax.ShapeDtypeStruct((b, nc, KV_W), BF16)] * 2,
        compiler_params=_cparams(("parallel",)),
        name="compress",
    )(x, w1, w2)


_NT = (((1,), (1,)), ((), ()))


def _dot_nt(a, b):
    return lax.dot_general(a, b, _NT, preferred_element_type=F32)


def _dot3(x, m):
    hi = x.astype(BF16)
    r1 = x - hi.astype(F32)
    mid = r1.astype(BF16)
    lo = (r1 - mid.astype(F32)).astype(BF16)
    d = lambda a: jnp.dot(a, m, preferred_element_type=F32)
    return d(hi) + d(mid) + d(lo)


def _stack_group(ref, g, tq):
    half = lax.broadcasted_iota(jnp.int32, (tq, LANES), 1) >> 6
    return jnp.concatenate([jnp.where(half == g, ref[:, j * LANES:(j + 1) * LANES], 0)
                            for j in range(NSA_REP)], axis=0).astype(BF16)


def _unstack_pairs(o0, o1, tq):
    half = lax.broadcasted_iota(jnp.int32, (tq, LANES), 1) >> 6
    return jnp.concatenate([jnp.where(half == 0, o0[j * tq:(j + 1) * tq], o1[j * tq:(j + 1) * tq])
                            for j in range(NSA_REP)], axis=1)


def _imp_matrix(nc, nsel=None):
    m = np.zeros((nc, LANES), np.float32)
    for j in range(nc * CMP_STRIDE // SEL_BLK if nsel is None else nsel):
        for c, wgt in ((4 * j - 1, 0.5), (4 * j, 1.0), (4 * j + 1, 1.0), (4 * j + 2, 1.0), (4 * j + 3, 0.5)):
            if 0 <= c < nc:
                m[c, j] += wgt
    return jnp.asarray(m, BF16)


def _select_blocks(imp_sel, cur, nsel):
    tq = imp_sel.shape[0]
    blk = lax.broadcasted_iota(jnp.int32, (tq, LANES), 1)
    forced = (blk == 0) | (blk == cur) | (blk == cur - 1)
    score = jnp.where(blk <= cur, imp_sel + jnp.where(forced, FORCE_BONUS, 0.0), -1e9)
    st = score.T
    nrow = -(-nsel // 8) * 8
    st = st[:nrow]
    jidx = lax.broadcasted_iota(jnp.int32, (nrow, tq), 0)
    cnt = jnp.zeros((nrow, tq), jnp.int32)
    for k in range(nsel):
        sk = st[k:k + 1, :]
        beats = (sk > st) | ((sk == st) & (jidx > k))
        cnt = cnt + jnp.where(beats, 1, 0)
    sel_t = jnp.where(cnt < TOP_N, 1.0, 0.0)
    if nrow < LANES:
        sel_t = jnp.concatenate([sel_t, jnp.zeros((LANES - nrow, tq), F32)], axis=0)
    return sel_t.T.astype(BF16)


def _nsa_kernel(qc_ref, qr_ref, kc_ref, vc_ref, ks_ref, vs_ref, m_ref, ocmp_ref, oslc_ref, *, tq, kt):
    q0 = pl.program_id(1) * tq
    nc = kc_ref.shape[0]
    nsel = nc * CMP_STRIDE // SEL_BLK
    rpos = q0 + (lax.broadcasted_iota(jnp.int32, (NSA_REP * tq, 1), 0) & (tq - 1))
    qpos = q0 + lax.broadcasted_iota(jnp.int32, (tq, 1), 0)
    o_c, o_s = [], []
    for g in range(NSA_KV):
        qcg = _stack_group(qc_ref, g, tq)
        qrg = _stack_group(qr_ref, g, tq)
        s = _dot_nt(qcg, kc_ref[...])
        cidx = lax.broadcasted_iota(jnp.int32, s.shape, 1)
        cmask = cidx * CMP_STRIDE + (CMP_BLK - 1) <= rpos
        s = jnp.where(cmask, s, -1e30)
        e = jnp.where(cmask, jnp.exp(s - jnp.max(s, axis=-1, keepdims=True)), 0.0)
        p = e / jnp.maximum(jnp.sum(e, axis=-1, keepdims=True), 1e-30)
        o_c.append(jnp.dot(p.astype(BF16), vc_ref[...], preferred_element_type=F32))
        imp = p[0:tq] + p[tq:2 * tq] + p[2 * tq:3 * tq] + p[3 * tq:4 * tq]
        sel = _select_blocks(_dot3(imp, m_ref[...]), qpos >> 6, nsel)

        def body(t, carry):
            m, l, acc = carry
            k0 = pl.multiple_of(t * kt, kt)
            k_t = ks_ref[pl.ds(k0, kt), :]
            v_t = vs_ref[pl.ds(k0, kt), :]
            sc = _dot_nt(qrg, k_t)
            erow = lax.broadcasted_iota(jnp.int32, (LANES, kt), 0)
            ecol = lax.broadcasted_iota(jnp.int32, (LANES, kt), 1)
            expand = jnp.where(erow == t * (kt // SEL_BLK) + (ecol >> 6), 1.0, 0.0).astype(BF16)
            bm = jnp.dot(sel, expand, preferred_element_type=F32)
            bm4 = jnp.concatenate([bm] * NSA_REP, axis=0)
            kpos = k0 + lax.broadcasted_iota(jnp.int32, sc.shape, 1)
            valid = (bm4 > 0.5) & (kpos <= rpos)
            sc = jnp.where(valid, sc, -1e30)
            m_new = jnp.maximum(m, jnp.max(sc, axis=-1, keepdims=True))
            a = jnp.where(valid, jnp.exp(sc - m_new), 0.0)
            alpha = jnp.exp(m - m_new)
            l = alpha * l + jnp.sum(a, axis=-1, keepdims=True)
            acc = alpha * acc + jnp.dot(a.astype(BF16), v_t, preferred_element_type=F32)
            return m_new, l, acc

        ntiles = (q0 + tq - 1) // kt + 1
        init = (jnp.full((NSA_REP * tq, 1), -1e30, F32), jnp.zeros((NSA_REP * tq, 1), F32),
                jnp.zeros((NSA_REP * tq, LANES), F32))
        _, l, acc = lax.fori_loop(0, ntiles, body, init)
        o_s.append(acc / jnp.maximum(l, 1e-30))
    ocmp_ref[...] = _unstack_pairs(o_c[0], o_c[1], tq)
    oslc_ref[...] = _unstack_pairs(o_s[0], o_s[1], tq)


def _nsa(qc, qr, kc, vc, ks, vs, b, s, tq=128, kt=512):
    nqb = s // tq
    kt = min(kt, s)
    m = _imp_matrix(s // CMP_STRIDE)
    qspec = pl.BlockSpec((tq, NSA_W), lambda bi, i: (bi * nqb + i, 0))
    cspec = pl.BlockSpec((None, s // CMP_STRIDE, KV_W), lambda bi, i: (bi, 0, 0))
    kspec = pl.BlockSpec((s, KV_W), lambda bi, i: (bi, 0))
    return pl.pallas_call(
        functools.partial(_nsa_kernel, tq=tq, kt=kt),
        grid=(b, nqb),
        in_specs=[qspec, qspec, cspec, cspec, kspec, kspec, pl.BlockSpec(m.shape, lambda bi, i: (0, 0))],
        out_specs=[qspec, qspec],
        out_shape=[jax.ShapeDtypeStruct((b * s, NSA_W), F32)] * 2,
        compiler_params=_cparams(("parallel", "parallel")),
        name="nsa",
    )(qc, qr, kc, vc, ks, vs, m)


def _window_kernel(qr_ref, k_ref, v_ref, o_ref, *, tq, span):
    q0 = pl.program_id(1) * tq
    s_len = k_ref.shape[0]
    start = pl.multiple_of(jnp.minimum(jnp.maximum(q0 - WINDOW, 0), s_len - span), tq)
    rpos = q0 + (lax.broadcasted_iota(jnp.int32, (NSA_REP * tq, 1), 0) & (tq - 1))
    k = k_ref[pl.ds(start, span), :]
    v = v_ref[pl.ds(start, span), :]
    outs = []
    for g in range(NSA_KV):
        qrg = _stack_group(qr_ref, g, tq)
        s = _dot_nt(qrg, k)
        diff = rpos - (start + lax.broadcasted_iota(jnp.int32, s.shape, 1))
        mask = (diff >= 0) & (diff < WINDOW)
        s = jnp.where(mask, s, -1e30)
        e = jnp.where(mask, jnp.exp(s - jnp.max(s, axis=-1, keepdims=True)), 0.0)
        p = e / jnp.maximum(jnp.sum(e, axis=-1, keepdims=True), 1e-30)
        outs.append(jnp.dot(p.astype(BF16), v, preferred_element_type=F32))
    o_ref[...] = _unstack_pairs(outs[0], outs[1], tq)


def _window(qr, kw, vw, b, s, tq=128):
    nqb = s // tq
    span = min(WINDOW + tq, s)
    qspec = pl.BlockSpec((tq, NSA_W), lambda bi, i: (bi * nqb + i, 0))
    kspec = pl.BlockSpec((s, KV_W), lambda bi, i: (bi, 0))
    return pl.pallas_call(
        functools.partial(_window_kernel, tq=tq, span=span),
        grid=(b, nqb),
        in_specs=[qspec, kspec, kspec],
        out_specs=qspec,
        out_shape=jax.ShapeDtypeStruct((b * s, NSA_W), F32),
        compiler_params=_cparams(("parallel", "parallel")),
        name="window",
    )(qr, kw, vw)


def _softplus(z):
    return jnp.maximum(z, 0.0) + jnp.log1p(jnp.exp(-jnp.abs(z)))


def _tri_matrix(n):
    return jnp.asarray(np.tril(np.ones((n, n), np.float32)), BF16)


def _sb_block(qh, k_t, v_t, tri, carry, acc, valid):
    z = _dot_nt(qh, k_t)
    sp = _softplus(z)
    if valid is not None:
        sp = jnp.where(valid, sp, 0.0)
    cum = jnp.dot(sp.astype(BF16), tri, preferred_element_type=F32) + carry
    a = jnp.exp(z - cum)
    if valid is not None:
        a = jnp.where(valid, a, 0.0)
    acc = acc + jnp.dot(a.astype(BF16), v_t, preferred_element_type=F32)
    return cum[:, 0:1], acc


def _sb_kernel(q_ref, k_ref, v_ref, tri_ref, o_ref, *, tq):
    i = pl.program_id(2)
    half = lax.broadcasted_iota(jnp.int32, (tq, LANES), 1) >> 6
    q = q_ref[...]
    qh = [jnp.where(half == hh, q, 0) for hh in range(2)]
    tri = tri_ref[...]
    d0 = pl.multiple_of(i * tq, tq)
    k_t = k_ref[pl.ds(d0, tq), :]
    v_t = v_ref[pl.ds(d0, tq), :]
    valid = lax.broadcasted_iota(jnp.int32, (tq, tq), 1) < lax.broadcasted_iota(jnp.int32, (tq, tq), 0)
    state = []
    for hh in range(2):
        state.extend(_sb_block(qh[hh], k_t, v_t, tri, jnp.zeros((tq, 1), F32), jnp.zeros((tq, LANES), F32), valid))

    def body(it, st):
        j0 = pl.multiple_of((i - it) * tq, tq)
        k_b = k_ref[pl.ds(j0, tq), :]
        v_b = v_ref[pl.ds(j0, tq), :]
        out = []
        for hh in range(2):
            out.extend(_sb_block(qh[hh], k_b, v_b, tri, st[2 * hh], st[2 * hh + 1], None))
        return tuple(out)

    st = lax.fori_loop(1, i + 1, body, tuple(state))
    o_ref[...] = jnp.where(half == 0, st[1], st[3])


def _sb(sbq, sbk, sbv, b, s, tq=256):
    tq = min(tq, s)
    nqb = s // tq
    tri = _tri_matrix(tq)
    npair = SB_HEADS // 2
    qspec = pl.BlockSpec((tq, LANES), lambda bi, p, i: (bi * nqb + i, p))
    kspec = pl.BlockSpec((s, LANES), lambda bi, p, i: (bi, p))
    return pl.pallas_call(
        functools.partial(_sb_kernel, tq=tq),
        grid=(b, npair, nqb),
        in_specs=[qspec, kspec, kspec, pl.BlockSpec(tri.shape, lambda bi, p, i: (0, 0))],
        out_specs=qspec,
        out_shape=jax.ShapeDtypeStruct((b * s, SB_W), F32),
        compiler_params=_cparams(("parallel", "parallel", "parallel")),
        name="sb",
    )(sbq, sbk, sbv, tri)


def _prep_merge(w_out, w_router, b_router):
    w_a = [w_out[h * HEAD_DIM:(h + 1) * HEAD_DIM] for h in _HEAD_PERM]
    w_out_p = jnp.concatenate(w_a + [w_out[NSA_W:]], axis=0).astype(BF16)
    e = np.zeros((LANES, 3 * NSA_W), np.float32)
    for br in range(3):
        for p, h in enumerate(_HEAD_PERM):
            e[br * NSA_HEADS + h, br * NSA_W + p * HEAD_DIM: br * NSA_W + (p + 1) * HEAD_DIM] = 1.0
    w_r = jnp.concatenate([w_router, jnp.zeros((D_MODEL, LANES - N_EXPERTS), w_router.dtype)], axis=1)
    b_r = jnp.concatenate([b_router, jnp.full((LANES - N_EXPERTS,), -1e30, b_router.dtype)]).reshape(1, LANES)
    return w_out_p, jnp.asarray(e, BF16), w_r, b_r


def _pack_bf16_pairs(lo, hi):
    lo_u = lax.bitcast_convert_type(lo, jnp.uint32)
    hi_u = lax.bitcast_convert_type(hi, jnp.uint32)
    return (lo_u >> 16) | (hi_u & jnp.uint32(0xFFFF0000))


def _unpack_bf16_pairs(u):
    lo = lax.bitcast_convert_type(u << 16, F32).astype(BF16)
    hi = lax.bitcast_convert_type(u & jnp.uint32(0xFFFF0000), F32).astype(BF16)
    return lo, hi


def _merge_kernel(x_ref, oc_ref, os_ref, ow_ref, ob_ref, gate_ref, br_ref, e_ref, wo_ref, g_ref, wr_ref, brt_ref,
                  x1_ref, hp_ref, idx_ref, wts_ref):
    gexp = _dot3(br_ref[...], e_ref[...])
    o_a = gexp[:, :NSA_W] * oc_ref[...] + gexp[:, NSA_W:2 * NSA_W] * os_ref[...] + gexp[:, 2 * NSA_W:] * ow_ref[...]
    gate = gate_ref[...].astype(F32)
    cat = jnp.concatenate([gate[:, :NSA_W] * o_a, gate[:, NSA_W:] * ob_ref[...]], axis=1).astype(BF16)
    x1 = x_ref[...] + jnp.dot(cat, wo_ref[...], preferred_element_type=F32)
    x1_ref[...] = x1
    ms = jnp.mean(x1 * x1, axis=-1, keepdims=True)
    h = (x1 * lax.rsqrt(ms + RMS_EPS)) * g_ref[...]
    hb = h.astype(BF16).astype(F32)
    hp_ref[...] = _pack_bf16_pairs(hb[:, :D_MODEL // 2], hb[:, D_MODEL // 2:])
    lg = jnp.dot(h, wr_ref[...], preferred_element_type=F32, precision=lax.Precision.HIGHEST) + brt_ref[...]
    lane = lax.broadcasted_iota(jnp.int32, lg.shape, 1)
    vals, ids = [], []
    for _ in range(TOP_K):
        mx = jnp.max(lg, axis=-1, keepdims=True)
        first = jnp.min(jnp.where(lg == mx, lane, LANES), axis=-1, keepdims=True)
        vals.append(mx)
        ids.append(first)
        lg = jnp.where(lane == first, -jnp.inf, lg)
    es = [jnp.exp(v - vals[0]) for v in vals]
    tot = es[0] + es[1] + es[2] + es[3]
    idx = jnp.zeros(lg.shape, jnp.int32)
    wts = jnp.zeros(lg.shape, F32)
    for k in range(TOP_K):
        idx = jnp.where(lane == k, ids[k], idx)
        wts = jnp.where(lane == k, es[k] / tot, wts)
    idx_ref[...] = idx
    wts_ref[...] = wts


def _merge(x2d, oc, os_, ow, ob, gate, br, prep, g_ffn, tm=256):
    t = x2d.shape[0]
    tm = min(tm, t)
    assert t % tm == 0
    w_out_p, e, w_r, b_r = prep
    row = lambda n: pl.BlockSpec((tm, n), lambda i: (i, 0))
    full = lambda a: pl.BlockSpec(a.shape, lambda i: (0,) * a.ndim)
    return pl.pallas_call(
        _merge_kernel,
        grid=(t // tm,),
        in_specs=[row(D_MODEL), row(NSA_W), row(NSA_W), row(NSA_W), row(SB_W), row(NSA_W + SB_W), row(LANES),
                  full(e), full(w_out_p), full(g_ffn), full(w_r), full(b_r)],
        out_specs=[row(D_MODEL), row(D_MODEL // 2), row(LANES), row(LANES)],
        out_shape=[jax.ShapeDtypeStruct((t, D_MODEL), F32), jax.ShapeDtypeStruct((t, D_MODEL // 2), jnp.uint32),
                   jax.ShapeDtypeStruct((t, LANES), jnp.int32), jax.ShapeDtypeStruct((t, LANES), F32)],
        compiler_params=_cparams(("parallel",)),
        name="merge",
    )(x2d, oc, os_, ow, ob, gate, br, e, w_out_p, g_ffn, w_r, b_r)


MOE_TILE = 256
GATHER_ROWS = 512
COMBINE_ROWS = 128


def _route_tables(idx):
    t = idx.shape[0]
    a = t * TOP_K
    n_tiles = a // MOE_TILE + N_EXPERTS
    e = idx.reshape(a)
    onehot = (e[:, None] == jnp.arange(N_EXPERTS, dtype=jnp.int32)[None, :]).astype(jnp.int32)
    csum = jnp.cumsum(onehot, axis=0)
    rank = jnp.take_along_axis(csum, e[:, None], axis=1)[:, 0] - 1
    counts = csum[-1]
    tiles_e = (counts + MOE_TILE - 1) // MOE_TILE
    tile_end = jnp.cumsum(tiles_e)
    tile_start = tile_end - tiles_e
    dest = tile_start[e] * MOE_TILE + rank
    row_token = jnp.zeros((n_tiles * MOE_TILE,), jnp.int32).at[dest].set(jnp.arange(a, dtype=jnp.int32) // TOP_K)
    tile_expert = jnp.minimum(jnp.searchsorted(tile_end, jnp.arange(n_tiles, dtype=jnp.int32), side='right'),
                              N_EXPERTS - 1).astype(jnp.int32)
    return row_token, dest.reshape(t, TOP_K), tile_expert, tile_end[-1:].astype(jnp.int32)


def _gather_kernel(idx_ref, src_ref, out_ref, sem):
    i = pl.program_id(0)
    n = pl.num_programs(0)

    def copy(r, step):
        return pltpu.make_async_copy(src_ref.at[pl.ds(idx_ref[0, r], 1)],
                                     out_ref.at[pl.ds(step * GATHER_ROWS + r, 1)], sem)

    def start(r, c):
        copy(r, i).start()
        return c

    def wait(r, c):
        copy(r, i).wait()
        return c

    lax.fori_loop(0, GATHER_ROWS, start, 0)

    @pl.when(i > 0)
    def _():
        lax.fori_loop(0, GATHER_ROWS, wait, 0)

    @pl.when(i == n - 1)
    def _():
        lax.fori_loop(0, GATHER_ROWS, wait, 0)


def _gather_rows(src, row_token):
    r = row_token.shape[0]
    assert r % GATHER_ROWS == 0
    steps = r // GATHER_ROWS
    return pl.pallas_call(
        _gather_kernel,
        grid=(steps,),
        in_specs=[pl.BlockSpec((None, 1, GATHER_ROWS), lambda i: (i, 0, 0), memory_space=pltpu.SMEM),
                  pl.BlockSpec(memory_space=pl.ANY)],
        out_specs=pl.BlockSpec(memory_space=pl.ANY),
        out_shape=jax.ShapeDtypeStruct((r, src.shape[1]), src.dtype),
        scratch_shapes=[pltpu.SemaphoreType.DMA(())],
        compiler_params=_cparams(("arbitrary",)),
        name="moe_gather",
    )(row_token.reshape(steps, 1, GATHER_ROWS), src)


def _expert_kernel(te_ref, nu_ref, hs_ref, wg_ref, wu_ref, bg_ref, bu_ref, wd_ref, bd_ref, y_ref):
    i = pl.program_id(0)

    @pl.when(i < nu_ref[0])
    def _():
        lo, hi = _unpack_bf16_pairs(hs_ref[...])
        half = D_MODEL // 2

        def up(w_ref, b_ref):
            return (jnp.dot(lo, w_ref[:half], preferred_element_type=F32)
                    + jnp.dot(hi, w_ref[half:], preferred_element_type=F32) + b_ref[...])

        g = jnp.minimum(up(wg_ref, bg_ref), SWIGLU_LIMIT)
        u = jnp.clip(up(wu_ref, bu_ref), -SWIGLU_LIMIT, SWIGLU_LIMIT)
        act = (u + 1.0) * g * jax.nn.sigmoid(SWIGLU_ALPHA * g)
        y_ref[...] = jnp.dot(act.astype(BF16), wd_ref[...], preferred_element_type=F32) + bd_ref[...]

    @pl.when(i >= nu_ref[0])
    def _():
        y_ref[...] = jnp.zeros(y_ref.shape, F32)


def _experts(hs, tile_expert, n_used, wg, wu, bg, bu, wd, bd):
    r = hs.shape[0]
    n_tiles = r // MOE_TILE
    wspec = lambda shape: pl.BlockSpec((None,) + shape, lambda i, te, nu: (te[i], 0, 0))
    return pl.pallas_call(
        _expert_kernel,
        grid_spec=pltpu.PrefetchScalarGridSpec(
            num_scalar_prefetch=2,
            grid=(n_tiles,),
            in_specs=[pl.BlockSpec((MOE_TILE, D_MODEL // 2), lambda i, te, nu: (i, 0)),
                      wspec((D_MODEL, D_FF)), wspec((D_MODEL, D_FF)), wspec((1, D_FF)), wspec((1, D_FF)),
                      wspec((D_FF, D_MODEL)), wspec((1, D_MODEL))],
            out_specs=pl.BlockSpec((MOE_TILE, D_MODEL), lambda i, te, nu: (i, 0)),
        ),
        out_shape=jax.ShapeDtypeStruct((r, D_MODEL), F32),
        compiler_params=_cparams(("arbitrary",)),
        name="moe_experts",
    )(tile_expert, n_used, hs, wg, wu, bg, bu, wd, bd)


def _combine_kernel(pos_ref, posn_ref, x_ref, wts_ref, g_ref, y_ref, out_ref, buf, sem):
    i = pl.program_id(0)
    n = pl.num_programs(0)
    n_rows = TOP_K * COMBINE_ROWS

    def copy(p_ref, r, slot):
        return pltpu.make_async_copy(y_ref.at[pl.ds(p_ref[0, r], 1)],
                                     buf.at[slot, r & (TOP_K - 1), pl.ds(r >> 2, 1)], sem.at[slot])

    def issue(p_ref, slot):
        def body(r, c):
            copy(p_ref, r, slot).start()
            return c
        lax.fori_loop(0, n_rows, body, 0)

    slot = i % 2

    @pl.when(i == 0)
    def _():
        issue(pos_ref, 0)

    @pl.when(i + 1 < n)
    def _():
        issue(posn_ref, 1 - slot)

    def wait(r, c):
        copy(pos_ref, r, slot).wait()
        return c
    lax.fori_loop(0, n_rows, wait, 0)

    w = wts_ref[...]
    x2 = x_ref[...]
    for k in range(TOP_K):
        x2 = x2 + w[:, k:k + 1] * buf[slot, k]
    ms = jnp.mean(x2 * x2, axis=-1, keepdims=True)
    out_ref[...] = (x2 * lax.rsqrt(ms + RMS_EPS)) * g_ref[...]


def _combine(x1, wts, pos, y_sorted, g_final):
    t = x1.shape[0]
    assert t % COMBINE_ROWS == 0
    steps = t // COMBINE_ROWS
    pos2 = pos.reshape(steps, 1, TOP_K * COMBINE_ROWS)
    row = lambda nn: pl.BlockSpec((COMBINE_ROWS, nn), lambda i: (i, 0))
    return pl.pallas_call(
        _combine_kernel,
        grid=(steps,),
        in_specs=[pl.BlockSpec((None, 1, TOP_K * COMBINE_ROWS), lambda i: (i, 0, 0), memory_space=pltpu.SMEM),
                  pl.BlockSpec((None, 1, TOP_K * COMBINE_ROWS), lambda i: (jnp.minimum(i + 1, steps - 1), 0, 0),
                               memory_space=pltpu.SMEM),
                  row(D_MODEL), row(LANES), pl.BlockSpec(g_final.shape, lambda i: (0, 0)),
                  pl.BlockSpec(memory_space=pl.ANY)],
        out_specs=row(D_MODEL),
        out_shape=jax.ShapeDtypeStruct((t, D_MODEL), F32),
        scratch_shapes=[pltpu.VMEM((2, TOP_K, COMBINE_ROWS, D_MODEL), F32), pltpu.SemaphoreType.DMA((2,))],
        compiler_params=_cparams(("arbitrary",)),
        name="moe_combine",
    )(pos2, pos2, x1, wts, g_final, y_sorted)


QPAD = 8


def _select_blocks_rows(imp_sel, cur, nsel):
    tq = imp_sel.shape[0]
    blk = lax.broadcasted_iota(jnp.int32, (tq, LANES), 1)
    forced = (blk == 0) | (blk == cur) | (blk == cur - 1)
    score = jnp.where(blk <= cur, imp_sel + jnp.where(forced, FORCE_BONUS, 0.0), -1e9)
    cnt = jnp.zeros((tq, LANES), jnp.int32)
    for k in range(nsel):
        sk = score[:, k:k + 1]
        beats = (sk > score) | ((sk == score) & (blk > k))
        cnt = cnt + jnp.where(beats, 1, 0)
    return jnp.where((cnt < TOP_N) & (blk < nsel), 1.0, 0.0)


def _pad_rows(x, n):
    return jnp.concatenate([x, jnp.zeros((n - x.shape[0], x.shape[1]), x.dtype)], axis=0)


def _nsa_s_kernel(pt_ref, qc_ref, qr_ref, xnew_ref, snew_ref, w1_ref, w2_ref, m_ref, *rest, n_pages, past, dec):
    cmp_pages = rest[:n_pages]
    slc_pages = rest[n_pages:2 * n_pages]
    ocmp_ref, oslc_ref = rest[2 * n_pages:]
    tq = QPAD
    nc = n_pages * (PAGE_SIZE // CMP_STRIDE)
    nsel = past // SEL_BLK + 1
    trow = lax.broadcasted_iota(jnp.int32, (NSA_REP * tq, 1), 0) & (tq - 1)
    rpos = past + jnp.minimum(trow, dec - 1)
    qpos = past + jnp.minimum(lax.broadcasted_iota(jnp.int32, (tq, 1), 0), dec - 1)
    xb = jnp.concatenate([p[...] for p in cmp_pages], axis=0).astype(BF16)
    y1 = jnp.dot(xb, w1_ref[...], preferred_element_type=F32)
    y2 = jnp.dot(xb, w2_ref[...], preferred_element_type=F32)
    xn = jnp.broadcast_to(xnew_ref[...], (8, xnew_ref.shape[1])).astype(BF16)
    ynew = jnp.dot(xn, w2_ref[...], preferred_element_type=F32)[0:1]
    rid = lax.broadcasted_iota(jnp.int32, y2.shape, 0)
    kcv = y1 + jnp.where(rid < nc - 1, pltpu.roll(y2, nc - 1, 0), ynew)
    kc = kcv[:, :KV_W].astype(BF16)
    vc = kcv[:, KV_W:].astype(BF16)
    lane = lax.broadcasted_iota(jnp.int32, (tq, LANES), 1)
    o_c, o_s = [], []
    for g in range(NSA_KV):
        qcg = _stack_group(qc_ref, g, tq)
        qrg = _stack_group(qr_ref, g, tq)
        s = _dot_nt(qcg, kc)
        cidx = lax.broadcasted_iota(jnp.int32, s.shape, 1)
        cmask = cidx * CMP_STRIDE + (CMP_BLK - 1) <= rpos
        s = jnp.where(cmask, s, -1e30)
        e = jnp.where(cmask, jnp.exp(s - jnp.max(s, axis=-1, keepdims=True)), 0.0)
        p = e / jnp.maximum(jnp.sum(e, axis=-1, keepdims=True), 1e-30)
        o_c.append(jnp.dot(p.astype(BF16), vc, preferred_element_type=F32))
        imp = p[0:tq] + p[tq:2 * tq] + p[2 * tq:3 * tq] + p[3 * tq:4 * tq]
        sel = _select_blocks_rows(_dot3(imp, m_ref[...]), qpos >> 6, nsel)
        scores, valids, vals = [], [], []
        for j in range(n_pages):
            page = slc_pages[j]
            bm = jnp.where(lane < SEL_BLK, sel[:, 2 * j:2 * j + 1], sel[:, 2 * j + 1:2 * j + 2])
            valids.append(jnp.concatenate([bm] * NSA_REP, axis=0) > 0.5)
            scores.append(_dot_nt(qrg, page[:, :KV_W].astype(BF16)))
            vals.append(page[:, KV_W:].astype(BF16))
        snew = snew_ref[...]
        bm = jnp.broadcast_to(sel[:, nsel - 1:nsel], (tq, LANES))
        lane4 = lax.broadcasted_iota(jnp.int32, (NSA_REP * tq, LANES), 1)
        valids.append((jnp.concatenate([bm] * NSA_REP, axis=0) > 0.5) & (lane4 < dec) & (past + lane4 <= rpos))
        scores.append(_dot_nt(qrg, _pad_rows(snew[:, :KV_W], LANES).astype(BF16)))
        vals.append(_pad_rows(snew[:, KV_W:], LANES).astype(BF16))
        scores = [jnp.where(v, sc, -1e30) for sc, v in zip(scores, valids)]
        mx = functools.reduce(jnp.maximum, [jnp.max(sc, axis=-1, keepdims=True) for sc in scores])
        l = jnp.zeros((NSA_REP * tq, 1), F32)
        acc = jnp.zeros((NSA_REP * tq, LANES), F32)
        for sc, v, vv in zip(scores, valids, vals):
            a = jnp.where(v, jnp.exp(sc - mx), 0.0)
            l = l + jnp.sum(a, axis=-1, keepdims=True)
            acc = acc + jnp.dot(a.astype(BF16), vv, preferred_element_type=F32)
        o_s.append(acc / jnp.maximum(l, 1e-30))
    ocmp_ref[...] = _unstack_pairs(o_c[0], o_c[1], tq)
    oslc_ref[...] = _unstack_pairs(o_s[0], o_s[1], tq)


def _nsa_s(page_table, qc, qr, xnew, snew, w1, w2, cache_cmp, cache_slc, past, dec):
    bd, n_pages = page_table.shape
    cpp = PAGE_SIZE // CMP_STRIDE
    cmp_pages = cache_cmp.reshape(cache_cmp.shape[0], cpp, CMP_STRIDE * 2 * KV_W)
    m = _imp_matrix(n_pages * cpp, past // SEL_BLK + 1)
    per_b = lambda a: pl.BlockSpec((None,) + a.shape[1:], lambda b, pt: (b,) + (0,) * (a.ndim - 1))
    full = lambda a: pl.BlockSpec(a.shape, lambda b, pt: (0,) * a.ndim)
    page = lambda a, j: pl.BlockSpec((None,) + a.shape[1:], lambda b, pt: (pt[b, j], 0, 0))
    ospec = pl.BlockSpec((None, QPAD, NSA_W), lambda b, pt: (b, 0, 0))
    return pl.pallas_call(
        functools.partial(_nsa_s_kernel, n_pages=n_pages, past=past, dec=dec),
        grid_spec=pltpu.PrefetchScalarGridSpec(
            num_scalar_prefetch=1,
            grid=(bd,),
            in_specs=[per_b(qc), per_b(qr), per_b(xnew), per_b(snew), full(w1), full(w2), full(m)]
            + [page(cmp_pages, j) for j in range(n_pages)] + [page(cache_slc, j) for j in range(n_pages)],
            out_specs=[ospec, ospec],
        ),
        out_shape=[jax.ShapeDtypeStruct((bd, QPAD, NSA_W), F32)] * 2,
        compiler_params=_cparams(("arbitrary",)),
        name="nsa_sample",
    )(page_table, qc, qr, xnew, snew, w1, w2, m, *([cmp_pages] * n_pages), *([cache_slc] * n_pages))


def _window_s_kernel(qr_ref, st_ref, wnew_ref, o_ref, *, past, dec):
    tq = QPAD
    wbuf = st_ref.shape[0]
    trow = jnp.minimum(lax.broadcasted_iota(jnp.int32, (NSA_REP * tq, 1), 0) & (tq - 1), dec - 1)
    k = st_ref[:, :KV_W].astype(BF16)
    v = st_ref[:, KV_W:].astype(BF16)
    wnew = wnew_ref[...]
    kn = _pad_rows(wnew[:, :KV_W], LANES).astype(BF16)
    vn = _pad_rows(wnew[:, KV_W:], LANES).astype(BF16)
    outs = []
    for g in range(NSA_KV):
        qrg = _stack_group(qr_ref, g, tq)
        s0 = _dot_nt(qrg, k)
        d0 = wbuf + trow - lax.broadcasted_iota(jnp.int32, s0.shape, 1)
        m0 = (d0 >= 0) & (d0 < WINDOW)
        s1 = _dot_nt(qrg, kn)
        i1 = lax.broadcasted_iota(jnp.int32, s1.shape, 1)
        m1 = (i1 < dec) & (trow - i1 >= 0) & (trow - i1 < WINDOW)
        s0 = jnp.where(m0, s0, -1e30)
        s1 = jnp.where(m1, s1, -1e30)
        mx = jnp.maximum(jnp.max(s0, axis=-1, keepdims=True), jnp.max(s1, axis=-1, keepdims=True))
        e0 = jnp.where(m0, jnp.exp(s0 - mx), 0.0)
        e1 = jnp.where(m1, jnp.exp(s1 - mx), 0.0)
        den = jnp.maximum(jnp.sum(e0, axis=-1, keepdims=True) + jnp.sum(e1, axis=-1, keepdims=True), 1e-30)
        o = (jnp.dot(e0.astype(BF16), v, preferred_element_type=F32)
             + jnp.dot(e1.astype(BF16), vn, preferred_element_type=F32))
        outs.append(o / den)
    o_ref[...] = _unstack_pairs(outs[0], outs[1], tq)


def _window_s(qr, state, wnew, past, dec):
    bd = qr.shape[0]
    per_b = lambda a: pl.BlockSpec((None,) + a.shape[1:], lambda b: (b,) + (0,) * (a.ndim - 1))
    return pl.pallas_call(
        functools.partial(_window_s_kernel, past=past, dec=dec),
        grid=(bd,),
        in_specs=[per_b(qr), per_b(state), per_b(wnew)],
        out_specs=pl.BlockSpec((None, QPAD, NSA_W), lambda b: (b, 0, 0)),
        out_shape=jax.ShapeDtypeStruct((bd, QPAD, NSA_W), F32),
        compiler_params=_cparams(("parallel",)),
        name="window_sample",
    )(qr, state, wnew)


def _sb_s_kernel(pt_ref, q_ref, new_ref, tri_ref, *rest, n_pages):
    pages = rest[:n_pages]
    o_ref = rest[n_pages]
    tq = QPAD
    nrow = SB_HEADS * tq
    row = lax.broadcasted_iota(jnp.int32, (nrow, SB_W), 0)
    col = lax.broadcasted_iota(jnp.int32, (nrow, SB_W), 1)
    own = (row >> 3) == (col >> 6)
    qbd = jnp.where(own, jnp.concatenate([q_ref[...]] * SB_HEADS, axis=0), 0.0).astype(BF16)
    tri = tri_ref[...]
    new = new_ref[...]
    kn = _pad_rows(new[:, :SB_W], PAGE_SIZE).astype(BF16)
    vn = _pad_rows(new[:, SB_W:], PAGE_SIZE).astype(BF16)
    lane = lax.broadcasted_iota(jnp.int32, (nrow, PAGE_SIZE), 1)
    trow = lax.broadcasted_iota(jnp.int32, (nrow, PAGE_SIZE), 0) & (tq - 1)
    carry, acc = _sb_block(qbd, kn, vn, tri, jnp.zeros((nrow, 1), F32), jnp.zeros((nrow, SB_W), F32), lane < trow)
    for j in reversed(range(n_pages)):
        pg = pages[j]
        carry, acc = _sb_block(qbd, pg[:, :SB_W].astype(BF16), pg[:, SB_W:].astype(BF16), tri, carry, acc, None)
    acc = jnp.where(own, acc, 0.0)
    out = acc[0:tq]
    for h in range(1, SB_HEADS):
        out = out + acc[h * tq:(h + 1) * tq]
    o_ref[...] = out


def _sb_s(page_table, sbq, sbnew, cache_sb):
    bd, n_pages = page_table.shape
    tri = _tri_matrix(PAGE_SIZE)
    per_b = lambda a: pl.BlockSpec((None,) + a.shape[1:], lambda b, pt: (b,) + (0,) * (a.ndim - 1))
    page = lambda j: pl.BlockSpec((None,) + cache_sb.shape[1:], lambda b, pt: (pt[b, j], 0, 0))
    return pl.pallas_call(
        functools.partial(_sb_s_kernel, n_pages=n_pages),
        grid_spec=pltpu.PrefetchScalarGridSpec(
            num_scalar_prefetch=1,
            grid=(bd,),
            in_specs=[per_b(sbq), per_b(sbnew), pl.BlockSpec(tri.shape, lambda b, pt: (0, 0))]
            + [page(j) for j in range(n_pages)],
            out_specs=pl.BlockSpec((None, QPAD, SB_W), lambda b, pt: (b, 0, 0)),
        ),
        out_shape=jax.ShapeDtypeStruct((bd, QPAD, SB_W), F32),
        compiler_params=_cparams(("arbitrary",)),
        name="sb_sample",
    )(page_table, sbq, sbnew, tri, *([cache_sb] * n_pages))


def kernel(x_prompt, x_sample, cache_nsa_cmp, cache_nsa_slc, cache_sb, state_nsa_win, page_table, g_mix, w_in, w_ck,
           w_cv, w_out, g_ffn, w_router, b_router, w_gu, b_gu, w_dn, b_dn, g_final):
    b, s, d = x_prompt.shape
    bd, ds, _ = x_sample.shape
    depth = g_mix.shape[0]
    assert depth == 1 and d == D_MODEL and ds <= QPAD
    n_pages = page_table.shape[1]
    past = n_pages * PAGE_SIZE
    wbuf = state_nsa_win.shape[2]
    n_phys = cache_nsa_cmp.shape[1]
    tp, ts = b * s, bd * ds

    w_perm = _prep_w_in(w_in[0])
    w1, w2 = _prep_w_cmp(w_ck[0], w_cv[0])
    prep = _prep_merge(w_out[0], w_router[0], b_router[0])
    g_mix2, g_ffn2, g_fin2 = g_mix[0].reshape(1, d), g_ffn[0].reshape(1, d), g_final.reshape(1, d)
    wg = w_gu[0][:, :, 0::2].astype(BF16)
    wu = w_gu[0][:, :, 1::2].astype(BF16)
    bg = b_gu[0][:, None, 0::2]
    bu = b_gu[0][:, None, 1::2]
    wd = w_dn[0].astype(BF16)
    bdn = b_dn[0][:, None, :]

    pos_p = jnp.arange(s, dtype=jnp.int32)
    tabs_p = [jnp.tile(t, (b, 1)) for t in _rope_tables(pos_p)]
    (qc, qr, sbq, cmp_p, slc_p, win_p, sbkv_p, kslc, vslc, kwin, vwin, sbk, sbv, gate_p, br_p) = _proj(
        x_prompt.reshape(tp, d), g_mix2, w_perm, tabs_p, 256)
    kc, vc = _compress(cmp_p, w1, w2, b, s)
    o_cmp, o_slc = _nsa(qc, qr, kc, vc, kslc, vslc, b, s)
    o_win = _window(qr, kwin, vwin, b, s)
    o_sb = _sb(sbq, sbk, sbv, b, s)
    x1_p, hp_p, idx_p, wts_p = _merge(x_prompt.reshape(tp, d), o_cmp, o_slc, o_win, o_sb, gate_p, br_p, prep, g_ffn2)

    pos_s = past + jnp.arange(ds, dtype=jnp.int32)
    tabs_s = [jnp.tile(t, (bd, 1)) for t in _rope_tables(pos_s)]
    (qc_s, qr_s, sbq_s, cmp_s, slc_s, win_s, sbkv_s, _, _, _, _, _, _, gate_s, br_s) = _proj(
        x_sample.reshape(ts, d), g_mix2, w_perm, tabs_s, min(256, ts))
    pad_q = lambda a: jnp.pad(a.astype(F32).reshape(bd, ds, a.shape[1]), ((0, 0), (0, QPAD - ds), (0, 0)))
    xnew = jnp.pad(cmp_s.reshape(bd, 1, ds * 2 * KV_W), ((0, 0), (0, 0), (0, (CMP_STRIDE - ds) * 2 * KV_W)))
    o_cmp_s, o_slc_s = _nsa_s(page_table, pad_q(qc_s), pad_q(qr_s), xnew, pad_q(slc_s), w1, w2,
                              cache_nsa_cmp[0].reshape(n_phys, PAGE_SIZE, 2 * KV_W),
                              cache_nsa_slc[0].reshape(n_phys, PAGE_SIZE, 2 * KV_W), past, ds)
    o_win_s = _window_s(pad_q(qr_s), state_nsa_win[0].reshape(bd, wbuf, 2 * KV_W), pad_q(win_s), past, ds)
    o_sb_s = _sb_s(page_table, pad_q(sbq_s), pad_q(sbkv_s), cache_sb[0].reshape(n_phys, PAGE_SIZE, 2 * SB_W))
    unpad = lambda a: a[:, :ds].reshape(ts, a.shape[2])
    x1_s, hp_s, idx_s, wts_s = _merge(x_sample.reshape(ts, d), unpad(o_cmp_s), unpad(o_slc_s), unpad(o_win_s),
                                      unpad(o_sb_s), gate_s, br_s, prep, g_ffn2)

    hp = jnp.concatenate([hp_p, hp_s], axis=0)
    idx = jnp.concatenate([idx_p[:, :TOP_K], idx_s[:, :TOP_K]], axis=0)
    row_token, pos, tile_expert, n_used = _route_tables(idx)
    hs = _gather_rows(hp, row_token)
    y_sorted = _experts(hs, tile_expert, n_used, wg, wu, bg, bu, wd, bdn)
    y_p = _combine(x1_p, wts_p, pos[:tp], y_sorted, g_fin2)
    y_s = _combine(x1_s, wts_s, pos[tp:], y_sorted, g_fin2)

    win_all = jnp.concatenate([state_nsa_win[0], win_s.reshape(bd, ds, 2, NSA_KV, HEAD_DIM)], axis=1)
    lead = lambda a, n, h: a.reshape(1, n, -1, 2, h, HEAD_DIM)
    return (y_p.reshape(b, s, d), y_s.reshape(bd, ds, d),
            lead(cmp_p, b, NSA_KV), lead(slc_p, b, NSA_KV), lead(sbkv_p, b, SB_HEADS),
            lead(win_p, b, NSA_KV)[:, :, -wbuf:] if s >= wbuf else
            jnp.pad(lead(win_p, b, NSA_KV), ((0, 0), (0, 0), (wbuf - s, 0), (0, 0), (0, 0), (0, 0))),
            lead(cmp_s, bd, NSA_KV), lead(slc_s, bd, NSA_KV), lead(sbkv_s, bd, SB_HEADS),
            win_all[None, :, ds:])
```

```python
import functools

import numpy as np
import jax
import jax.numpy as jnp
from jax import lax
from jax.experimental import pallas as pl
from jax.experimental.pallas import tpu as pltpu

D_MODEL = 1024
HEAD_DIM = 64
NSA_HEADS = 8
NSA_KV = 2
NSA_REP = NSA_HEADS // NSA_KV
SB_HEADS = 8
NSA_W = NSA_HEADS * HEAD_DIM
SB_W = SB_HEADS * HEAD_DIM
KV_W = NSA_KV * HEAD_DIM
CMP_BLK = 32
CMP_STRIDE = 16
SEL_BLK = 64
TOP_N = 16
WINDOW = 512
FORCE_BONUS = 1.0e4
ROPE_THETA = 500000.0
ROPE_DIMS = HEAD_DIM // 4
N_EXPERTS = 32
TOP_K = 4
D_FF = D_MODEL
SWIGLU_LIMIT = 7.0
SWIGLU_ALPHA = 1.702
RMS_EPS = 1e-5
PAGE_SIZE = 128
SCALE = HEAD_DIM ** -0.5

LANES = 128
VMEM_LIMIT_BYTES = 56 * 1024 * 1024

BF16 = jnp.bfloat16
F32 = jnp.float32

_OFF = dict(q=0, k_cmp=512, v_cmp=640, k_slc=768, v_slc=896, k_win=1024, v_win=1152, br=1280,
            sb_q=1304, sb_k=1816, sb_v=2328, gate_a=2840, gate_b=3352, end=3864)
_HEAD_PERM = [j + 4 * half for j in range(4) for half in range(2)]
_C_Q, _C_SBQ, _C_CMP, _C_SLC, _C_WIN, _C_SBKV, _C_GATE, _C_BR, _NP = 0, 512, 1024, 1280, 1536, 1792, 2816, 3840, 3968


def _cparams(sem):
    return pltpu.CompilerParams(dimension_semantics=sem, vmem_limit_bytes=VMEM_LIMIT_BYTES)


def _prep_w_in(w):
    def head_perm(base):
        return [w[:, base + h * HEAD_DIM: base + (h + 1) * HEAD_DIM] for h in _HEAD_PERM]
    parts = [c * SCALE for c in head_perm(_OFF['q'])]
    parts.append(w[:, _OFF['sb_q']:_OFF['sb_k']] * SCALE)
    parts.append(w[:, _OFF['k_cmp']:_OFF['br']])
    parts.append(w[:, _OFF['sb_k']:_OFF['gate_a']])
    parts.extend(head_perm(_OFF['gate_a']))
    parts.append(w[:, _OFF['gate_b']:_OFF['end']])
    parts.append(w[:, _OFF['br']:_OFF['sb_q']])
    parts.append(jnp.zeros((w.shape[0], LANES - 3 * NSA_HEADS), w.dtype))
    return jnp.concatenate(parts, axis=1).astype(BF16)


def _rope_tables(pos):
    half = ROPE_DIMS // 2
    inv = ROPE_THETA ** (-jnp.arange(half, dtype=F32) / half)
    ang = pos.astype(F32)[:, None] * inv[None, :]
    cos = jnp.cos(ang)
    sin = jnp.sin(ang)
    n = pos.shape[0]
    one = jnp.ones((n, HEAD_DIM - ROPE_DIMS), F32)
    zero = jnp.zeros((n, HEAD_DIM - ROPE_DIMS), F32)
    z8 = jnp.zeros((n, half), F32)
    c = jnp.concatenate([cos, cos, one], axis=1)
    s1 = jnp.concatenate([-sin, z8, zero], axis=1)
    s2 = jnp.concatenate([z8, sin, zero], axis=1)
    tile2 = lambda t: jnp.concatenate([t, t], axis=1)
    return tile2(c), tile2(s1), tile2(s2)


def _rope_apply(x, c, s1, s2):
    outs = []
    for j in range(x.shape[1] // LANES):
        xb = x[:, j * LANES:(j + 1) * LANES]
        outs.append(xb * c + pltpu.roll(xb, LANES - 8, 1) * s1 + pltpu.roll(xb, 8, 1) * s2)
    return outs[0] if len(outs) == 1 else jnp.concatenate(outs, axis=1)


def _proj_kernel(x_ref, g_ref, w_ref, c_ref, s1_ref, s2_ref,
                 qc_ref, qr_ref, sbq_ref, cmp_ref, slc_ref, win_ref, sbkv_ref,
                 kslc_ref, vslc_ref, kwin_ref, vwin_ref, sbk_ref, sbv_ref, gate_ref, br_ref):
    x = x_ref[...]
    ms = jnp.mean(x * x, axis=-1, keepdims=True)
    h = (x * lax.rsqrt(ms + RMS_EPS)) * g_ref[...]
    hb = h.astype(BF16)
    c, s1, s2 = c_ref[...], s1_ref[...], s2_ref[...]

    def mm(c0, c1):
        return jnp.dot(hb, w_ref[:, c0:c1], preferred_element_type=F32)

    q = mm(_C_Q, _C_SBQ)
    qc_ref[...] = q.astype(BF16)
    qr_ref[...] = _rope_apply(q, c, s1, s2).astype(BF16)
    sbq_ref[...] = mm(_C_SBQ, _C_CMP).astype(BF16)
    cmp_ref[...] = mm(_C_CMP, _C_SLC)
    for c0, cache_ref, k_ref, v_ref in ((_C_SLC, slc_ref, kslc_ref, vslc_ref), (_C_WIN, win_ref, kwin_ref, vwin_ref)):
        z = mm(c0, c0 + 2 * KV_W)
        k = _rope_apply(z[:, :KV_W], c, s1, s2)
        v = z[:, KV_W:]
        cache_ref[:, :KV_W] = k
        cache_ref[:, KV_W:] = v
        k_ref[...] = k.astype(BF16)
        v_ref[...] = v.astype(BF16)
    z = mm(_C_SBKV, _C_GATE)
    sbkv_ref[...] = z
    sbk_ref[...] = z[:, :SB_W].astype(BF16)
    sbv_ref[...] = z[:, SB_W:].astype(BF16)
    gate_ref[...] = jax.nn.sigmoid(mm(_C_GATE, _C_BR)).astype(BF16)
    br_ref[...] = jax.nn.sigmoid(mm(_C_BR, _NP))


def _proj(x2d, g, w_perm, tabs, tm):
    t = x2d.shape[0]
    tab_blocks = tabs[0].shape[0] // tm
    assert t % tm == 0 and tabs[0].shape[0] % tm == 0
    row = lambda n: pl.BlockSpec((tm, n), lambda i: (i, 0))
    tab = pl.BlockSpec((tm, LANES), lambda i: (i % tab_blocks, 0))
    full = lambda a: pl.BlockSpec(a.shape, lambda i: (0,) * a.ndim)
    outs = [(512, BF16), (512, BF16), (512, BF16), (256, F32), (256, F32), (256, F32), (1024, F32),
            (128, BF16), (128, BF16), (128, BF16), (128, BF16), (512, BF16), (512, BF16), (1024, BF16), (128, F32)]
    return pl.pallas_call(
        _proj_kernel,
        grid=(t // tm,),
        in_specs=[row(D_MODEL), full(g), full(w_perm), tab, tab, tab],
        out_specs=[row(n) for n, _ in outs],
        out_shape=[jax.ShapeDtypeStruct((t, n), dt) for n, dt in outs],
        compiler_params=_cparams(("parallel",)),
        name="proj",
    )(x2d, g, w_perm, *tabs)


def _prep_w_cmp(w_ck, w_cv):
    eye2 = jnp.eye(NSA_KV, dtype=w_ck.dtype)
    bd = lambda w: jnp.einsum('ab,ide->iadbe', eye2, w).reshape(CMP_BLK, KV_W, KV_W)
    z = jnp.zeros((CMP_BLK, KV_W, KV_W), w_ck.dtype)
    w_all = jnp.concatenate([jnp.concatenate([bd(w_ck), z], axis=2),
                             jnp.concatenate([z, bd(w_cv)], axis=2)], axis=1)
    w_all = w_all.astype(BF16)
    return (w_all[:CMP_STRIDE].reshape(CMP_STRIDE * 2 * KV_W, 2 * KV_W),
            w_all[CMP_STRIDE:].reshape(CMP_STRIDE * 2 * KV_W, 2 * KV_W))


def _compress_rows(xb, w1, w2):
    n = xb.shape[0]
    y1 = jnp.dot(xb, w1, preferred_element_type=F32)
    y2 = jnp.dot(xb, w2, preferred_element_type=F32)
    y2s = pltpu.roll(y2, n - 1, 0)
    rid = lax.broadcasted_iota(jnp.int32, y2.shape, 0)
    return y1 + jnp.where(rid < n - 1, y2s, 0.0)


def _compress_kernel(x_ref, w1_ref, w2_ref, k_ref, v_ref):
    out = _compress_rows(x_ref[...].astype(BF16), w1_ref[...], w2_ref[...])
    k_ref[...] = out[:, :KV_W].astype(BF16)
    v_ref[...] = out[:, KV_W:].astype(BF16)


def _compress(cache2d, w1, w2, b, s):
    nc = s // CMP_STRIDE
    x = cache2d.reshape(b, nc, CMP_STRIDE * 2 * KV_W)
    full = lambda a: pl.BlockSpec(a.shape, lambda i: (0,) * a.ndim)
    blk = pl.BlockSpec((None, nc, KV_W), lambda i: (i, 0, 0))
    return pl.pallas_call(
        _compress_kernel,
        grid=(b,),
        in_specs=[pl.BlockSpec((None, nc, x.shape[2]), lambda i: (i, 0, 0)), full(w1), full(w2)],
        out_specs=[blk, blk],
        out_shape=[jax.ShapeDtypeStruct((b, nc, KV_W), BF16)] * 2,
        compiler_params=_cparams(("parallel",)),
        name="compress",
    )(x, w1, w2)


_NT = (((1,), (1,)), ((), ()))


def _dot_nt(a, b):
    return lax.dot_general(a, b, _NT, preferred_element_type=F32)


def _dot3(x, m):
    hi = x.astype(BF16)
    r1 = x - hi.astype(F32)
    mid = r1.astype(BF16)
    lo = (r1 - mid.astype(F32)).astype(BF16)
    d = lambda a: jnp.dot(a, m, preferred_element_type=F32)
    return d(hi) + d(mid) + d(lo)


def _stack_group(ref, g, tq):
    half = lax.broadcasted_iota(jnp.int32, (tq, LANES), 1) >> 6
    return jnp.concatenate([jnp.where(half == g, ref[:, j * LANES:(j + 1) * LANES], 0)
                            for j in range(NSA_REP)], axis=0).astype(BF16)


def _unstack_pairs(o0, o1, tq):
    half = lax.broadcasted_iota(jnp.int32, (tq, LANES), 1) >> 6
    return jnp.concatenate([jnp.where(half == 0, o0[j * tq:(j + 1) * tq], o1[j * tq:(j + 1) * tq])
                            for j in range(NSA_REP)], axis=1)


def _imp_matrix(nc, nsel=None):
    m = np.zeros((nc, LANES), np.float32)
    for j in range(nc * CMP_STRIDE // SEL_BLK if nsel is None else nsel):
        for c, wgt in ((4 * j - 1, 0.5), (4 * j, 1.0), (4 * j + 1, 1.0), (4 * j + 2, 1.0), (4 * j + 3, 0.5)):
            if 0 <= c < nc:
                m[c, j] += wgt
    return jnp.asarray(m, BF16)


def _select_blocks(imp_sel, cur, nsel):
    tq = imp_sel.shape[0]
    blk = lax.broadcasted_iota(jnp.int32, (tq, LANES), 1)
    forced = (blk == 0) | (blk == cur) | (blk == cur - 1)
    score = jnp.where(blk <= cur, imp_sel + jnp.where(forced, FORCE_BONUS, 0.0), -1e9)
    st = score.T
    nslab = -(-nsel // 8)
    slabs = [st[8 * v:8 * v + 8] for v in range(nslab)]
    sub = lax.broadcasted_iota(jnp.int32, (8, tq), 0)
    cnts = [jnp.zeros((8, tq), jnp.int32) for _ in range(nslab)]
    for k in range(nsel):
        sk = st[k:k + 1, :]
        for v in range(nslab):
            if 8 * v + 7 < k:
                inc = jnp.where(sk > slabs[v], 1, 0)
            elif 8 * v > k:
                inc = jnp.where(sk >= slabs[v], 1, 0)
            else:
                inc = jnp.where(sub + 8 * v > k, jnp.where(sk >= slabs[v], 1, 0), jnp.where(sk > slabs[v], 1, 0))
            cnts[v] = cnts[v] + inc
    parts = [jnp.where((c < TOP_N) & (sub + 8 * v < nsel), 1.0, 0.0) for v, c in enumerate(cnts)]
    if 8 * nslab < LANES:
        parts.append(jnp.zeros((LANES - 8 * nslab, tq), F32))
    return jnp.concatenate(parts, axis=0).T.astype(BF16)


def _nsa_kernel(qc_ref, qr_ref, kc_ref, vc_ref, ks_ref, vs_ref, m_ref, ocmp_ref, oslc_ref, *, tq, kt):
    q0 = pl.program_id(1) * tq
    nc = kc_ref.shape[0]
    nsel = nc * CMP_STRIDE // SEL_BLK
    qpos = q0 + lax.broadcasted_iota(jnp.int32, (tq, 1), 0)
    tile4 = lambda a: jnp.concatenate([a] * NSA_REP, axis=0)
    cidx = lax.broadcasted_iota(jnp.int32, (tq, nc), 1)
    cbias = tile4(jnp.where(cidx * CMP_STRIDE + (CMP_BLK - 1) <= qpos, 0.0, -1e30))
    row_ok = tile4(jnp.where(qpos >= CMP_BLK - 1, 1.0, 0.0))
    kcol = lax.broadcasted_iota(jnp.int32, (tq, kt), 1)
    eblk = (lax.broadcasted_iota(jnp.int32, (LANES, kt), 0)
            - (lax.broadcasted_iota(jnp.int32, (LANES, kt), 1) >> 6))
    n_full = q0 // kt
    o_c, o_s = [], []
    for g in range(NSA_KV):
        qcg = _stack_group(qc_ref, g, tq)
        qrg = _stack_group(qr_ref, g, tq)
        s = _dot_nt(qcg, kc_ref[...]) + cbias
        e = jnp.exp(s - jnp.max(s, axis=-1, keepdims=True))
        p = e * (row_ok / jnp.maximum(jnp.sum(e, axis=-1, keepdims=True), 1e-30))
        o_c.append(jnp.dot(p.astype(BF16), vc_ref[...], preferred_element_type=F32))
        imp = p[0:tq] + p[tq:2 * tq] + p[2 * tq:3 * tq] + p[3 * tq:4 * tq]
        sel = _select_blocks(_dot3(imp, m_ref[...]), qpos >> 6, nsel)

        def tile_step(t, carry, causal):
            m, l, acc = carry
            k0 = pl.multiple_of(t * kt, kt)
            sc = _dot_nt(qrg, ks_ref[pl.ds(k0, kt), :])
            expand = jnp.where(eblk == t * (kt // SEL_BLK), 1.0, 0.0).astype(BF16)
            bias = (jnp.dot(sel, expand, preferred_element_type=F32) - 1.0) * 1e30
            if causal:
                bias = jnp.where(k0 + kcol <= qpos, bias, -1e30)
            sc = sc + tile4(bias)
            m_new = jnp.maximum(m, jnp.max(sc, axis=-1, keepdims=True))
            a = jnp.exp(sc - m_new)
            alpha = jnp.exp(m - m_new)
            l = alpha * l + jnp.sum(a, axis=-1, keepdims=True)
            acc = alpha * acc + jnp.dot(a.astype(BF16), vs_ref[pl.ds(k0, kt), :], preferred_element_type=F32)
            return m_new, l, acc

        init = (jnp.full((NSA_REP * tq, 1), -1e30, F32), jnp.zeros((NSA_REP * tq, 1), F32),
                jnp.zeros((NSA_REP * tq, LANES), F32))
        carry = lax.fori_loop(0, n_full, lambda t, c: tile_step(t, c, False), init)
        _, l, acc = tile_step(n_full, carry, True)
        o_s.append(acc / jnp.maximum(l, 1e-30))
    ocmp_ref[...] = _unstack_pairs(o_c[0], o_c[1], tq)
    oslc_ref[...] = _unstack_pairs(o_s[0], o_s[1], tq)


def _nsa(qc, qr, kc, vc, ks, vs, b, s, tq=128, kt=512):
    nqb = s // tq
    kt = min(kt, s)
    m = _imp_matrix(s // CMP_STRIDE)
    qspec = pl.BlockSpec((tq, NSA_W), lambda bi, i: (bi * nqb + i, 0))
    cspec = pl.BlockSpec((None, s // CMP_STRIDE, KV_W), lambda bi, i: (bi, 0, 0))
    kspec = pl.BlockSpec((s, KV_W), lambda bi, i: (bi, 0))
    return pl.pallas_call(
        functools.partial(_nsa_kernel, tq=tq, kt=kt),
        grid=(b, nqb),
        in_specs=[qspec, qspec, cspec, cspec, kspec, kspec, pl.BlockSpec(m.shape, lambda bi, i: (0, 0))],
        out_specs=[qspec, qspec],
        out_shape=[jax.ShapeDtypeStruct((b * s, NSA_W), F32)] * 2,
        compiler_params=_cparams(("parallel", "parallel")),
        name="nsa",
    )(qc, qr, kc, vc, ks, vs, m)


def _window_kernel(qr_ref, k_ref, v_ref, o_ref, *, tq, span):
    q0 = pl.program_id(1) * tq
    s_len = k_ref.shape[0]
    start = pl.multiple_of(jnp.minimum(jnp.maximum(q0 - WINDOW, 0), s_len - span), tq)
    rpos = q0 + (lax.broadcasted_iota(jnp.int32, (NSA_REP * tq, 1), 0) & (tq - 1))
    k = k_ref[pl.ds(start, span), :]
    v = v_ref[pl.ds(start, span), :]
    outs = []
    for g in range(NSA_KV):
        qrg = _stack_group(qr_ref, g, tq)
        s = _dot_nt(qrg, k)
        diff = rpos - (start + lax.broadcasted_iota(jnp.int32, s.shape, 1))
        mask = (diff >= 0) & (diff < WINDOW)
        s = jnp.where(mask, s, -1e30)
        e = jnp.where(mask, jnp.exp(s - jnp.max(s, axis=-1, keepdims=True)), 0.0)
        p = e / jnp.maximum(jnp.sum(e, axis=-1, keepdims=True), 1e-30)
        outs.append(jnp.dot(p.astype(BF16), v, preferred_element_type=F32))
    o_ref[...] = _unstack_pairs(outs[0], outs[1], tq)


def _window(qr, kw, vw, b, s, tq=128):
    nqb = s // tq
    span = min(WINDOW + tq, s)
    qspec = pl.BlockSpec((tq, NSA_W), lambda bi, i: (bi * nqb + i, 0))
    kspec = pl.BlockSpec((s, KV_W), lambda bi, i: (bi, 0))
    return pl.pallas_call(
        functools.partial(_window_kernel, tq=tq, span=span),
        grid=(b, nqb),
        in_specs=[qspec, kspec, kspec],
        out_specs=qspec,
        out_shape=jax.ShapeDtypeStruct((b * s, NSA_W), F32),
        compiler_params=_cparams(("parallel", "parallel")),
        name="window",
    )(qr, kw, vw)


def _softplus(z):
    return jnp.maximum(z, 0.0) + jnp.log(1.0 + jnp.exp(-jnp.abs(z)))


def _tri_matrix(n):
    return jnp.asarray(np.tril(np.ones((n, n), np.float32)), BF16)


def _sb_block(qh, k_t, v_t, tri, carry, acc, valid):
    z = _dot_nt(qh, k_t)
    sp = _softplus(z)
    if valid is not None:
        sp = jnp.where(valid, sp, 0.0)
    cum = jnp.dot(sp.astype(BF16), tri, preferred_element_type=F32) + carry
    a = jnp.exp(z - cum)
    if valid is not None:
        a = jnp.where(valid, a, 0.0)
    acc = acc + jnp.dot(a.astype(BF16), v_t, preferred_element_type=F32)
    return cum[:, 0:1], acc


def _sb_two_blocks(qh, k_ref, v_ref, tri, ja, jb, state, mask_a, ok_b, tq):
    a0 = pl.multiple_of(ja * tq, tq)
    b0 = pl.multiple_of(jb * tq, tq)
    k_a, v_a = k_ref[pl.ds(a0, tq), :], v_ref[pl.ds(a0, tq), :]
    k_b, v_b = k_ref[pl.ds(b0, tq), :], v_ref[pl.ds(b0, tq), :]
    out = []
    for hh in range(2):
        carry, acc = state[2 * hh], state[2 * hh + 1]
        za = _dot_nt(qh[hh], k_a)
        zb = _dot_nt(qh[hh], k_b)
        spa = _softplus(za)
        spb = _softplus(zb)
        if mask_a is not None:
            spa = jnp.where(mask_a, spa, 0.0)
        if ok_b is not None:
            spb = jnp.where(ok_b, spb, 0.0)
        cuma = jnp.dot(spa.astype(BF16), tri, preferred_element_type=F32) + carry
        cumb = jnp.dot(spb.astype(BF16), tri, preferred_element_type=F32) + cuma[:, 0:1]
        aa = jnp.exp(za - cuma)
        ab = jnp.exp(zb - cumb)
        if mask_a is not None:
            aa = jnp.where(mask_a, aa, 0.0)
        if ok_b is not None:
            ab = jnp.where(ok_b, ab, 0.0)
        acc = (acc + jnp.dot(aa.astype(BF16), v_a, preferred_element_type=F32)
               + jnp.dot(ab.astype(BF16), v_b, preferred_element_type=F32))
        out.extend((cumb[:, 0:1], acc))
    return tuple(out)


def _sb_kernel(q_ref, k_ref, v_ref, tri_ref, o_ref, *, tq):
    i = pl.program_id(2)
    half = lax.broadcasted_iota(jnp.int32, (tq, LANES), 1) >> 6
    q = q_ref[...]
    qh = [jnp.where(half == hh, q, 0) for hh in range(2)]
    tri = tri_ref[...]
    odd = (i & 1) == 1
    causal = lax.broadcasted_iota(jnp.int32, (tq, tq), 1) < lax.broadcasted_iota(jnp.int32, (tq, tq), 0)
    zero = (jnp.zeros((tq, 1), F32), jnp.zeros((tq, LANES), F32))
    state = _sb_two_blocks(qh, k_ref, v_ref, tri, i, jnp.maximum(i - 1, 0), zero + zero, causal, odd, tq)
    first = i - 1 - (i & 1)

    def body(p, st):
        ja = first - 2 * p
        return _sb_two_blocks(qh, k_ref, v_ref, tri, ja, ja - 1, st, None, None, tq)

    st = lax.fori_loop(0, i >> 1, body, state)
    o_ref[...] = jnp.where(half == 0, st[1], st[3])


def _sb(sbq, sbk, sbv, b, s, tq=256):
    tq = min(tq, s)
    nqb = s // tq
    tri = _tri_matrix(tq)
    npair = SB_HEADS // 2
    qspec = pl.BlockSpec((tq, LANES), lambda bi, p, i: (bi * nqb + i, p))
    kspec = pl.BlockSpec((s, LANES), lambda bi, p, i: (bi, p))
    return pl.pallas_call(
        functools.partial(_sb_kernel, tq=tq),
        grid=(b, npair, nqb),
        in_specs=[qspec, kspec, kspec, pl.BlockSpec(tri.shape, lambda bi, p, i: (0, 0))],
        out_specs=qspec,
        out_shape=jax.ShapeDtypeStruct((b * s, SB_W), F32),
        compiler_params=_cparams(("parallel", "parallel", "parallel")),
        name="sb",
    )(sbq, sbk, sbv, tri)


def _prep_merge(w_out, w_router, b_router):
    w_a = [w_out[h * HEAD_DIM:(h + 1) * HEAD_DIM] for h in _HEAD_PERM]
    w_out_p = jnp.concatenate(w_a + [w_out[NSA_W:]], axis=0).astype(BF16)
    e = np.zeros((LANES, 3 * NSA_W), np.float32)
    for br in range(3):
        for p, h in enumerate(_HEAD_PERM):
            e[br * NSA_HEADS + h, br * NSA_W + p * HEAD_DIM: br * NSA_W + (p + 1) * HEAD_DIM] = 1.0
    w_r = jnp.concatenate([w_router, jnp.zeros((D_MODEL, LANES - N_EXPERTS), w_router.dtype)], axis=1)
    b_r = jnp.concatenate([b_router, jnp.full((LANES - N_EXPERTS,), -1e30, b_router.dtype)]).reshape(1, LANES)
    return w_out_p, jnp.asarray(e, BF16), w_r, b_r


def _merge_kernel(x_ref, oc_ref, os_ref, ow_ref, ob_ref, gate_ref, br_ref, e_ref, wo_ref, g_ref, wr_ref, brt_ref,
                  x1_ref, hp_ref, idx_ref, wts_ref):
    gexp = _dot3(br_ref[...], e_ref[...])
    o_a = gexp[:, :NSA_W] * oc_ref[...] + gexp[:, NSA_W:2 * NSA_W] * os_ref[...] + gexp[:, 2 * NSA_W:] * ow_ref[...]
    gate = gate_ref[...].astype(F32)
    cat = jnp.concatenate([gate[:, :NSA_W] * o_a, gate[:, NSA_W:] * ob_ref[...]], axis=1).astype(BF16)
    x1 = x_ref[...] + jnp.dot(cat, wo_ref[...], preferred_element_type=F32)
    x1_ref[...] = x1
    ms = jnp.mean(x1 * x1, axis=-1, keepdims=True)
    h = (x1 * lax.rsqrt(ms + RMS_EPS)) * g_ref[...]
    hb = h.astype(BF16).astype(F32)
    for j in range(D_MODEL // LANES):
        hp_ref[:, j, :] = hb[:, j * LANES:(j + 1) * LANES]
    lg = jnp.dot(h, wr_ref[...], preferred_element_type=F32, precision=lax.Precision.HIGHEST) + brt_ref[...]
    lane = lax.broadcasted_iota(jnp.int32, lg.shape, 1)
    vals, ids = [], []
    for _ in range(TOP_K):
        mx = jnp.max(lg, axis=-1, keepdims=True)
        first = jnp.min(jnp.where(lg == mx, lane, LANES), axis=-1, keepdims=True)
        vals.append(mx)
        ids.append(first)
        lg = jnp.where(lane == first, -jnp.inf, lg)
    es = [jnp.exp(v - vals[0]) for v in vals]
    tot = es[0] + es[1] + es[2] + es[3]
    idx = jnp.zeros(lg.shape, jnp.int32)
    wts = jnp.zeros(lg.shape, F32)
    for k in range(TOP_K):
        idx = jnp.where(lane == k, ids[k], idx)
        wts = jnp.where(lane == k, es[k] / tot, wts)
    idx_ref[...] = idx
    wts_ref[...] = wts


def _merge(x2d, oc, os_, ow, ob, gate, br, prep, g_ffn, tm=256):
    t = x2d.shape[0]
    tm = min(tm, t)
    assert t % tm == 0
    w_out_p, e, w_r, b_r = prep
    row = lambda n: pl.BlockSpec((tm, n), lambda i: (i, 0))
    full = lambda a: pl.BlockSpec(a.shape, lambda i: (0,) * a.ndim)
    return pl.pallas_call(
        _merge_kernel,
        grid=(t // tm,),
        in_specs=[row(D_MODEL), row(NSA_W), row(NSA_W), row(NSA_W), row(SB_W), row(NSA_W + SB_W), row(LANES),
                  full(e), full(w_out_p), full(g_ffn), full(w_r), full(b_r)],
        out_specs=[row(D_MODEL), pl.BlockSpec((tm, ROW_SUB, LANES), lambda i: (i, 0, 0)), row(LANES), row(LANES)],
        out_shape=[jax.ShapeDtypeStruct((t, D_MODEL), F32), jax.ShapeDtypeStruct((t, ROW_SUB, LANES), F32),
                   jax.ShapeDtypeStruct((t, LANES), jnp.int32), jax.ShapeDtypeStruct((t, LANES), F32)],
        compiler_params=_cparams(("parallel",)),
        name="merge",
    )(x2d, oc, os_, ow, ob, gate, br, e, w_out_p, g_ffn, w_r, b_r)


MOE_TILE = 256
COMBINE_ROWS = 128
ROW_SUB = D_MODEL // LANES


def _route_tables(idx):
    t = idx.shape[0]
    a = t * TOP_K
    n_tiles = a // MOE_TILE + N_EXPERTS
    e = idx.reshape(a)
    onehot = (e[:, None] == jnp.arange(N_EXPERTS, dtype=jnp.int32)[None, :]).astype(jnp.int32)
    csum = jnp.cumsum(onehot, axis=0)
    rank = jnp.take_along_axis(csum, e[:, None], axis=1)[:, 0] - 1
    counts = csum[-1]
    tiles_e = (counts + MOE_TILE - 1) // MOE_TILE
    tile_end = jnp.cumsum(tiles_e)
    tile_start = tile_end - tiles_e
    dest = tile_start[e] * MOE_TILE + rank
    row_token = jnp.zeros((n_tiles * MOE_TILE,), jnp.int32).at[dest].set(jnp.arange(a, dtype=jnp.int32) // TOP_K)
    tile_ids = jnp.arange(n_tiles, dtype=jnp.int32)
    tile_expert = jnp.minimum(jnp.sum((tile_end[None, :] <= tile_ids[:, None]).astype(jnp.int32), axis=1),
                              N_EXPERTS - 1).astype(jnp.int32)
    return (row_token.reshape(n_tiles, 1, MOE_TILE), dest.reshape(t, TOP_K), tile_expert,
            tile_end[-1:].astype(jnp.int32))


def _deinterleave_matrices():
    e = np.zeros((2 * LANES, LANES), np.float32)
    o = np.zeros((2 * LANES, LANES), np.float32)
    e[2 * np.arange(LANES), np.arange(LANES)] = 1.0
    o[2 * np.arange(LANES) + 1, np.arange(LANES)] = 1.0
    return jnp.asarray(e, BF16), jnp.asarray(o, BF16)


def _row_tiles_to_2d(load):
    return jnp.concatenate([load(j) for j in range(ROW_SUB)], axis=1)


def _expert_kernel(te_ref, nu_ref, rt_ref, rtn_ref, h_ref, wgu_ref, wdn_ref, bg_ref, bu_ref, bd_ref, se_ref, so_ref,
                   y_ref, xbuf, sem, wg_s, wu_s, wd_s):
    i = pl.program_id(0)
    n_used = nu_ref[0]
    slot = i % 2

    def copy(idx_ref, r, s):
        return pltpu.make_async_copy(h_ref.at[pl.ds(idx_ref[0, r], 1)], xbuf.at[s, pl.ds(r, 1)], sem.at[s])

    def issue(idx_ref, s):
        def body(r, c):
            copy(idx_ref, r, s).start()
            return c
        lax.fori_loop(0, MOE_TILE, body, 0, unroll=8)

    @pl.when(i == 0)
    def _():
        issue(rt_ref, 0)

    @pl.when(i + 1 < n_used)
    def _():
        issue(rtn_ref, 1 - slot)

    @pl.when(i < n_used)
    def _():
        @pl.when((i == 0) | (te_ref[i] != te_ref[jnp.maximum(i - 1, 0)]))
        def _():
            for k in range(D_FF // LANES):
                wblk = wgu_ref[:, 2 * LANES * k:2 * LANES * (k + 1)].astype(BF16)
                wg_s[:, LANES * k:LANES * (k + 1)] = jnp.dot(wblk, se_ref[...], preferred_element_type=F32).astype(BF16)
                wu_s[:, LANES * k:LANES * (k + 1)] = jnp.dot(wblk, so_ref[...], preferred_element_type=F32).astype(BF16)
            wd_s[...] = wdn_ref[...].astype(BF16)

        def wait(r, c):
            copy(rt_ref, r, slot).wait()
            return c
        lax.fori_loop(0, MOE_TILE, wait, 0, unroll=8)

        x = _row_tiles_to_2d(lambda j: xbuf[slot, :, j, :]).astype(BF16)
        g = jnp.minimum(jnp.dot(x, wg_s[...], preferred_element_type=F32) + bg_ref[...], SWIGLU_LIMIT)
        u = jnp.clip(jnp.dot(x, wu_s[...], preferred_element_type=F32) + bu_ref[...], -SWIGLU_LIMIT, SWIGLU_LIMIT)
        act = (u + 1.0) * g * jax.nn.sigmoid(SWIGLU_ALPHA * g)
        y = jnp.dot(act.astype(BF16), wd_s[...], preferred_element_type=F32) + bd_ref[...]
        for j in range(ROW_SUB):
            y_ref[:, j, :] = y[:, j * LANES:(j + 1) * LANES]

    @pl.when(i >= n_used)
    def _():
        y_ref[...] = jnp.zeros(y_ref.shape, F32)


def _experts(h_rows, row_token, tile_expert, n_used, w_gu, w_dn, bg, bu, bd):
    n_tiles = row_token.shape[0]
    se, so = _deinterleave_matrices()
    wspec = lambda shape: pl.BlockSpec((None,) + shape, lambda i, te, nu: (te[i], 0, 0))
    full = lambda a: pl.BlockSpec(a.shape, lambda i, te, nu: (0,) * a.ndim)
    rt_spec = lambda off: pl.BlockSpec((None, 1, MOE_TILE), lambda i, te, nu: (jnp.minimum(i + off, n_tiles - 1), 0, 0),
                                       memory_space=pltpu.SMEM)
    return pl.pallas_call(
        _expert_kernel,
        grid_spec=pltpu.PrefetchScalarGridSpec(
            num_scalar_prefetch=2,
            grid=(n_tiles,),
            in_specs=[rt_spec(0), rt_spec(1), pl.BlockSpec(memory_space=pl.ANY),
                      wspec((D_MODEL, 2 * D_FF)), wspec((D_FF, D_MODEL)), wspec((1, D_FF)), wspec((1, D_FF)),
                      wspec((1, D_MODEL)), full(se), full(so)],
            out_specs=pl.BlockSpec((MOE_TILE, ROW_SUB, LANES), lambda i, te, nu: (i, 0, 0)),
            scratch_shapes=[pltpu.VMEM((2, MOE_TILE, ROW_SUB, LANES), F32), pltpu.SemaphoreType.DMA((2,)),
                            pltpu.VMEM((D_MODEL, D_FF), BF16), pltpu.VMEM((D_MODEL, D_FF), BF16),
                            pltpu.VMEM((D_FF, D_MODEL), BF16)],
        ),
        out_shape=jax.ShapeDtypeStruct((n_tiles * MOE_TILE, ROW_SUB, LANES), F32),
        compiler_params=_cparams(("arbitrary",)),
        name="moe_experts",
    )(tile_expert, n_used, row_token, row_token, h_rows, w_gu, w_dn, bg, bu, bd, se, so)


def _combine_kernel(pos_ref, posn_ref, x_ref, wts_ref, g_ref, y_ref, out_ref, buf, sem, *, n):
    i = pl.program_id(0)
    n_rows = TOP_K * COMBINE_ROWS

    def copy(p_ref, r, slot):
        return pltpu.make_async_copy(y_ref.at[pl.ds(p_ref[0, r], 1)],
                                     buf.at[slot, r & (TOP_K - 1), pl.ds(r >> 2, 1)], sem.at[slot])

    def issue(p_ref, slot):
        def body(r, c):
            copy(p_ref, r, slot).start()
            return c
        lax.fori_loop(0, n_rows, body, 0, unroll=8)

    slot = i % 2

    @pl.when(i == 0)
    def _():
        issue(pos_ref, 0)

    @pl.when(i + 1 < n)
    def _():
        issue(posn_ref, 1 - slot)

    def wait(r, c):
        copy(pos_ref, r, slot).wait()
        return c
    lax.fori_loop(0, n_rows, wait, 0, unroll=8)

    w = wts_ref[...]
    parts = []
    ssq = jnp.zeros((COMBINE_ROWS, 1), F32)
    for j in range(ROW_SUB):
        xs = x_ref[:, j * LANES:(j + 1) * LANES]
        for k in range(TOP_K):
            xs = xs + w[:, k:k + 1] * buf[slot, k, :, j, :]
        parts.append(xs)
        ssq = ssq + jnp.sum(xs * xs, axis=-1, keepdims=True)
    rs = lax.rsqrt(ssq * (1.0 / D_MODEL) + RMS_EPS)
    for j in range(ROW_SUB):
        out_ref[:, j * LANES:(j + 1) * LANES] = (parts[j] * rs) * g_ref[:, j * LANES:(j + 1) * LANES]


def _combine(x1, wts, pos, y_sorted, g_final):
    t = x1.shape[0]
    assert t % COMBINE_ROWS == 0
    steps = t // COMBINE_ROWS
    pos2 = pos.reshape(steps, 1, TOP_K * COMBINE_ROWS)
    row = lambda nn: pl.BlockSpec((COMBINE_ROWS, nn), lambda i: (i, 0))
    return pl.pallas_call(
        functools.partial(_combine_kernel, n=steps),
        grid=(steps,),
        in_specs=[pl.BlockSpec((None, 1, TOP_K * COMBINE_ROWS), lambda i: (i, 0, 0), memory_space=pltpu.SMEM),
                  pl.BlockSpec((None, 1, TOP_K * COMBINE_ROWS), lambda i: (jnp.minimum(i + 1, steps - 1), 0, 0),
                               memory_space=pltpu.SMEM),
                  row(D_MODEL), row(LANES), pl.BlockSpec(g_final.shape, lambda i: (0, 0)),
                  pl.BlockSpec(memory_space=pl.ANY)],
        out_specs=row(D_MODEL),
        out_shape=jax.ShapeDtypeStruct((t, D_MODEL), F32),
        scratch_shapes=[pltpu.VMEM((2, TOP_K, COMBINE_ROWS, ROW_SUB, LANES), F32), pltpu.SemaphoreType.DMA((2,))],
        compiler_params=_cparams(("arbitrary",)),
        name="moe_combine",
    )(pos2, pos2, x1, wts, g_final, y_sorted)


QPAD = 8


def _select_blocks_rows(imp_sel, cur, nsel):
    tq = imp_sel.shape[0]
    blk = lax.broadcasted_iota(jnp.int32, (tq, LANES), 1)
    forced = (blk == 0) | (blk == cur) | (blk == cur - 1)
    score = jnp.where(blk <= cur, imp_sel + jnp.where(forced, FORCE_BONUS, 0.0), -1e9)
    cnt = jnp.zeros((tq, LANES), jnp.int32)
    for k in range(nsel):
        sk = score[:, k:k + 1]
        beats = (sk > score) | ((sk == score) & (blk > k))
        cnt = cnt + jnp.where(beats, 1, 0)
    return jnp.where((cnt < TOP_N) & (blk < nsel), 1.0, 0.0)


def _pad_rows(x, n):
    return jnp.concatenate([x, jnp.zeros((n - x.shape[0], x.shape[1]), x.dtype)], axis=0)


def _nsa_s_kernel(pt_ref, qc_ref, qr_ref, xnew_ref, snew_ref, w1_ref, w2_ref, m_ref, *rest, n_pages, past, dec):
    cmp_pages = rest[:n_pages]
    slc_pages = rest[n_pages:2 * n_pages]
    ocmp_ref, oslc_ref = rest[2 * n_pages:]
    tq = QPAD
    nc = n_pages * (PAGE_SIZE // CMP_STRIDE)
    nsel = past // SEL_BLK + 1
    trow = lax.broadcasted_iota(jnp.int32, (NSA_REP * tq, 1), 0) & (tq - 1)
    rpos = past + jnp.minimum(trow, dec - 1)
    qpos = past + jnp.minimum(lax.broadcasted_iota(jnp.int32, (tq, 1), 0), dec - 1)
    xb = jnp.concatenate([p[...] for p in cmp_pages], axis=0).astype(BF16)
    y1 = jnp.dot(xb, w1_ref[...], preferred_element_type=F32)
    y2 = jnp.dot(xb, w2_ref[...], preferred_element_type=F32)
    xn = jnp.broadcast_to(xnew_ref[...], (8, xnew_ref.shape[1])).astype(BF16)
    ynew = jnp.dot(xn, w2_ref[...], preferred_element_type=F32)[0:1]
    rid = lax.broadcasted_iota(jnp.int32, y2.shape, 0)
    kcv = y1 + jnp.where(rid < nc - 1, pltpu.roll(y2, nc - 1, 0), ynew)
    kc = kcv[:, :KV_W].astype(BF16)
    vc = kcv[:, KV_W:].astype(BF16)
    lane = lax.broadcasted_iota(jnp.int32, (tq, LANES), 1)
    o_c, o_s = [], []
    for g in range(NSA_KV):
        qcg = _stack_group(qc_ref, g, tq)
        qrg = _stack_group(qr_ref, g, tq)
        s = _dot_nt(qcg, kc)
        cidx = lax.broadcasted_iota(jnp.int32, s.shape, 1)
        cmask = cidx * CMP_STRIDE + (CMP_BLK - 1) <= rpos
        s = jnp.where(cmask, s, -1e30)
        e = jnp.where(cmask, jnp.exp(s - jnp.max(s, axis=-1, keepdims=True)), 0.0)
        p = e / jnp.maximum(jnp.sum(e, axis=-1, keepdims=True), 1e-30)
        o_c.append(jnp.dot(p.astype(BF16), vc, preferred_element_type=F32))
        imp = p[0:tq] + p[tq:2 * tq] + p[2 * tq:3 * tq] + p[3 * tq:4 * tq]
        sel = _select_blocks_rows(_dot3(imp, m_ref[...]), qpos >> 6, nsel)
        scores, valids, vals = [], [], []
        for j in range(n_pages):
            page = slc_pages[j]
            bm = jnp.where(lane < SEL_BLK, sel[:, 2 * j:2 * j + 1], sel[:, 2 * j + 1:2 * j + 2])
            valids.append(jnp.concatenate([bm] * NSA_REP, axis=0) > 0.5)
            scores.append(_dot_nt(qrg, page[:, :KV_W].astype(BF16)))
            vals.append(page[:, KV_W:].astype(BF16))
        snew = snew_ref[...]
        bm = jnp.broadcast_to(sel[:, nsel - 1:nsel], (tq, LANES))
        lane4 = lax.broadcasted_iota(jnp.int32, (NSA_REP * tq, LANES), 1)
        valids.append((jnp.concatenate([bm] * NSA_REP, axis=0) > 0.5) & (lane4 < dec) & (past + lane4 <= rpos))
        scores.append(_dot_nt(qrg, _pad_rows(snew[:, :KV_W], LANES).astype(BF16)))
        vals.append(_pad_rows(snew[:, KV_W:], LANES).astype(BF16))
        scores = [jnp.where(v, sc, -1e30) for sc, v in zip(scores, valids)]
        mx = functools.reduce(jnp.maximum, [jnp.max(sc, axis=-1, keepdims=True) for sc in scores])
        l = jnp.zeros((NSA_REP * tq, 1), F32)
        acc = jnp.zeros((NSA_REP * tq, LANES), F32)
        for sc, v, vv in zip(scores, valids, vals):
            a = jnp.where(v, jnp.exp(sc - mx), 0.0)
            l = l + jnp.sum(a, axis=-1, keepdims=True)
            acc = acc + jnp.dot(a.astype(BF16), vv, preferred_element_type=F32)
        o_s.append(acc / jnp.maximum(l, 1e-30))
    ocmp_ref[...] = _unstack_pairs(o_c[0], o_c[1], tq)
    oslc_ref[...] = _unstack_pairs(o_s[0], o_s[1], tq)


def _nsa_s(page_table, qc, qr, xnew, snew, w1, w2, cache_cmp, cache_slc, past, dec):
    bd, n_pages = page_table.shape
    cpp = PAGE_SIZE // CMP_STRIDE
    cmp_pages = cache_cmp.reshape(cache_cmp.shape[0], cpp, CMP_STRIDE * 2 * KV_W)
    m = _imp_matrix(n_pages * cpp, past // SEL_BLK + 1)
    per_b = lambda a: pl.BlockSpec((None,) + a.shape[1:], lambda b, pt: (b,) + (0,) * (a.ndim - 1))
    full = lambda a: pl.BlockSpec(a.shape, lambda b, pt: (0,) * a.ndim)
    page = lambda a, j: pl.BlockSpec((None,) + a.shape[1:], lambda b, pt: (pt[b, j], 0, 0))
    ospec = pl.BlockSpec((None, QPAD, NSA_W), lambda b, pt: (b, 0, 0))
    return pl.pallas_call(
        functools.partial(_nsa_s_kernel, n_pages=n_pages, past=past, dec=dec),
        grid_spec=pltpu.PrefetchScalarGridSpec(
            num_scalar_prefetch=1,
            grid=(bd,),
            in_specs=[per_b(qc), per_b(qr), per_b(xnew), per_b(snew), full(w1), full(w2), full(m)]
            + [page(cmp_pages, j) for j in range(n_pages)] + [page(cache_slc, j) for j in range(n_pages)],
            out_specs=[ospec, ospec],
        ),
        out_shape=[jax.ShapeDtypeStruct((bd, QPAD, NSA_W), F32)] * 2,
        compiler_params=_cparams(("arbitrary",)),
        name="nsa_sample",
    )(page_table, qc, qr, xnew, snew, w1, w2, m, *([cmp_pages] * n_pages), *([cache_slc] * n_pages))


def _window_s_kernel(qr_ref, st_ref, wnew_ref, o_ref, *, past, dec):
    tq = QPAD
    wbuf = st_ref.shape[0]
    trow = jnp.minimum(lax.broadcasted_iota(jnp.int32, (NSA_REP * tq, 1), 0) & (tq - 1), dec - 1)
    k = st_ref[:, :KV_W].astype(BF16)
    v = st_ref[:, KV_W:].astype(BF16)
    wnew = wnew_ref[...]
    kn = _pad_rows(wnew[:, :KV_W], LANES).astype(BF16)
    vn = _pad_rows(wnew[:, KV_W:], LANES).astype(BF16)
    outs = []
    for g in range(NSA_KV):
        qrg = _stack_group(qr_ref, g, tq)
        s0 = _dot_nt(qrg, k)
        d0 = wbuf + trow - lax.broadcasted_iota(jnp.int32, s0.shape, 1)
        m0 = (d0 >= 0) & (d0 < WINDOW)
        s1 = _dot_nt(qrg, kn)
        i1 = lax.broadcasted_iota(jnp.int32, s1.shape, 1)
        m1 = (i1 < dec) & (trow - i1 >= 0) & (trow - i1 < WINDOW)
        s0 = jnp.where(m0, s0, -1e30)
        s1 = jnp.where(m1, s1, -1e30)
        mx = jnp.maximum(jnp.max(s0, axis=-1, keepdims=True), jnp.max(s1, axis=-1, keepdims=True))
        e0 = jnp.where(m0, jnp.exp(s0 - mx), 0.0)
        e1 = jnp.where(m1, jnp.exp(s1 - mx), 0.0)
        den = jnp.maximum(jnp.sum(e0, axis=-1, keepdims=True) + jnp.sum(e1, axis=-1, keepdims=True), 1e-30)
        o = (jnp.dot(e0.astype(BF16), v, preferred_element_type=F32)
             + jnp.dot(e1.astype(BF16), vn, preferred_element_type=F32))
        outs.append(o / den)
    o_ref[...] = _unstack_pairs(outs[0], outs[1], tq)


def _window_s(qr, state, wnew, past, dec):
    bd = qr.shape[0]
    per_b = lambda a: pl.BlockSpec((None,) + a.shape[1:], lambda b: (b,) + (0,) * (a.ndim - 1))
    return pl.pallas_call(
        functools.partial(_window_s_kernel, past=past, dec=dec),
        grid=(bd,),
        in_specs=[per_b(qr), per_b(state), per_b(wnew)],
        out_specs=pl.BlockSpec((None, QPAD, NSA_W), lambda b: (b, 0, 0)),
        out_shape=jax.ShapeDtypeStruct((bd, QPAD, NSA_W), F32),
        compiler_params=_cparams(("parallel",)),
        name="window_sample",
    )(qr, state, wnew)


def _sb_s_kernel(pt_ref, q_ref, new_ref, tri_ref, *rest, n_pages):
    pages = rest[:n_pages]
    o_ref = rest[n_pages]
    tq = QPAD
    nrow = SB_HEADS * tq
    row = lax.broadcasted_iota(jnp.int32, (nrow, SB_W), 0)
    col = lax.broadcasted_iota(jnp.int32, (nrow, SB_W), 1)
    own = (row >> 3) == (col >> 6)
    qbd = jnp.where(own, jnp.concatenate([q_ref[...]] * SB_HEADS, axis=0), 0.0).astype(BF16)
    tri = tri_ref[...]
    new = new_ref[...]
    kn = _pad_rows(new[:, :SB_W], PAGE_SIZE).astype(BF16)
    vn = _pad_rows(new[:, SB_W:], PAGE_SIZE).astype(BF16)
    lane = lax.broadcasted_iota(jnp.int32, (nrow, PAGE_SIZE), 1)
    trow = lax.broadcasted_iota(jnp.int32, (nrow, PAGE_SIZE), 0) & (tq - 1)
    carry, acc = _sb_block(qbd, kn, vn, tri, jnp.zeros((nrow, 1), F32), jnp.zeros((nrow, SB_W), F32), lane < trow)
    for j in reversed(range(n_pages)):
        pg = pages[j]
        carry, acc = _sb_block(qbd, pg[:, :SB_W].astype(BF16), pg[:, SB_W:].astype(BF16), tri, carry, acc, None)
    acc = jnp.where(own, acc, 0.0)
    out = acc[0:tq]
    for h in range(1, SB_HEADS):
        out = out + acc[h * tq:(h + 1) * tq]
    o_ref[...] = out


def _sb_s(page_table, sbq, sbnew, cache_sb):
    bd, n_pages = page_table.shape
    tri = _tri_matrix(PAGE_SIZE)
    per_b = lambda a: pl.BlockSpec((None,) + a.shape[1:], lambda b, pt: (b,) + (0,) * (a.ndim - 1))
    page = lambda j: pl.BlockSpec((None,) + cache_sb.shape[1:], lambda b, pt: (pt[b, j], 0, 0))
    return pl.pallas_call(
        functools.partial(_sb_s_kernel, n_pages=n_pages),
        grid_spec=pltpu.PrefetchScalarGridSpec(
            num_scalar_prefetch=1,
            grid=(bd,),
            in_specs=[per_b(sbq), per_b(sbnew), pl.BlockSpec(tri.shape, lambda b, pt: (0, 0))]
            + [page(j) for j in range(n_pages)],
            out_specs=pl.BlockSpec((None, QPAD, SB_W), lambda b, pt: (b, 0, 0)),
        ),
        out_shape=jax.ShapeDtypeStruct((bd, QPAD, SB_W), F32),
        compiler_params=_cparams(("arbitrary",)),
        name="sb_sample",
    )(page_table, sbq, sbnew, tri, *([cache_sb] * n_pages))


def kernel(x_prompt, x_sample, cache_nsa_cmp, cache_nsa_slc, cache_sb, state_nsa_win, page_table, g_mix, w_in, w_ck,
           w_cv, w_out, g_ffn, w_router, b_router, w_gu, b_gu, w_dn, b_dn, g_final):
    b, s, d = x_prompt.shape
    bd, ds, _ = x_sample.shape
    depth = g_mix.shape[0]
    assert depth == 1 and d == D_MODEL and ds <= QPAD
    n_pages = page_table.shape[1]
    past = n_pages * PAGE_SIZE
    wbuf = state_nsa_win.shape[2]
    n_phys = cache_nsa_cmp.shape[1]
    tp, ts = b * s, bd * ds

    w_perm = _prep_w_in(w_in[0])
    w1, w2 = _prep_w_cmp(w_ck[0], w_cv[0])
    prep = _prep_merge(w_out[0], w_router[0], b_router[0])
    g_mix2, g_ffn2, g_fin2 = g_mix[0].reshape(1, d), g_ffn[0].reshape(1, d), g_final.reshape(1, d)
    bg = b_gu[0][:, None, 0::2]
    bu = b_gu[0][:, None, 1::2]
    bdn = b_dn[0][:, None, :]

    pos_p = jnp.arange(s, dtype=jnp.int32)
    (qc, qr, sbq, cmp_p, slc_p, win_p, sbkv_p, kslc, vslc, kwin, vwin, sbk, sbv, gate_p, br_p) = _proj(
        x_prompt.reshape(tp, d), g_mix2, w_perm, _rope_tables(pos_p), 256)
    kc, vc = _compress(cmp_p, w1, w2, b, s)
    o_cmp, o_slc = _nsa(qc, qr, kc, vc, kslc, vslc, b, s)
    o_win = _window(qr, kwin, vwin, b, s)
    o_sb = _sb(sbq, sbk, sbv, b, s)
    x1_p, h_p, idx_p, wts_p = _merge(x_prompt.reshape(tp, d), o_cmp, o_slc, o_win, o_sb, gate_p, br_p, prep, g_ffn2)

    pos_s = past + jnp.arange(ds, dtype=jnp.int32)
    tabs_s = [jnp.tile(t, (bd, 1)) for t in _rope_tables(pos_s)]
    (qc_s, qr_s, sbq_s, cmp_s, slc_s, win_s, sbkv_s, _, _, _, _, _, _, gate_s, br_s) = _proj(
        x_sample.reshape(ts, d), g_mix2, w_perm, tabs_s, min(256, ts))
    pad_q = lambda a: jnp.pad(a.astype(F32).reshape(bd, ds, a.shape[1]), ((0, 0), (0, QPAD - ds), (0, 0)))
    xnew = jnp.pad(cmp_s.reshape(bd, 1, ds * 2 * KV_W), ((0, 0), (0, 0), (0, (CMP_STRIDE - ds) * 2 * KV_W)))
    o_cmp_s, o_slc_s = _nsa_s(page_table, pad_q(qc_s), pad_q(qr_s), xnew, pad_q(slc_s), w1, w2,
                              cache_nsa_cmp[0].reshape(n_phys, PAGE_SIZE, 2 * KV_W),
                              cache_nsa_slc[0].reshape(n_phys, PAGE_SIZE, 2 * KV_W), past, ds)
    o_win_s = _window_s(pad_q(qr_s), state_nsa_win[0].reshape(bd, wbuf, 2 * KV_W), pad_q(win_s), past, ds)
    o_sb_s = _sb_s(page_table, pad_q(sbq_s), pad_q(sbkv_s), cache_sb[0].reshape(n_phys, PAGE_SIZE, 2 * SB_W))
    unpad = lambda a: a[:, :ds].reshape(ts, a.shape[2])
    x1_s, h_s, idx_s, wts_s = _merge(x_sample.reshape(ts, d), unpad(o_cmp_s), unpad(o_slc_s), unpad(o_win_s),
                                     unpad(o_sb_s), gate_s, br_s, prep, g_ffn2)

    h_rows = jnp.concatenate([h_p, h_s], axis=0)
    idx = jnp.concatenate([idx_p[:, :TOP_K], idx_s[:, :TOP_K]], axis=0)
    row_token, pos, tile_expert, n_used = _route_tables(idx)
    y_sorted = _experts(h_rows, row_token, tile_expert, n_used, w_gu[0], w_dn[0], bg, bu, bdn)
    y_p = _combine(x1_p, wts_p, pos[:tp], y_sorted, g_fin2)
    y_s = _combine(x1_s, wts_s, pos[tp:], y_sorted, g_fin2)

    win_all = jnp.concatenate([state_nsa_win[0], win_s.reshape(bd, ds, 2, NSA_KV, HEAD_DIM)], axis=1)
    lead = lambda a, n, h: a.reshape(1, n, -1, 2, h, HEAD_DIM)
    return (y_p.reshape(b, s, d), y_s.reshape(bd, ds, d),
            lead(cmp_p, b, NSA_KV), lead(slc_p, b, NSA_KV), lead(sbkv_p, b, SB_HEADS),
            lead(win_p, b, NSA_KV)[:, :, -wbuf:] if s >= wbuf else
            jnp.pad(lead(win_p, b, NSA_KV), ((0, 0), (0, 0), (wbuf - s, 0), (0, 0), (0, 0), (0, 0))),
            lead(cmp_s, bd, NSA_KV), lead(slc_s, bd, NSA_KV), lead(sbkv_s, bd, SB_HEADS),
            win_all[None, :, ds:])
```

```python
import functools

import numpy as np
import jax
import jax.numpy as jnp
from jax import lax
from jax.experimental import pallas as pl
from jax.experimental.pallas import tpu as pltpu

D_MODEL = 1024
HEAD_DIM = 64
NSA_HEADS = 8
NSA_KV = 2
NSA_REP = NSA_HEADS // NSA_KV
SB_HEADS = 8
NSA_W = NSA_HEADS * HEAD_DIM
SB_W = SB_HEADS * HEAD_DIM
KV_W = NSA_KV * HEAD_DIM
CMP_BLK = 32
CMP_STRIDE = 16
SEL_BLK = 64
TOP_N = 16
WINDOW = 512
FORCE_BONUS = 1.0e4
ROPE_THETA = 500000.0
ROPE_DIMS = HEAD_DIM // 4
N_EXPERTS = 32
TOP_K = 4
D_FF = D_MODEL
SWIGLU_LIMIT = 7.0
SWIGLU_ALPHA = 1.702
RMS_EPS = 1e-5
PAGE_SIZE = 128
SCALE = HEAD_DIM ** -0.5

LANES = 128
VMEM_LIMIT_BYTES = 56 * 1024 * 1024

BF16 = jnp.bfloat16
F32 = jnp.float32

_OFF = dict(q=0, k_cmp=512, v_cmp=640, k_slc=768, v_slc=896, k_win=1024, v_win=1152, br=1280,
            sb_q=1304, sb_k=1816, sb_v=2328, gate_a=2840, gate_b=3352, end=3864)
_HEAD_PERM = [j + 4 * half for j in range(4) for half in range(2)]
_C_Q, _C_SBQ, _C_CMP, _C_SLC, _C_WIN, _C_SBKV, _C_GATE, _C_BR, _NP = 0, 512, 1024, 1280, 1536, 1792, 2816, 3840, 3968


def _cparams(sem):
    return pltpu.CompilerParams(dimension_semantics=sem, vmem_limit_bytes=VMEM_LIMIT_BYTES)


def _prep_w_in(w):
    def head_perm(base):
        return [w[:, base + h * HEAD_DIM: base + (h + 1) * HEAD_DIM] for h in _HEAD_PERM]
    parts = [c * SCALE for c in head_perm(_OFF['q'])]
    parts.append(w[:, _OFF['sb_q']:_OFF['sb_k']] * SCALE)
    parts.append(w[:, _OFF['k_cmp']:_OFF['br']])
    parts.append(w[:, _OFF['sb_k']:_OFF['gate_a']])
    parts.extend(head_perm(_OFF['gate_a']))
    parts.append(w[:, _OFF['gate_b']:_OFF['end']])
    parts.append(w[:, _OFF['br']:_OFF['sb_q']])
    parts.append(jnp.zeros((w.shape[0], LANES - 3 * NSA_HEADS), w.dtype))
    return jnp.concatenate(parts, axis=1).astype(BF16)


def _rope_tables(pos):
    half = ROPE_DIMS // 2
    inv = ROPE_THETA ** (-jnp.arange(half, dtype=F32) / half)
    ang = pos.astype(F32)[:, None] * inv[None, :]
    cos = jnp.cos(ang)
    sin = jnp.sin(ang)
    n = pos.shape[0]
    one = jnp.ones((n, HEAD_DIM - ROPE_DIMS), F32)
    zero = jnp.zeros((n, HEAD_DIM - ROPE_DIMS), F32)
    z8 = jnp.zeros((n, half), F32)
    c = jnp.concatenate([cos, cos, one], axis=1)
    s1 = jnp.concatenate([-sin, z8, zero], axis=1)
    s2 = jnp.concatenate([z8, sin, zero], axis=1)
    tile2 = lambda t: jnp.concatenate([t, t], axis=1)
    return tile2(c), tile2(s1), tile2(s2)


def _rope_apply(x, c, s1, s2):
    outs = []
    for j in range(x.shape[1] // LANES):
        xb = x[:, j * LANES:(j + 1) * LANES]
        outs.append(xb * c + pltpu.roll(xb, LANES - 8, 1) * s1 + pltpu.roll(xb, 8, 1) * s2)
    return outs[0] if len(outs) == 1 else jnp.concatenate(outs, axis=1)


def _proj_kernel(x_ref, g_ref, w_ref, c_ref, s1_ref, s2_ref,
                 qc_ref, qr_ref, sbq_ref, cmp_ref, slc_ref, win_ref, sbkv_ref,
                 kslc_ref, vslc_ref, kwin_ref, vwin_ref, sbk_ref, sbv_ref, gate_ref, br_ref):
    x = x_ref[...]
    ms = jnp.mean(x * x, axis=-1, keepdims=True)
    h = (x * lax.rsqrt(ms + RMS_EPS)) * g_ref[...]
    hb = h.astype(BF16)
    c, s1, s2 = c_ref[...], s1_ref[...], s2_ref[...]

    def mm(c0, c1):
        return jnp.dot(hb, w_ref[:, c0:c1], preferred_element_type=F32)

    q = mm(_C_Q, _C_SBQ)
    qc_ref[...] = q.astype(BF16)
    qr_ref[...] = _rope_apply(q, c, s1, s2).astype(BF16)
    sbq_ref[...] = mm(_C_SBQ, _C_CMP).astype(BF16)
    cmp_ref[...] = mm(_C_CMP, _C_SLC)
    for c0, cache_ref, k_ref, v_ref in ((_C_SLC, slc_ref, kslc_ref, vslc_ref), (_C_WIN, win_ref, kwin_ref, vwin_ref)):
        z = mm(c0, c0 + 2 * KV_W)
        k = _rope_apply(z[:, :KV_W], c, s1, s2)
        v = z[:, KV_W:]
        cache_ref[:, :KV_W] = k
        cache_ref[:, KV_W:] = v
        k_ref[...] = k.astype(BF16)
        v_ref[...] = v.astype(BF16)
    z = mm(_C_SBKV, _C_GATE)
    sbkv_ref[...] = z
    sbk_ref[...] = z[:, :SB_W].astype(BF16)
    sbv_ref[...] = z[:, SB_W:].astype(BF16)
    gate_ref[...] = jax.nn.sigmoid(mm(_C_GATE, _C_BR)).astype(BF16)
    br_ref[...] = jax.nn.sigmoid(mm(_C_BR, _NP))


def _proj(x2d, g, w_perm, tabs, tm):
    t = x2d.shape[0]
    tab_blocks = tabs[0].shape[0] // tm
    assert t % tm == 0 and tabs[0].shape[0] % tm == 0
    row = lambda n: pl.BlockSpec((tm, n), lambda i: (i, 0))
    tab = pl.BlockSpec((tm, LANES), lambda i: (i % tab_blocks, 0))
    full = lambda a: pl.BlockSpec(a.shape, lambda i: (0,) * a.ndim)
    outs = [(512, BF16), (512, BF16), (512, BF16), (256, F32), (256, F32), (256, F32), (1024, F32),
            (128, BF16), (128, BF16), (128, BF16), (128, BF16), (512, BF16), (512, BF16), (1024, BF16), (128, F32)]
    return pl.pallas_call(
        _proj_kernel,
        grid=(t // tm,),
        in_specs=[row(D_MODEL), full(g), full(w_perm), tab, tab, tab],
        out_specs=[row(n) for n, _ in outs],
        out_shape=[jax.ShapeDtypeStruct((t, n), dt) for n, dt in outs],
        compiler_params=_cparams(("parallel",)),
        name="proj",
    )(x2d, g, w_perm, *tabs)


def _prep_w_cmp(w_ck, w_cv):
    eye2 = jnp.eye(NSA_KV, dtype=w_ck.dtype)
    bd = lambda w: jnp.einsum('ab,ide->iadbe', eye2, w).reshape(CMP_BLK, KV_W, KV_W)
    z = jnp.zeros((CMP_BLK, KV_W, KV_W), w_ck.dtype)
    w_all = jnp.concatenate([jnp.concatenate([bd(w_ck), z], axis=2),
                             jnp.concatenate([z, bd(w_cv)], axis=2)], axis=1)
    w_all = w_all.astype(BF16)
    return (w_all[:CMP_STRIDE].reshape(CMP_STRIDE * 2 * KV_W, 2 * KV_W),
            w_all[CMP_STRIDE:].reshape(CMP_STRIDE * 2 * KV_W, 2 * KV_W))


def _compress_rows(xb, w1, w2):
    n = xb.shape[0]
    y1 = jnp.dot(xb, w1, preferred_element_type=F32)
    y2 = jnp.dot(xb, w2, preferred_element_type=F32)
    y2s = pltpu.roll(y2, n - 1, 0)
    rid = lax.broadcasted_iota(jnp.int32, y2.shape, 0)
    return y1 + jnp.where(rid < n - 1, y2s, 0.0)


def _compress_kernel(x_ref, w1_ref, w2_ref, k_ref, v_ref):
    out = _compress_rows(x_ref[...].astype(BF16), w1_ref[...], w2_ref[...])
    k_ref[...] = out[:, :KV_W].astype(BF16)
    v_ref[...] = out[:, KV_W:].astype(BF16)


def _compress(cache2d, w1, w2, b, s):
    nc = s // CMP_STRIDE
    x = cache2d.reshape(b, nc, CMP_STRIDE * 2 * KV_W)
    full = lambda a: pl.BlockSpec(a.shape, lambda i: (0,) * a.ndim)
    blk = pl.BlockSpec((None, nc, KV_W), lambda i: (i, 0, 0))
    return pl.pallas_call(
        _compress_kernel,
        grid=(b,),
        in_specs=[pl.BlockSpec((None, nc, x.shape[2]), lambda i: (i, 0, 0)), full(w1), full(w2)],
        out_specs=[blk, blk],
        out_shape=[jax.ShapeDtypeStruct((b, nc, KV_W), BF16)] * 2,
        compiler_params=_cparams(("parallel",)),
        name="compress",
    )(x, w1, w2)


_NT = (((1,), (1,)), ((), ()))


def _dot_nt(a, b):
    return lax.dot_general(a, b, _NT, preferred_element_type=F32)


def _dot3(x, m):
    hi = x.astype(BF16)
    r1 = x - hi.astype(F32)
    mid = r1.astype(BF16)
    lo = (r1 - mid.astype(F32)).astype(BF16)
    d = lambda a: jnp.dot(a, m, preferred_element_type=F32)
    return d(hi) + d(mid) + d(lo)


def _stack_group(ref, g, tq):
    half = lax.broadcasted_iota(jnp.int32, (tq, LANES), 1) >> 6
    return jnp.concatenate([jnp.where(half == g, ref[:, j * LANES:(j + 1) * LANES], 0)
                            for j in range(NSA_REP)], axis=0).astype(BF16)


def _unstack_pairs(o0, o1, tq):
    half = lax.broadcasted_iota(jnp.int32, (tq, LANES), 1) >> 6
    return jnp.concatenate([jnp.where(half == 0, o0[j * tq:(j + 1) * tq], o1[j * tq:(j + 1) * tq])
                            for j in range(NSA_REP)], axis=1)


def _imp_matrix(nc, nsel=None):
    m = np.zeros((nc, LANES), np.float32)
    for j in range(nc * CMP_STRIDE // SEL_BLK if nsel is None else nsel):
        for c, wgt in ((4 * j - 1, 0.5), (4 * j, 1.0), (4 * j + 1, 1.0), (4 * j + 2, 1.0), (4 * j + 3, 0.5)):
            if 0 <= c < nc:
                m[c, j] += wgt
    return jnp.asarray(m, BF16)


def _select_blocks(imp_sel, cur, nsel):
    tq = imp_sel.shape[0]
    blk = lax.broadcasted_iota(jnp.int32, (tq, LANES), 1)
    forced = (blk == 0) | (blk == cur) | (blk == cur - 1)
    score = jnp.where(blk <= cur, imp_sel + jnp.where(forced, FORCE_BONUS, 0.0), -1e9)
    st = score.T
    nslab = -(-nsel // 8)
    slabs = [st[8 * v:8 * v + 8] for v in range(nslab)]
    sub = lax.broadcasted_iota(jnp.int32, (8, tq), 0)
    cnts = [jnp.zeros((8, tq), jnp.int32) for _ in range(nslab)]
    for k in range(nsel):
        sk = st[k:k + 1, :]
        for v in range(nslab):
            if 8 * v + 7 < k:
                inc = jnp.where(sk > slabs[v], 1, 0)
            elif 8 * v > k:
                inc = jnp.where(sk >= slabs[v], 1, 0)
            else:
                inc = jnp.where(sub + 8 * v > k, jnp.where(sk >= slabs[v], 1, 0), jnp.where(sk > slabs[v], 1, 0))
            cnts[v] = cnts[v] + inc
    parts = [jnp.where((c < TOP_N) & (sub + 8 * v < nsel), 1.0, 0.0) for v, c in enumerate(cnts)]
    if 8 * nslab < LANES:
        parts.append(jnp.zeros((LANES - 8 * nslab, tq), F32))
    return jnp.concatenate(parts, axis=0).T.astype(BF16)


def _nsa_kernel(qc_ref, qr_ref, kc_ref, vc_ref, ks_ref, vs_ref, m_ref, ocmp_ref, oslc_ref, *, tq, kt):
    q0 = pl.program_id(1) * tq
    nc = kc_ref.shape[0]
    nsel = nc * CMP_STRIDE // SEL_BLK
    qpos = q0 + lax.broadcasted_iota(jnp.int32, (tq, 1), 0)
    tile4 = lambda a: jnp.concatenate([a] * NSA_REP, axis=0)
    cidx = lax.broadcasted_iota(jnp.int32, (tq, nc), 1)
    cbias = tile4(jnp.where(cidx * CMP_STRIDE + (CMP_BLK - 1) <= qpos, 0.0, -1e30))
    row_ok = tile4(jnp.where(qpos >= CMP_BLK - 1, 1.0, 0.0))
    kcol = lax.broadcasted_iota(jnp.int32, (tq, kt), 1)
    eblk = (lax.broadcasted_iota(jnp.int32, (LANES, kt), 0)
            - (lax.broadcasted_iota(jnp.int32, (LANES, kt), 1) >> 6))
    n_full = q0 // kt
    o_c, o_s = [], []
    for g in range(NSA_KV):
        qcg = _stack_group(qc_ref, g, tq)
        qrg = _stack_group(qr_ref, g, tq)
        s = _dot_nt(qcg, kc_ref[...]) + cbias
        e = jnp.exp(s - jnp.max(s, axis=-1, keepdims=True))
        p = e * (row_ok / jnp.maximum(jnp.sum(e, axis=-1, keepdims=True), 1e-30))
        o_c.append(jnp.dot(p.astype(BF16), vc_ref[...], preferred_element_type=F32))
        imp = p[0:tq] + p[tq:2 * tq] + p[2 * tq:3 * tq] + p[3 * tq:4 * tq]
        sel = _select_blocks(_dot3(imp, m_ref[...]), qpos >> 6, nsel)

        def tile_step(t, carry, causal):
            m, l, acc = carry
            k0 = pl.multiple_of(t * kt, kt)
            sc = _dot_nt(qrg, ks_ref[pl.ds(k0, kt), :])
            expand = jnp.where(eblk == t * (kt // SEL_BLK), 1.0, 0.0).astype(BF16)
            bias = (jnp.dot(sel, expand, preferred_element_type=F32) - 1.0) * 1e30
            if causal:
                bias = jnp.where(k0 + kcol <= qpos, bias, -1e30)
            sc = sc + tile4(bias)
            m_new = jnp.maximum(m, jnp.max(sc, axis=-1, keepdims=True))
            a = jnp.exp(sc - m_new)
            alpha = jnp.exp(m - m_new)
            l = alpha * l + jnp.sum(a, axis=-1, keepdims=True)
            acc = alpha * acc + jnp.dot(a.astype(BF16), vs_ref[pl.ds(k0, kt), :], preferred_element_type=F32)
            return m_new, l, acc

        init = (jnp.full((NSA_REP * tq, 1), -1e30, F32), jnp.zeros((NSA_REP * tq, 1), F32),
                jnp.zeros((NSA_REP * tq, LANES), F32))
        carry = lax.fori_loop(0, n_full, lambda t, c: tile_step(t, c, False), init)
        _, l, acc = tile_step(n_full, carry, True)
        o_s.append(acc / jnp.maximum(l, 1e-30))
    ocmp_ref[...] = _unstack_pairs(o_c[0], o_c[1], tq)
    oslc_ref[...] = _unstack_pairs(o_s[0], o_s[1], tq)


def _nsa(qc, qr, kc, vc, ks, vs, b, s, tq=128, kt=512):
    nqb = s // tq
    kt = min(kt, s)
    m = _imp_matrix(s // CMP_STRIDE)
    qspec = pl.BlockSpec((tq, NSA_W), lambda bi, i: (bi * nqb + i, 0))
    cspec = pl.BlockSpec((None, s // CMP_STRIDE, KV_W), lambda bi, i: (bi, 0, 0))
    kspec = pl.BlockSpec((s, KV_W), lambda bi, i: (bi, 0))
    return pl.pallas_call(
        functools.partial(_nsa_kernel, tq=tq, kt=kt),
        grid=(b, nqb),
        in_specs=[qspec, qspec, cspec, cspec, kspec, kspec, pl.BlockSpec(m.shape, lambda bi, i: (0, 0))],
        out_specs=[qspec, qspec],
        out_shape=[jax.ShapeDtypeStruct((b * s, NSA_W), F32)] * 2,
        compiler_params=_cparams(("parallel", "parallel")),
        name="nsa",
    )(qc, qr, kc, vc, ks, vs, m)


def _window_kernel(qr_ref, k_ref, v_ref, o_ref, *, tq, span):
    q0 = pl.program_id(1) * tq
    s_len = k_ref.shape[0]
    start = pl.multiple_of(jnp.minimum(jnp.maximum(q0 - WINDOW, 0), s_len - span), tq)
    rpos = q0 + (lax.broadcasted_iota(jnp.int32, (NSA_REP * tq, 1), 0) & (tq - 1))
    k = k_ref[pl.ds(start, span), :]
    v = v_ref[pl.ds(start, span), :]
    outs = []
    for g in range(NSA_KV):
        qrg = _stack_group(qr_ref, g, tq)
        s = _dot_nt(qrg, k)
        diff = rpos - (start + lax.broadcasted_iota(jnp.int32, s.shape, 1))
        mask = (diff >= 0) & (diff < WINDOW)
        s = jnp.where(mask, s, -1e30)
        e = jnp.where(mask, jnp.exp(s - jnp.max(s, axis=-1, keepdims=True)), 0.0)
        p = e / jnp.maximum(jnp.sum(e, axis=-1, keepdims=True), 1e-30)
        outs.append(jnp.dot(p.astype(BF16), v, preferred_element_type=F32))
    o_ref[...] = _unstack_pairs(outs[0], outs[1], tq)


def _window(qr, kw, vw, b, s, tq=128):
    nqb = s // tq
    span = min(WINDOW + tq, s)
    qspec = pl.BlockSpec((tq, NSA_W), lambda bi, i: (bi * nqb + i, 0))
    kspec = pl.BlockSpec((s, KV_W), lambda bi, i: (bi, 0))
    return pl.pallas_call(
        functools.partial(_window_kernel, tq=tq, span=span),
        grid=(b, nqb),
        in_specs=[qspec, kspec, kspec],
        out_specs=qspec,
        out_shape=jax.ShapeDtypeStruct((b * s, NSA_W), F32),
        compiler_params=_cparams(("parallel", "parallel")),
        name="window",
    )(qr, kw, vw)


def _softplus(z):
    return jnp.maximum(z, 0.0) + jnp.log(1.0 + jnp.exp(-jnp.abs(z)))


def _tri_matrix(n):
    return jnp.asarray(np.tril(np.ones((n, n), np.float32)), BF16)


def _sb_block(qh, k_t, v_t, tri, carry, acc, valid):
    z = _dot_nt(qh, k_t)
    sp = _softplus(z)
    if valid is not None:
        sp = jnp.where(valid, sp, 0.0)
    cum = jnp.dot(sp.astype(BF16), tri, preferred_element_type=F32) + carry
    a = jnp.exp(z - cum)
    if valid is not None:
        a = jnp.where(valid, a, 0.0)
    acc = acc + jnp.dot(a.astype(BF16), v_t, preferred_element_type=F32)
    return cum[:, 0:1], acc


def _sb_two_blocks(qh, k_ref, v_ref, tri, ja, jb, state, mask_a, ok_b, tq):
    a0 = pl.multiple_of(ja * tq, tq)
    b0 = pl.multiple_of(jb * tq, tq)
    k_a, v_a = k_ref[pl.ds(a0, tq), :], v_ref[pl.ds(a0, tq), :]
    k_b, v_b = k_ref[pl.ds(b0, tq), :], v_ref[pl.ds(b0, tq), :]
    out = []
    for hh in range(2):
        carry, acc = state[2 * hh], state[2 * hh + 1]
        za = _dot_nt(qh[hh], k_a)
        zb = _dot_nt(qh[hh], k_b)
        spa = _softplus(za)
        spb = _softplus(zb)
        if mask_a is not None:
            spa = jnp.where(mask_a, spa, 0.0)
        if ok_b is not None:
            spb = jnp.where(ok_b, spb, 0.0)
        cuma = jnp.dot(spa.astype(BF16), tri, preferred_element_type=F32) + carry
        cumb = jnp.dot(spb.astype(BF16), tri, preferred_element_type=F32) + cuma[:, 0:1]
        aa = jnp.exp(za - cuma)
        ab = jnp.exp(zb - cumb)
        if mask_a is not None:
            aa = jnp.where(mask_a, aa, 0.0)
        if ok_b is not None:
            ab = jnp.where(ok_b, ab, 0.0)
        acc = (acc + jnp.dot(aa.astype(BF16), v_a, preferred_element_type=F32)
               + jnp.dot(ab.astype(BF16), v_b, preferred_element_type=F32))
        out.extend((cumb[:, 0:1], acc))
    return tuple(out)


def _sb_kernel(q_ref, k_ref, v_ref, tri_ref, o_ref, *, tq):
    i = pl.program_id(2)
    half = lax.broadcasted_iota(jnp.int32, (tq, LANES), 1) >> 6
    q = q_ref[...]
    qh = [jnp.where(half == hh, q, 0) for hh in range(2)]
    tri = tri_ref[...]
    odd = (i & 1) == 1
    causal = lax.broadcasted_iota(jnp.int32, (tq, tq), 1) < lax.broadcasted_iota(jnp.int32, (tq, tq), 0)
    zero = (jnp.zeros((tq, 1), F32), jnp.zeros((tq, LANES), F32))
    state = _sb_two_blocks(qh, k_ref, v_ref, tri, i, jnp.maximum(i - 1, 0), zero + zero, causal, odd, tq)
    first = i - 1 - (i & 1)

    def body(p, st):
        ja = first - 2 * p
        return _sb_two_blocks(qh, k_ref, v_ref, tri, ja, ja - 1, st, None, None, tq)

    st = lax.fori_loop(0, i >> 1, body, state)
    o_ref[...] = jnp.where(half == 0, st[1], st[3])


def _sb(sbq, sbk, sbv, b, s, tq=256):
    tq = min(tq, s)
    nqb = s // tq
    tri = _tri_matrix(tq)
    npair = SB_HEADS // 2
    qspec = pl.BlockSpec((tq, LANES), lambda bi, p, i: (bi * nqb + i, p))
    kspec = pl.BlockSpec((s, LANES), lambda bi, p, i: (bi, p))
    return pl.pallas_call(
        functools.partial(_sb_kernel, tq=tq),
        grid=(b, npair, nqb),
        in_specs=[qspec, kspec, kspec, pl.BlockSpec(tri.shape, lambda bi, p, i: (0, 0))],
        out_specs=qspec,
        out_shape=jax.ShapeDtypeStruct((b * s, SB_W), F32),
        compiler_params=_cparams(("parallel", "parallel", "parallel")),
        name="sb",
    )(sbq, sbk, sbv, tri)


def _prep_merge(w_out, w_router, b_router):
    w_a = [w_out[h * HEAD_DIM:(h + 1) * HEAD_DIM] for h in _HEAD_PERM]
    w_out_p = jnp.concatenate(w_a + [w_out[NSA_W:]], axis=0).astype(BF16)
    e = np.zeros((LANES, 3 * NSA_W), np.float32)
    for br in range(3):
        for p, h in enumerate(_HEAD_PERM):
            e[br * NSA_HEADS + h, br * NSA_W + p * HEAD_DIM: br * NSA_W + (p + 1) * HEAD_DIM] = 1.0
    w_r = jnp.concatenate([w_router, jnp.zeros((D_MODEL, LANES - N_EXPERTS), w_router.dtype)], axis=1)
    b_r = jnp.concatenate([b_router, jnp.full((LANES - N_EXPERTS,), -1e30, b_router.dtype)]).reshape(1, LANES)
    return w_out_p, jnp.asarray(e, BF16), w_r, b_r


def _merge_kernel(x_ref, oc_ref, os_ref, ow_ref, ob_ref, gate_ref, br_ref, e_ref, wo_ref, g_ref, wr_ref, brt_ref,
                  x1_ref, hp_ref, idx_ref, wts_ref):
    gexp = _dot3(br_ref[...], e_ref[...])
    o_a = gexp[:, :NSA_W] * oc_ref[...] + gexp[:, NSA_W:2 * NSA_W] * os_ref[...] + gexp[:, 2 * NSA_W:] * ow_ref[...]
    gate = gate_ref[...].astype(F32)
    cat = jnp.concatenate([gate[:, :NSA_W] * o_a, gate[:, NSA_W:] * ob_ref[...]], axis=1).astype(BF16)
    x1 = x_ref[...] + jnp.dot(cat, wo_ref[...], preferred_element_type=F32)
    x1_ref[...] = x1
    ms = jnp.mean(x1 * x1, axis=-1, keepdims=True)
    h = (x1 * lax.rsqrt(ms + RMS_EPS)) * g_ref[...]
    hb = h.astype(BF16).astype(F32)
    for j in range(D_MODEL // LANES):
        hp_ref[:, j, :] = hb[:, j * LANES:(j + 1) * LANES]
    lg = jnp.dot(h, wr_ref[...], preferred_element_type=F32, precision=lax.Precision.HIGHEST) + brt_ref[...]
    lane = lax.broadcasted_iota(jnp.int32, lg.shape, 1)
    vals, ids = [], []
    for _ in range(TOP_K):
        mx = jnp.max(lg, axis=-1, keepdims=True)
        first = jnp.min(jnp.where(lg == mx, lane, LANES), axis=-1, keepdims=True)
        vals.append(mx)
        ids.append(first)
        lg = jnp.where(lane == first, -jnp.inf, lg)
    es = [jnp.exp(v - vals[0]) for v in vals]
    tot = es[0] + es[1] + es[2] + es[3]
    idx = jnp.zeros(lg.shape, jnp.int32)
    wts = jnp.zeros(lg.shape, F32)
    for k in range(TOP_K):
        idx = jnp.where(lane == k, ids[k], idx)
        wts = jnp.where(lane == k, es[k] / tot, wts)
    idx_ref[...] = idx
    wts_ref[...] = wts


def _merge(x2d, oc, os_, ow, ob, gate, br, prep, g_ffn, tm=256):
    t = x2d.shape[0]
    tm = min(tm, t)
    assert t % tm == 0
    w_out_p, e, w_r, b_r = prep
    row = lambda n: pl.BlockSpec((tm, n), lambda i: (i, 0))
    full = lambda a: pl.BlockSpec(a.shape, lambda i: (0,) * a.ndim)
    return pl.pallas_call(
        _merge_kernel,
        grid=(t // tm,),
        in_specs=[row(D_MODEL), row(NSA_W), row(NSA_W), row(NSA_W), row(SB_W), row(NSA_W + SB_W), row(LANES),
                  full(e), full(w_out_p), full(g_ffn), full(w_r), full(b_r)],
        out_specs=[row(D_MODEL), pl.BlockSpec((tm, ROW_SUB, LANES), lambda i: (i, 0, 0)), row(LANES), row(LANES)],
        out_shape=[jax.ShapeDtypeStruct((t, D_MODEL), F32), jax.ShapeDtypeStruct((t, ROW_SUB, LANES), F32),
                   jax.ShapeDtypeStruct((t, LANES), jnp.int32), jax.ShapeDtypeStruct((t, LANES), F32)],
        compiler_params=_cparams(("parallel",)),
        name="merge",
    )(x2d, oc, os_, ow, ob, gate, br, e, w_out_p, g_ffn, w_r, b_r)


MOE_TILE = 256
COMBINE_ROWS = 128
ROW_SUB = D_MODEL // LANES


def _route_tables(idx):
    t = idx.shape[0]
    a = t * TOP_K
    assert a % MOE_TILE == 0
    n_tiles = a // MOE_TILE + N_EXPERTS
    e = idx.reshape(a)
    onehot = (e[:, None] == jnp.arange(N_EXPERTS, dtype=jnp.int32)[None, :]).astype(jnp.int32)
    csum = jnp.cumsum(onehot, axis=0)
    rank = jnp.take_along_axis(csum, e[:, None], axis=1)[:, 0] - 1
    counts = csum[-1]
    tiles_e = (counts + MOE_TILE - 1) // MOE_TILE
    tile_end = jnp.cumsum(tiles_e)
    tile_start = tile_end - tiles_e
    dest = tile_start[e] * MOE_TILE + rank
    row_token = jnp.zeros((n_tiles * MOE_TILE,), jnp.int32).at[dest].set(jnp.arange(a, dtype=jnp.int32) // TOP_K)
    tile_ids = jnp.arange(n_tiles, dtype=jnp.int32)
    tile_expert = jnp.minimum(jnp.sum((tile_end[None, :] <= tile_ids[:, None]).astype(jnp.int32), axis=1),
                              N_EXPERTS - 1).astype(jnp.int32)
    return (row_token.reshape(n_tiles, 1, MOE_TILE), dest.reshape(t, TOP_K), tile_expert,
            tile_end[-1:].astype(jnp.int32))


def _deinterleave_matrices():
    e = np.zeros((2 * LANES, LANES), np.float32)
    o = np.zeros((2 * LANES, LANES), np.float32)
    e[2 * np.arange(LANES), np.arange(LANES)] = 1.0
    o[2 * np.arange(LANES) + 1, np.arange(LANES)] = 1.0
    return jnp.asarray(e, BF16), jnp.asarray(o, BF16)


def _row_tiles_to_2d(load):
    return jnp.concatenate([load(j) for j in range(ROW_SUB)], axis=1)


def _expert_kernel(te_ref, nu_ref, rt_ref, rtn_ref, h_ref, wgu_ref, wdn_ref, bg_ref, bu_ref, bd_ref, se_ref, so_ref,
                   y_ref, xbuf, sem, wg_s, wu_s, wd_s):
    i = pl.program_id(0)
    n_used = nu_ref[0]
    slot = i % 2

    def copy(idx_ref, r, s):
        return pltpu.make_async_copy(h_ref.at[pl.ds(idx_ref[0, r], 1)], xbuf.at[s, pl.ds(r, 1)], sem.at[s])

    def issue(idx_ref, s):
        def body(r, c):
            copy(idx_ref, r, s).start()
            return c
        lax.fori_loop(0, MOE_TILE, body, 0, unroll=8)

    def wait_rows():
        def wait(r, c):
            copy(rt_ref, r, slot).wait()
            return c
        lax.fori_loop(0, MOE_TILE, wait, 0, unroll=8)

    @pl.when(i == 0)
    def _():
        issue(rt_ref, 0)

    @pl.when(i < n_used)
    def _():
        @pl.when((i == 0) | (te_ref[i] != te_ref[jnp.maximum(i - 1, 0)]))
        def _():
            for k in range(D_FF // LANES):
                wblk = wgu_ref[:, 2 * LANES * k:2 * LANES * (k + 1)].astype(BF16)
                wg_s[:, LANES * k:LANES * (k + 1)] = jnp.dot(wblk, se_ref[...], preferred_element_type=F32).astype(BF16)
                wu_s[:, LANES * k:LANES * (k + 1)] = jnp.dot(wblk, so_ref[...], preferred_element_type=F32).astype(BF16)
            wd_s[...] = wdn_ref[...].astype(BF16)

        wait_rows()
        x = _row_tiles_to_2d(lambda j: xbuf[slot, :, j, :]).astype(BF16)
        g = jnp.minimum(jnp.dot(x, wg_s[...], preferred_element_type=F32) + bg_ref[...], SWIGLU_LIMIT)
        u = jnp.clip(jnp.dot(x, wu_s[...], preferred_element_type=F32) + bu_ref[...], -SWIGLU_LIMIT, SWIGLU_LIMIT)
        act = (u + 1.0) * g * jax.nn.sigmoid(SWIGLU_ALPHA * g)
        y = jnp.dot(act.astype(BF16), wd_s[...], preferred_element_type=F32) + bd_ref[...]
        for j in range(ROW_SUB):
            y_ref[:, j, :] = y[:, j * LANES:(j + 1) * LANES]
        for r in range(MOE_TILE):
            copy(rtn_ref, r, 1 - slot).start(priority=r % 2)

    @pl.when(i == n_used)
    def _():
        wait_rows()

    @pl.when(i >= n_used)
    def _():
        y_ref[...] = jnp.zeros(y_ref.shape, F32)


def _experts(h_rows, row_token, tile_expert, n_used, w_gu, w_dn, bg, bu, bd):
    n_tiles = row_token.shape[0]
    se, so = _deinterleave_matrices()
    wspec = lambda shape: pl.BlockSpec((None,) + shape, lambda i, te, nu: (te[i], 0, 0))
    full = lambda a: pl.BlockSpec(a.shape, lambda i, te, nu: (0,) * a.ndim)
    rt_spec = lambda off: pl.BlockSpec((None, 1, MOE_TILE), lambda i, te, nu: (jnp.minimum(i + off, n_tiles - 1), 0, 0),
                                       memory_space=pltpu.SMEM)
    return pl.pallas_call(
        _expert_kernel,
        grid_spec=pltpu.PrefetchScalarGridSpec(
            num_scalar_prefetch=2,
            grid=(n_tiles,),
            in_specs=[rt_spec(0), rt_spec(1), pl.BlockSpec(memory_space=pl.ANY),
                      wspec((D_MODEL, 2 * D_FF)), wspec((D_FF, D_MODEL)), wspec((1, D_FF)), wspec((1, D_FF)),
                      wspec((1, D_MODEL)), full(se), full(so)],
            out_specs=pl.BlockSpec((MOE_TILE, ROW_SUB, LANES), lambda i, te, nu: (i, 0, 0)),
            scratch_shapes=[pltpu.VMEM((2, MOE_TILE, ROW_SUB, LANES), F32), pltpu.SemaphoreType.DMA((2,)),
                            pltpu.VMEM((D_MODEL, D_FF), BF16), pltpu.VMEM((D_MODEL, D_FF), BF16),
                            pltpu.VMEM((D_FF, D_MODEL), BF16)],
        ),
        out_shape=jax.ShapeDtypeStruct((n_tiles * MOE_TILE, ROW_SUB, LANES), F32),
        compiler_params=_cparams(("arbitrary",)),
        name="moe_experts",
    )(tile_expert, n_used, row_token, row_token, h_rows, w_gu, w_dn, bg, bu, bd, se, so)


def _combine_kernel(pos_ref, posn_ref, x_ref, wts_ref, g_ref, y_ref, out_ref, buf, sem, *, n):
    i = pl.program_id(0)
    n_rows = TOP_K * COMBINE_ROWS

    def copy(p_ref, r, slot):
        return pltpu.make_async_copy(y_ref.at[pl.ds(p_ref[0, r], 1)],
                                     buf.at[slot, r & (TOP_K - 1), pl.ds(r >> 2, 1)], sem.at[slot])

    def issue(p_ref, slot):
        def body(r2, c):
            copy(p_ref, 2 * r2, slot).start(priority=0)
            copy(p_ref, 2 * r2 + 1, slot).start(priority=1)
            return c
        lax.fori_loop(0, n_rows // 2, body, 0, unroll=4)

    slot = i % 2

    @pl.when(i == 0)
    def _():
        issue(pos_ref, 0)

    @pl.when(i + 1 < n)
    def _():
        issue(posn_ref, 1 - slot)

    def wait(r, c):
        copy(pos_ref, r, slot).wait()
        return c
    lax.fori_loop(0, n_rows, wait, 0, unroll=8)

    w = wts_ref[...]
    parts = []
    ssq = jnp.zeros((COMBINE_ROWS, 1), F32)
    for j in range(ROW_SUB):
        xs = x_ref[:, j * LANES:(j + 1) * LANES]
        for k in range(TOP_K):
            xs = xs + w[:, k:k + 1] * buf[slot, k, :, j, :]
        parts.append(xs)
        ssq = ssq + jnp.sum(xs * xs, axis=-1, keepdims=True)
    rs = lax.rsqrt(ssq * (1.0 / D_MODEL) + RMS_EPS)
    for j in range(ROW_SUB):
        out_ref[:, j * LANES:(j + 1) * LANES] = (parts[j] * rs) * g_ref[:, j * LANES:(j + 1) * LANES]


def _combine(x1, wts, pos, y_sorted, g_final):
    t = x1.shape[0]
    assert t % COMBINE_ROWS == 0
    steps = t // COMBINE_ROWS
    pos2 = pos.reshape(steps, 1, TOP_K * COMBINE_ROWS)
    row = lambda nn: pl.BlockSpec((COMBINE_ROWS, nn), lambda i: (i, 0))
    return pl.pallas_call(
        functools.partial(_combine_kernel, n=steps),
        grid=(steps,),
        in_specs=[pl.BlockSpec((None, 1, TOP_K * COMBINE_ROWS), lambda i: (i, 0, 0), memory_space=pltpu.SMEM),
                  pl.BlockSpec((None, 1, TOP_K * COMBINE_ROWS), lambda i: (jnp.minimum(i + 1, steps - 1), 0, 0),
                               memory_space=pltpu.SMEM),
                  row(D_MODEL), row(LANES), pl.BlockSpec(g_final.shape, lambda i: (0, 0)),
                  pl.BlockSpec(memory_space=pl.ANY)],
        out_specs=row(D_MODEL),
        out_shape=jax.ShapeDtypeStruct((t, D_MODEL), F32),
        scratch_shapes=[pltpu.VMEM((2, TOP_K, COMBINE_ROWS, ROW_SUB, LANES), F32), pltpu.SemaphoreType.DMA((2,))],
        compiler_params=_cparams(("arbitrary",)),
        name="moe_combine",
    )(pos2, pos2, x1, wts, g_final, y_sorted)


QPAD = 8


def _select_blocks_rows(imp_sel, cur, nsel):
    tq = imp_sel.shape[0]
    blk = lax.broadcasted_iota(jnp.int32, (tq, LANES), 1)
    forced = (blk == 0) | (blk == cur) | (blk == cur - 1)
    score = jnp.where(blk <= cur, imp_sel + jnp.where(forced, FORCE_BONUS, 0.0), -1e9)
    cnt = jnp.zeros((tq, LANES), jnp.int32)
    for k in range(nsel):
        sk = score[:, k:k + 1]
        beats = (sk > score) | ((sk == score) & (blk > k))
        cnt = cnt + jnp.where(beats, 1, 0)
    return jnp.where((cnt < TOP_N) & (blk < nsel), 1.0, 0.0)


def _pad_rows(x, n):
    return jnp.concatenate([x, jnp.zeros((n - x.shape[0], x.shape[1]), x.dtype)], axis=0)


NSA_S_SEQS = 4
SB_S_SEQS = 2


def _nsa_s_kernel(pt_ref, qc_ref, qr_ref, xnew_ref, snew_ref, w1_ref, w2_ref, m_ref, *rest, n_pages, past, dec, seqs):
    ocmp_ref, oslc_ref = rest[2 * seqs * n_pages:]
    pages = [rest[2 * n_pages * s:2 * n_pages * (s + 1)] for s in range(seqs)]
    nc = n_pages * (PAGE_SIZE // CMP_STRIDE)
    xb = jnp.concatenate([p[...] for s in range(seqs) for p in pages[s][:n_pages]], axis=0).astype(BF16)
    y1 = jnp.dot(xb, w1_ref[...], preferred_element_type=F32)
    y2 = jnp.dot(xb, w2_ref[...], preferred_element_type=F32)
    xn = jnp.concatenate([xnew_ref[s] for s in range(seqs)] + [xnew_ref[0]] * (8 - seqs), axis=0)
    ynew = jnp.dot(xn.astype(BF16), w2_ref[...], preferred_element_type=F32)
    rid = lax.broadcasted_iota(jnp.int32, (nc, y2.shape[1]), 0)
    for s in range(seqs):
        y2s = pltpu.roll(y2[s * nc:(s + 1) * nc], nc - 1, 0)
        kcv = y1[s * nc:(s + 1) * nc] + jnp.where(rid < nc - 1, y2s, ynew[s:s + 1])
        _nsa_s_one(qc_ref.at[s], qr_ref.at[s], snew_ref.at[s], m_ref, kcv[:, :KV_W].astype(BF16),
                   kcv[:, KV_W:].astype(BF16), pages[s][n_pages:], ocmp_ref.at[s], oslc_ref.at[s], n_pages, past, dec)


def _nsa_s_one(qc_ref, qr_ref, snew_ref, m_ref, kc, vc, slc_pages, ocmp_ref, oslc_ref, n_pages, past, dec):
    tq = QPAD
    nsel = past // SEL_BLK + 1
    trow = lax.broadcasted_iota(jnp.int32, (NSA_REP * tq, 1), 0) & (tq - 1)
    rpos = past + jnp.minimum(trow, dec - 1)
    qpos = past + jnp.minimum(lax.broadcasted_iota(jnp.int32, (tq, 1), 0), dec - 1)
    lane = lax.broadcasted_iota(jnp.int32, (tq, LANES), 1)
    snew = snew_ref[...]
    k_all = jnp.concatenate([pg[:, :KV_W].astype(BF16) for pg in slc_pages]
                            + [_pad_rows(snew[:, :KV_W], LANES).astype(BF16)], axis=0)
    v_all = jnp.concatenate([pg[:, KV_W:].astype(BF16) for pg in slc_pages]
                            + [_pad_rows(snew[:, KV_W:], LANES).astype(BF16)], axis=0)
    o_c, o_s = [], []
    for g in range(NSA_KV):
        qcg = _stack_group(qc_ref, g, tq)
        qrg = _stack_group(qr_ref, g, tq)
        s = _dot_nt(qcg, kc)
        cidx = lax.broadcasted_iota(jnp.int32, s.shape, 1)
        cmask = cidx * CMP_STRIDE + (CMP_BLK - 1) <= rpos
        s = jnp.where(cmask, s, -1e30)
        e = jnp.where(cmask, jnp.exp(s - jnp.max(s, axis=-1, keepdims=True)), 0.0)
        p = e / jnp.maximum(jnp.sum(e, axis=-1, keepdims=True), 1e-30)
        o_c.append(jnp.dot(p.astype(BF16), vc, preferred_element_type=F32))
        imp = p[0:tq] + p[tq:2 * tq] + p[2 * tq:3 * tq] + p[3 * tq:4 * tq]
        sel = _select_blocks_rows(_dot3(imp, m_ref[...]), qpos >> 6, nsel)
        masks =[jnp.where(lane < SEL_BLK, sel[:, 2 * j:2 * j + 1], sel[:, 2 * j + 1:2 * j + 2]) > 0.5
                 for j in range(n_pages)]
        masks.append((sel[:, nsel - 1:nsel] > 0.5) & (lane < dec) & (past + lane <= qpos))
        valid = jnp.concatenate([jnp.concatenate(masks, axis=1)] * NSA_REP, axis=0)
        sc = jnp.where(valid, _dot_nt(qrg, k_all), -1e30)
        a = jnp.where(valid, jnp.exp(sc - jnp.max(sc, axis=-1, keepdims=True)), 0.0)
        l = jnp.sum(a, axis=-1, keepdims=True)
        o_s.append(jnp.dot(a.astype(BF16), v_all, preferred_element_type=F32) / jnp.maximum(l, 1e-30))
    ocmp_ref[...] = _unstack_pairs(o_c[0], o_c[1], tq)
    oslc_ref[...] = _unstack_pairs(o_s[0], o_s[1], tq)


def _nsa_s(page_table, qc, qr, xnew, snew, w1, w2, cache_cmp, cache_slc, past, dec):
    bd, n_pages = page_table.shape
    seqs = NSA_S_SEQS if bd % NSA_S_SEQS == 0 else 1
    cpp = PAGE_SIZE // CMP_STRIDE
    cmp_pages = cache_cmp.reshape(cache_cmp.shape[0], cpp, CMP_STRIDE * 2 * KV_W)
    m = _imp_matrix(n_pages * cpp, past // SEL_BLK + 1)
    per_b = lambda a: pl.BlockSpec((seqs,) + a.shape[1:], lambda b, pt: (b,) + (0,) * (a.ndim - 1))
    full = lambda a: pl.BlockSpec(a.shape, lambda b, pt: (0,) * a.ndim)
    page = lambda a, s, j: pl.BlockSpec((None,) + a.shape[1:], lambda b, pt: (pt[b * seqs + s, j], 0, 0))
    ospec = pl.BlockSpec((seqs, QPAD, NSA_W), lambda b, pt: (b, 0, 0))
    page_specs, page_args = [], []
    for s in range(seqs):
        page_specs += [page(cmp_pages, s, j) for j in range(n_pages)] + [page(cache_slc, s, j) for j in range(n_pages)]
        page_args += [cmp_pages] * n_pages + [cache_slc] * n_pages
    return pl.pallas_call(
        functools.partial(_nsa_s_kernel, n_pages=n_pages, past=past, dec=dec, seqs=seqs),
        grid_spec=pltpu.PrefetchScalarGridSpec(
            num_scalar_prefetch=1,
            grid=(bd // seqs,),
            in_specs=[per_b(qc), per_b(qr), per_b(xnew), per_b(snew), full(w1), full(w2), full(m)] + page_specs,
            out_specs=[ospec, ospec],
        ),
        out_shape=[jax.ShapeDtypeStruct((bd, QPAD, NSA_W), F32)] * 2,
        compiler_params=_cparams(("arbitrary",)),
        name="nsa_sample",
    )(page_table, qc, qr, xnew, snew, w1, w2, m, *page_args)


def _window_s_kernel(qr_ref, st_ref, wnew_ref, o_ref, *, past, dec):
    tq = QPAD
    wbuf = st_ref.shape[0]
    trow = jnp.minimum(lax.broadcasted_iota(jnp.int32, (NSA_REP * tq, 1), 0) & (tq - 1), dec - 1)
    k = st_ref[:, :KV_W].astype(BF16)
    v = st_ref[:, KV_W:].astype(BF16)
    wnew = wnew_ref[...]
    kn = _pad_rows(wnew[:, :KV_W], LANES).astype(BF16)
    vn = _pad_rows(wnew[:, KV_W:], LANES).astype(BF16)
    outs = []
    for g in range(NSA_KV):
        qrg = _stack_group(qr_ref, g, tq)
        s0 = _dot_nt(qrg, k)
        d0 = wbuf + trow - lax.broadcasted_iota(jnp.int32, s0.shape, 1)
        m0 = (d0 >= 0) & (d0 < WINDOW)
        s1 = _dot_nt(qrg, kn)
        i1 = lax.broadcasted_iota(jnp.int32, s1.shape, 1)
        m1 = (i1 < dec) & (trow - i1 >= 0) & (trow - i1 < WINDOW)
        s0 = jnp.where(m0, s0, -1e30)
        s1 = jnp.where(m1, s1, -1e30)
        mx = jnp.maximum(jnp.max(s0, axis=-1, keepdims=True), jnp.max(s1, axis=-1, keepdims=True))
        e0 = jnp.where(m0, jnp.exp(s0 - mx), 0.0)
        e1 = jnp.where(m1, jnp.exp(s1 - mx), 0.0)
        den = jnp.maximum(jnp.sum(e0, axis=-1, keepdims=True) + jnp.sum(e1, axis=-1, keepdims=True), 1e-30)
        o = (jnp.dot(e0.astype(BF16), v, preferred_element_type=F32)
             + jnp.dot(e1.astype(BF16), vn, preferred_element_type=F32))
        outs.append(o / den)
    o_ref[...] = _unstack_pairs(outs[0], outs[1], tq)


def _window_s(qr, state, wnew, past, dec):
    bd = qr.shape[0]
    per_b = lambda a: pl.BlockSpec((None,) + a.shape[1:], lambda b: (b,) + (0,) * (a.ndim - 1))
    return pl.pallas_call(
        functools.partial(_window_s_kernel, past=past, dec=dec),
        grid=(bd,),
        in_specs=[per_b(qr), per_b(state), per_b(wnew)],
        out_specs=pl.BlockSpec((None, QPAD, NSA_W), lambda b: (b, 0, 0)),
        out_shape=jax.ShapeDtypeStruct((bd, QPAD, NSA_W), F32),
        compiler_params=_cparams(("parallel",)),
        name="window_sample",
    )(qr, state, wnew)


def _sb_s_kernel(pt_ref, q_ref, new_ref, tri_ref, *rest, n_pages, seqs):
    o_ref = rest[seqs * n_pages]
    for s in range(seqs):
        _sb_s_one(q_ref.at[s], new_ref.at[s], tri_ref, rest[n_pages * s:n_pages * (s + 1)], o_ref.at[s], n_pages)


def _sb_s_one(q_ref, new_ref, tri_ref, pages, o_ref, n_pages):
    tq = QPAD
    nrow = SB_HEADS * tq
    row = lax.broadcasted_iota(jnp.int32, (nrow, SB_W), 0)
    col = lax.broadcasted_iota(jnp.int32, (nrow, SB_W), 1)
    own = (row >> 3) == (col >> 6)
    qbd = jnp.where(own, jnp.concatenate([q_ref[...]] * SB_HEADS, axis=0), 0.0).astype(BF16)
    new = new_ref[...]
    nblk = n_pages + 1
    k_all = jnp.concatenate([pg[:, :SB_W].astype(BF16) for pg in pages]
                            + [_pad_rows(new[:, :SB_W], PAGE_SIZE).astype(BF16)], axis=0)
    v_all = jnp.concatenate([pg[:, SB_W:].astype(BF16) for pg in pages]
                            + [_pad_rows(new[:, SB_W:], PAGE_SIZE).astype(BF16)], axis=0)
    z_all = _dot_nt(qbd, k_all)
    blk = lambda x, j: x[j * nrow:(j + 1) * nrow]
    zs = jnp.concatenate([z_all[:, j * PAGE_SIZE:(j + 1) * PAGE_SIZE] for j in range(nblk)], axis=0)
    lane = lax.broadcasted_iota(jnp.int32, (nrow, PAGE_SIZE), 1)
    trow = lax.broadcasted_iota(jnp.int32, (nrow, PAGE_SIZE), 0) & (tq - 1)
    causal_new = lane < trow
    sp = _softplus(zs)
    sp = jnp.concatenate([sp[:n_pages * nrow], jnp.where(causal_new, blk(sp, n_pages), 0.0)], axis=0)
    p = jnp.dot(sp.astype(BF16), tri_ref[...], preferred_element_type=F32)
    carry = jnp.zeros((nrow, 1), F32)
    carries = [None] * nblk
    for j in reversed(range(nblk)):
        carries[j] = jnp.broadcast_to(carry, (nrow, PAGE_SIZE))
        carry = carry + blk(p, j)[:, 0:1]
    a = jnp.exp(zs - (p + jnp.concatenate(carries, axis=0)))
    a_wide = jnp.concatenate([blk(a, j) for j in range(n_pages)] + [jnp.where(causal_new, blk(a, n_pages), 0.0)],
                             axis=1)
    acc = jnp.dot(a_wide.astype(BF16), v_all, preferred_element_type=F32)
    acc = jnp.where(own, acc, 0.0)
    out = acc[0:tq]
    for h in range(1, SB_HEADS):
        out = out + acc[h * tq:(h + 1) * tq]
    o_ref[...] = out


def _sb_s(page_table, sbq, sbnew, cache_sb):
    bd, n_pages = page_table.shape
    seqs = SB_S_SEQS if bd % SB_S_SEQS == 0 else 1
    tri = _tri_matrix(PAGE_SIZE)
    per_b = lambda a: pl.BlockSpec((seqs,) + a.shape[1:], lambda b, pt: (b,) + (0,) * (a.ndim - 1))
    page = lambda s, j: pl.BlockSpec((None,) + cache_sb.shape[1:], lambda b, pt: (pt[b * seqs + s, j], 0, 0))
    return pl.pallas_call(
        functools.partial(_sb_s_kernel, n_pages=n_pages, seqs=seqs),
        grid_spec=pltpu.PrefetchScalarGridSpec(
            num_scalar_prefetch=1,
            grid=(bd // seqs,),
            in_specs=[per_b(sbq), per_b(sbnew), pl.BlockSpec(tri.shape, lambda b, pt: (0, 0))]
            + [page(s, j) for s in range(seqs) for j in range(n_pages)],
            out_specs=pl.BlockSpec((seqs, QPAD, SB_W), lambda b, pt: (b, 0, 0)),
        ),
        out_shape=jax.ShapeDtypeStruct((bd, QPAD, SB_W), F32),
        compiler_params=_cparams(("arbitrary",)),
        name="sb_sample",
    )(page_table, sbq, sbnew, tri, *([cache_sb] * (seqs * n_pages)))


def kernel(x_prompt, x_sample, cache_nsa_cmp, cache_nsa_slc, cache_sb, state_nsa_win, page_table, g_mix, w_in, w_ck,
           w_cv, w_out, g_ffn, w_router, b_router, w_gu, b_gu, w_dn, b_dn, g_final):
    b, s, d = x_prompt.shape
    bd, ds, _ = x_sample.shape
    depth = g_mix.shape[0]
    assert depth == 1 and d == D_MODEL and ds <= QPAD
    n_pages = page_table.shape[1]
    past = n_pages * PAGE_SIZE
    wbuf = state_nsa_win.shape[2]
    n_phys = cache_nsa_cmp.shape[1]
    tp, ts = b * s, bd * ds

    w_perm = _prep_w_in(w_in[0])
    w1, w2 = _prep_w_cmp(w_ck[0], w_cv[0])
    prep = _prep_merge(w_out[0], w_router[0], b_router[0])
    g_mix2, g_ffn2, g_fin2 = g_mix[0].reshape(1, d), g_ffn[0].reshape(1, d), g_final.reshape(1, d)
    bg = b_gu[0][:, None, 0::2]
    bu = b_gu[0][:, None, 1::2]
    bdn = b_dn[0][:, None, :]

    pos_p = jnp.arange(s, dtype=jnp.int32)
    (qc, qr, sbq, cmp_p, slc_p, win_p, sbkv_p, kslc, vslc, kwin, vwin, sbk, sbv, gate_p, br_p) = _proj(
        x_prompt.reshape(tp, d), g_mix2, w_perm, _rope_tables(pos_p), 256)
    kc, vc = _compress(cmp_p, w1, w2, b, s)
    o_cmp, o_slc = _nsa(qc, qr, kc, vc, kslc, vslc, b, s)
    o_win = _window(qr, kwin, vwin, b, s)
    o_sb = _sb(sbq, sbk, sbv, b, s)
    x1_p, h_p, idx_p, wts_p = _merge(x_prompt.reshape(tp, d), o_cmp, o_slc, o_win, o_sb, gate_p, br_p, prep, g_ffn2)

    pos_s = past + jnp.arange(ds, dtype=jnp.int32)
    tabs_s = [jnp.tile(t, (bd, 1)) for t in _rope_tables(pos_s)]
    (qc_s, qr_s, sbq_s, cmp_s, slc_s, win_s, sbkv_s, _, _, _, _, _, _, gate_s, br_s) = _proj(
        x_sample.reshape(ts, d), g_mix2, w_perm, tabs_s, min(256, ts))
    pad_q = lambda a: jnp.pad(a.astype(F32).reshape(bd, ds, a.shape[1]), ((0, 0), (0, QPAD - ds), (0, 0)))
    xnew = jnp.pad(cmp_s.reshape(bd, 1, ds * 2 * KV_W), ((0, 0), (0, 0), (0, (CMP_STRIDE - ds) * 2 * KV_W)))
    o_cmp_s, o_slc_s = _nsa_s(page_table, pad_q(qc_s), pad_q(qr_s), xnew, pad_q(slc_s), w1, w2,
                              cache_nsa_cmp[0].reshape(n_phys, PAGE_SIZE, 2 * KV_W),
                              cache_nsa_slc[0].reshape(n_phys, PAGE_SIZE, 2 * KV_W).astype(BF16), past, ds)
    o_win_s = _window_s(pad_q(qr_s), state_nsa_win[0].reshape(bd, wbuf, 2 * KV_W).astype(BF16), pad_q(win_s),
                        past, ds)
    o_sb_s = _sb_s(page_table, pad_q(sbq_s), pad_q(sbkv_s),
                   cache_sb[0].reshape(n_phys, PAGE_SIZE, 2 * SB_W).astype(BF16))
    unpad = lambda a: a[:, :ds].reshape(ts, a.shape[2])
    x1_s, h_s, idx_s, wts_s = _merge(x_sample.reshape(ts, d), unpad(o_cmp_s), unpad(o_slc_s), unpad(o_win_s),
                                     unpad(o_sb_s), gate_s, br_s, prep, g_ffn2)

    h_rows = jnp.concatenate([h_p, h_s], axis=0)
    idx = jnp.concatenate([idx_p[:, :TOP_K], idx_s[:, :TOP_K]], axis=0)
    row_token, pos, tile_expert, n_used = _route_tables(idx)
    y_sorted = _experts(h_rows, row_token, tile_expert, n_used, w_gu[0], w_dn[0], bg, bu, bdn)
    y_p = _combine(x1_p, wts_p, pos[:tp], y_sorted, g_fin2)
    y_s = _combine(x1_s, wts_s, pos[tp:], y_sorted, g_fin2)

    win_all = jnp.concatenate([state_nsa_win[0], win_s.reshape(bd, ds, 2, NSA_KV, HEAD_DIM)], axis=1)
    lead = lambda a, n, h: a.reshape(1, n, -1, 2, h, HEAD_DIM)
    return (y_p.reshape(b, s, d), y_s.reshape(bd, ds, d),
            lead(cmp_p, b, NSA_KV), lead(slc_p, b, NSA_KV), lead(sbkv_p, b, SB_HEADS),
            lead(win_p, b, NSA_KV)[:, :, -wbuf:] if s >= wbuf else
            jnp.pad(lead(win_p, b, NSA_KV), ((0, 0), (0, 0), (wbuf - s, 0), (0, 0), (0, 0), (0, 0))),
            lead(cmp_s, bd, NSA_KV), lead(slc_s, bd, NSA_KV), lead(sbkv_s, bd, SB_HEADS),
            win_all[None, :, ds:])
```

```python
import functools

import numpy as np
import jax
import jax.numpy as jnp
from jax import lax
from jax.experimental import pallas as pl
from jax.experimental.pallas import tpu as pltpu

D_MODEL = 1024
HEAD_DIM = 64
NSA_HEADS = 8
NSA_KV = 2
NSA_REP = NSA_HEADS // NSA_KV
SB_HEADS = 8
NSA_W = NSA_HEADS * HEAD_DIM
SB_W = SB_HEADS * HEAD_DIM
KV_W = NSA_KV * HEAD_DIM
CMP_BLK = 32
CMP_STRIDE = 16
SEL_BLK = 64
TOP_N = 16
WINDOW = 512
FORCE_BONUS = 1.0e4
ROPE_THETA = 500000.0
ROPE_DIMS = HEAD_DIM // 4
N_EXPERTS = 32
TOP_K = 4
D_FF = D_MODEL
SWIGLU_LIMIT = 7.0
SWIGLU_ALPHA = 1.702
RMS_EPS = 1e-5
PAGE_SIZE = 128
SCALE = HEAD_DIM ** -0.5

LANES = 128
VMEM_LIMIT_BYTES = 56 * 1024 * 1024

BF16 = jnp.bfloat16
F32 = jnp.float32

_OFF = dict(q=0, k_cmp=512, v_cmp=640, k_slc=768, v_slc=896, k_win=1024, v_win=1152, br=1280,
            sb_q=1304, sb_k=1816, sb_v=2328, gate_a=2840, gate_b=3352, end=3864)
_HEAD_PERM = [j + 4 * half for j in range(4) for half in range(2)]
_C_Q, _C_SBQ, _C_CMP, _C_SLC, _C_WIN, _C_SBKV, _C_GATE, _C_BR, _NP = 0, 512, 1024, 1280, 1536, 1792, 2816, 3840, 3968


def _cparams(sem, fuse_inputs=None, flags=None):
    return pltpu.CompilerParams(dimension_semantics=sem, vmem_limit_bytes=VMEM_LIMIT_BYTES,
                                allow_input_fusion=fuse_inputs, flags=flags)


def _prep_w_in(w):
    def head_perm(base):
        return [w[:, base + h * HEAD_DIM: base + (h + 1) * HEAD_DIM] for h in _HEAD_PERM]
    parts = [c * SCALE for c in head_perm(_OFF['q'])]
    parts.append(w[:, _OFF['sb_q']:_OFF['sb_k']] * SCALE)
    parts.append(w[:, _OFF['k_cmp']:_OFF['br']])
    parts.append(w[:, _OFF['sb_k']:_OFF['gate_a']])
    parts.extend(head_perm(_OFF['gate_a']))
    parts.append(w[:, _OFF['gate_b']:_OFF['end']])
    parts.append(w[:, _OFF['br']:_OFF['sb_q']])
    parts.append(jnp.zeros((w.shape[0], LANES - 3 * NSA_HEADS), w.dtype))
    return jnp.concatenate(parts, axis=1).astype(BF16)


def _rope_tables(pos):
    half = ROPE_DIMS // 2
    inv = ROPE_THETA ** (-jnp.arange(half, dtype=F32) / half)
    ang = pos.astype(F32)[:, None] * inv[None, :]
    cos = jnp.cos(ang)
    sin = jnp.sin(ang)
    n = pos.shape[0]
    one = jnp.ones((n, HEAD_DIM - ROPE_DIMS), F32)
    zero = jnp.zeros((n, HEAD_DIM - ROPE_DIMS), F32)
    z8 = jnp.zeros((n, half), F32)
    c = jnp.concatenate([cos, cos, one], axis=1)
    s1 = jnp.concatenate([-sin, z8, zero], axis=1)
    s2 = jnp.concatenate([z8, sin, zero], axis=1)
    tile2 = lambda t: jnp.concatenate([t, t], axis=1)
    return tile2(c), tile2(s1), tile2(s2)


def _rope_apply(x, c, s1, s2):
    outs = []
    for j in range(x.shape[1] // LANES):
        xb = x[:, j * LANES:(j + 1) * LANES]
        outs.append(xb * c + pltpu.roll(xb, LANES - 8, 1) * s1 + pltpu.roll(xb, 8, 1) * s2)
    return outs[0] if len(outs) == 1 else jnp.concatenate(outs, axis=1)


def _proj_kernel(x_ref, g_ref, w_ref, c_ref, s1_ref, s2_ref,
                 qc_ref, qr_ref, sbq_ref, cmp_ref, slc_ref, win_ref, sbkv_ref,
                 kslc_ref, vslc_ref, kwin_ref, vwin_ref, sbk_ref, sbv_ref, gate_ref, br_ref):
    x = x_ref[...]
    ms = jnp.mean(x * x, axis=-1, keepdims=True)
    h = (x * lax.rsqrt(ms + RMS_EPS)) * g_ref[...]
    hb = h.astype(BF16)
    c, s1, s2 = c_ref[...], s1_ref[...], s2_ref[...]

    def mm(c0, c1):
        return jnp.dot(hb, w_ref[:, c0:c1], preferred_element_type=F32)

    q = mm(_C_Q, _C_SBQ)
    qc_ref[...] = q.astype(BF16)
    qr_ref[...] = _rope_apply(q, c, s1, s2).astype(BF16)
    sbq_ref[...] = mm(_C_SBQ, _C_CMP).astype(BF16)
    cmp_ref[...] = mm(_C_CMP, _C_SLC)
    for c0, cache_ref, k_ref, v_ref in ((_C_SLC, slc_ref, kslc_ref, vslc_ref), (_C_WIN, win_ref, kwin_ref, vwin_ref)):
        z = mm(c0, c0 + 2 * KV_W)
        k = _rope_apply(z[:, :KV_W], c, s1, s2)
        v = z[:, KV_W:]
        cache_ref[:, :KV_W] = k
        cache_ref[:, KV_W:] = v
        k_ref[...] = k.astype(BF16)
        v_ref[...] = v.astype(BF16)
    z = mm(_C_SBKV, _C_GATE)
    sbkv_ref[...] = z
    sbk_ref[...] = z[:, :SB_W].astype(BF16)
    sbv_ref[...] = z[:, SB_W:].astype(BF16)
    gate_ref[...] = jax.nn.sigmoid(mm(_C_GATE, _C_BR)).astype(BF16)
    br_ref[...] = jax.nn.sigmoid(mm(_C_BR, _NP))


def _proj(x2d, g, w_perm, tabs, tm):
    t = x2d.shape[0]
    tab_blocks = tabs[0].shape[0] // tm
    assert t % tm == 0 and tabs[0].shape[0] % tm == 0
    row = lambda n: pl.BlockSpec((tm, n), lambda i: (i, 0))
    tab = pl.BlockSpec((tm, LANES), lambda i: (i % tab_blocks, 0))
    full = lambda a: pl.BlockSpec(a.shape, lambda i: (0,) * a.ndim)
    outs = [(512, BF16), (512, BF16), (512, BF16), (256, F32), (256, F32), (256, F32), (1024, F32),
            (128, BF16), (128, BF16), (128, BF16), (128, BF16), (512, BF16), (512, BF16), (1024, BF16), (128, F32)]
    return pl.pallas_call(
        _proj_kernel,
        grid=(t // tm,),
        in_specs=[row(D_MODEL), full(g), full(w_perm), tab, tab, tab],
        out_specs=[row(n) for n, _ in outs],
        out_shape=[jax.ShapeDtypeStruct((t, n), dt) for n, dt in outs],
        compiler_params=_cparams(("parallel",)),
        name="proj",
    )(x2d, g, w_perm, *tabs)


def _prep_w_cmp(w_ck, w_cv):
    eye2 = jnp.eye(NSA_KV, dtype=w_ck.dtype)
    bd = lambda w: jnp.einsum('ab,ide->iadbe', eye2, w).reshape(CMP_BLK, KV_W, KV_W)
    z = jnp.zeros((CMP_BLK, KV_W, KV_W), w_ck.dtype)
    w_all = jnp.concatenate([jnp.concatenate([bd(w_ck), z], axis=2),
                             jnp.concatenate([z, bd(w_cv)], axis=2)], axis=1)
    w_all = w_all.astype(BF16)
    return (w_all[:CMP_STRIDE].reshape(CMP_STRIDE * 2 * KV_W, 2 * KV_W),
            w_all[CMP_STRIDE:].reshape(CMP_STRIDE * 2 * KV_W, 2 * KV_W))


def _compress_rows(xb, w1, w2):
    n = xb.shape[0]
    y1 = jnp.dot(xb, w1, preferred_element_type=F32)
    y2 = jnp.dot(xb, w2, preferred_element_type=F32)
    y2s = pltpu.roll(y2, n - 1, 0)
    rid = lax.broadcasted_iota(jnp.int32, y2.shape, 0)
    return y1 + jnp.where(rid < n - 1, y2s, 0.0)


def _compress_kernel(x_ref, w1_ref, w2_ref, k_ref, v_ref):
    out = _compress_rows(x_ref[...].astype(BF16), w1_ref[...], w2_ref[...])
    k_ref[...] = out[:, :KV_W].astype(BF16)
    v_ref[...] = out[:, KV_W:].astype(BF16)


def _compress(cache2d, w1, w2, b, s):
    nc = s // CMP_STRIDE
    x = cache2d.reshape(b, nc, CMP_STRIDE * 2 * KV_W)
    full = lambda a: pl.BlockSpec(a.shape, lambda i: (0,) * a.ndim)
    blk = pl.BlockSpec((None, nc, KV_W), lambda i: (i, 0, 0))
    return pl.pallas_call(
        _compress_kernel,
        grid=(b,),
        in_specs=[pl.BlockSpec((None, nc, x.shape[2]), lambda i: (i, 0, 0)), full(w1), full(w2)],
        out_specs=[blk, blk],
        out_shape=[jax.ShapeDtypeStruct((b, nc, KV_W), BF16)] * 2,
        compiler_params=_cparams(("parallel",)),
        name="compress",
    )(x, w1, w2)


_NT = (((1,), (1,)), ((), ()))


def _dot_nt(a, b):
    return lax.dot_general(a, b, _NT, preferred_element_type=F32)


def _dot3(x, m):
    hi = x.astype(BF16)
    r1 = x - hi.astype(F32)
    mid = r1.astype(BF16)
    lo = (r1 - mid.astype(F32)).astype(BF16)
    d = lambda a: jnp.dot(a, m, preferred_element_type=F32)
    return d(hi) + d(mid) + d(lo)


def _stack_group(ref, g, tq):
    half = lax.broadcasted_iota(jnp.int32, (tq, LANES), 1) >> 6
    return jnp.concatenate([jnp.where(half == g, ref[:, j * LANES:(j + 1) * LANES], 0)
                            for j in range(NSA_REP)], axis=0).astype(BF16)


def _unstack_pairs(o0, o1, tq):
    half = lax.broadcasted_iota(jnp.int32, (tq, LANES), 1) >> 6
    return jnp.concatenate([jnp.where(half == 0, o0[j * tq:(j + 1) * tq], o1[j * tq:(j + 1) * tq])
                            for j in range(NSA_REP)], axis=1)


def _imp_matrix(nc, nsel=None):
    m = np.zeros((nc, LANES), np.float32)
    for j in range(nc * CMP_STRIDE // SEL_BLK if nsel is None else nsel):
        for c, wgt in ((4 * j - 1, 0.5), (4 * j, 1.0), (4 * j + 1, 1.0), (4 * j + 2, 1.0), (4 * j + 3, 0.5)):
            if 0 <= c < nc:
                m[c, j] += wgt
    return jnp.asarray(m, BF16)


def _select_blocks(imp_sel, cur, nsel):
    tq = imp_sel.shape[0]
    blk = lax.broadcasted_iota(jnp.int32, (tq, LANES), 1)
    forced = (blk == 0) | (blk == cur) | (blk == cur - 1)
    score = jnp.where(blk <= cur, imp_sel + jnp.where(forced, FORCE_BONUS, 0.0), -1e9)
    st = score.T
    nslab = -(-nsel // 8)
    slabs = [st[8 * v:8 * v + 8] for v in range(nslab)]
    sub = lax.broadcasted_iota(jnp.int32, (8, tq), 0)
    cnts = [jnp.zeros((8, tq), jnp.int32) for _ in range(nslab)]
    for k in range(nsel):
        sk = st[k:k + 1, :]
        for v in range(nslab):
            if 8 * v + 7 < k:
                inc = jnp.where(sk > slabs[v], 1, 0)
            elif 8 * v > k:
                inc = jnp.where(sk >= slabs[v], 1, 0)
            else:
                inc = jnp.where(sub + 8 * v > k, jnp.where(sk >= slabs[v], 1, 0), jnp.where(sk > slabs[v], 1, 0))
            cnts[v] = cnts[v] + inc
    parts = [jnp.where((c < TOP_N) & (sub + 8 * v < nsel), 1.0, 0.0) for v, c in enumerate(cnts)]
    if 8 * nslab < LANES:
        parts.append(jnp.zeros((LANES - 8 * nslab, tq), F32))
    return jnp.concatenate(parts, axis=0).T.astype(BF16)


def _nsa_kernel(qc_ref, qr_ref, kc_ref, vc_ref, ks_ref, vs_ref, m_ref, ocmp_ref, oslc_ref, *, tq, kt):
    q0 = pl.program_id(1) * tq
    nc = kc_ref.shape[0]
    nsel = nc * CMP_STRIDE // SEL_BLK
    qpos = q0 + lax.broadcasted_iota(jnp.int32, (tq, 1), 0)
    tile4 = lambda a: jnp.concatenate([a] * NSA_REP, axis=0)
    cidx = lax.broadcasted_iota(jnp.int32, (tq, nc), 1)
    cbias = tile4(jnp.where(cidx * CMP_STRIDE + (CMP_BLK - 1) <= qpos, 0.0, -1e30))
    row_ok = tile4(jnp.where(qpos >= CMP_BLK - 1, 1.0, 0.0))
    kcol = lax.broadcasted_iota(jnp.int32, (tq, kt), 1)
    eblk = (lax.broadcasted_iota(jnp.int32, (LANES, kt), 0)
            - (lax.broadcasted_iota(jnp.int32, (LANES, kt), 1) >> 6))
    n_full = q0 // kt
    o_c, o_s = [], []
    for g in range(NSA_KV):
        qcg = _stack_group(qc_ref, g, tq)
        qrg = _stack_group(qr_ref, g, tq)
        s = _dot_nt(qcg, kc_ref[...]) + cbias
        e = jnp.exp(s - jnp.max(s, axis=-1, keepdims=True))
        p = e * (row_ok / jnp.maximum(jnp.sum(e, axis=-1, keepdims=True), 1e-30))
        o_c.append(jnp.dot(p.astype(BF16), vc_ref[...], preferred_element_type=F32))
        imp = p[0:tq] + p[tq:2 * tq] + p[2 * tq:3 * tq] + p[3 * tq:4 * tq]
        sel = _select_blocks(_dot3(imp, m_ref[...]), qpos >> 6, nsel)

        def tile_step(t, carry, causal):
            m, l, acc = carry
            k0 = pl.multiple_of(t * kt, kt)
            sc = _dot_nt(qrg, ks_ref[pl.ds(k0, kt), :])
            expand = jnp.where(eblk == t * (kt // SEL_BLK), 1.0, 0.0).astype(BF16)
            bias = (jnp.dot(sel, expand, preferred_element_type=F32) - 1.0) * 1e30
            if causal:
                bias = jnp.where(k0 + kcol <= qpos, bias, -1e30)
            sc = sc + tile4(bias)
            m_new = jnp.maximum(m, jnp.max(sc, axis=-1, keepdims=True))
            a = jnp.exp(sc - m_new)
            alpha = jnp.exp(m - m_new)
            l = alpha * l + jnp.sum(a, axis=-1, keepdims=True)
            acc = alpha * acc + jnp.dot(a.astype(BF16), vs_ref[pl.ds(k0, kt), :], preferred_element_type=F32)
            return m_new, l, acc

        init = (jnp.full((NSA_REP * tq, 1), -1e30, F32), jnp.zeros((NSA_REP * tq, 1), F32),
                jnp.zeros((NSA_REP * tq, LANES), F32))
        carry = lax.fori_loop(0, n_full, lambda t, c: tile_step(t, c, False), init)
        _, l, acc = tile_step(n_full, carry, True)
        o_s.append(acc / jnp.maximum(l, 1e-30))
    ocmp_ref[...] = _unstack_pairs(o_c[0], o_c[1], tq)
    oslc_ref[...] = _unstack_pairs(o_s[0], o_s[1], tq)


def _nsa(qc, qr, kc, vc, ks, vs, b, s, tq=128, kt=512):
    nqb = s // tq
    kt = min(kt, s)
    m = _imp_matrix(s // CMP_STRIDE)
    qspec = pl.BlockSpec((tq, NSA_W), lambda bi, i: (bi * nqb + i, 0))
    cspec = pl.BlockSpec((None, s // CMP_STRIDE, KV_W), lambda bi, i: (bi, 0, 0))
    kspec = pl.BlockSpec((s, KV_W), lambda bi, i: (bi, 0))
    return pl.pallas_call(
        functools.partial(_nsa_kernel, tq=tq, kt=kt),
        grid=(b, nqb),
        in_specs=[qspec, qspec, cspec, cspec, kspec, kspec, pl.BlockSpec(m.shape, lambda bi, i: (0, 0))],
        out_specs=[qspec, qspec],
        out_shape=[jax.ShapeDtypeStruct((b * s, NSA_W), F32)] * 2,
        compiler_params=_cparams(("parallel", "parallel")),
        name="nsa",
    )(qc, qr, kc, vc, ks, vs, m)


def _window_kernel(qr_ref, k_ref, v_ref, o_ref, *, tq, span):
    q0 = pl.program_id(1) * tq
    s_len = k_ref.shape[0]
    start = pl.multiple_of(jnp.minimum(jnp.maximum(q0 - WINDOW, 0), s_len - span), tq)
    rpos = q0 + (lax.broadcasted_iota(jnp.int32, (NSA_REP * tq, 1), 0) & (tq - 1))
    k = k_ref[pl.ds(start, span), :]
    v = v_ref[pl.ds(start, span), :]
    outs = []
    for g in range(NSA_KV):
        qrg = _stack_group(qr_ref, g, tq)
        s = _dot_nt(qrg, k)
        diff = rpos - (start + lax.broadcasted_iota(jnp.int32, s.shape, 1))
        mask = (diff >= 0) & (diff < WINDOW)
        s = jnp.where(mask, s, -1e30)
        e = jnp.where(mask, jnp.exp(s - jnp.max(s, axis=-1, keepdims=True)), 0.0)
        p = e / jnp.maximum(jnp.sum(e, axis=-1, keepdims=True), 1e-30)
        outs.append(jnp.dot(p.astype(BF16), v, preferred_element_type=F32))
    o_ref[...] = _unstack_pairs(outs[0], outs[1], tq)


def _window(qr, kw, vw, b, s, tq=128):
    nqb = s // tq
    span = min(WINDOW + tq, s)
    qspec = pl.BlockSpec((tq, NSA_W), lambda bi, i: (bi * nqb + i, 0))
    kspec = pl.BlockSpec((s, KV_W), lambda bi, i: (bi, 0))
    return pl.pallas_call(
        functools.partial(_window_kernel, tq=tq, span=span),
        grid=(b, nqb),
        in_specs=[qspec, kspec, kspec],
        out_specs=qspec,
        out_shape=jax.ShapeDtypeStruct((b * s, NSA_W), F32),
        compiler_params=_cparams(("parallel", "parallel")),
        name="window",
    )(qr, kw, vw)


def _softplus(z):
    return jnp.maximum(z, 0.0) + jnp.log(1.0 + jnp.exp(-jnp.abs(z)))


def _tri_matrix(n):
    return jnp.asarray(np.tril(np.ones((n, n), np.float32)), BF16)


def _sb_block(qh, k_t, v_t, tri, carry, acc, valid):
    z = _dot_nt(qh, k_t)
    sp = _softplus(z)
    if valid is not None:
        sp = jnp.where(valid, sp, 0.0)
    cum = jnp.dot(sp.astype(BF16), tri, preferred_element_type=F32) + carry
    a = jnp.exp(z - cum)
    if valid is not None:
        a = jnp.where(valid, a, 0.0)
    acc = acc + jnp.dot(a.astype(BF16), v_t, preferred_element_type=F32)
    return cum[:, 0:1], acc


def _sb_two_blocks(qh, k_ref, v_ref, tri, ja, jb, state, mask_a, ok_b, tq):
    a0 = pl.multiple_of(ja * tq, tq)
    b0 = pl.multiple_of(jb * tq, tq)
    k_a, v_a = k_ref[pl.ds(a0, tq), :], v_ref[pl.ds(a0, tq), :]
    k_b, v_b = k_ref[pl.ds(b0, tq), :], v_ref[pl.ds(b0, tq), :]
    out = []
    for hh in range(2):
        carry, acc = state[2 * hh], state[2 * hh + 1]
        za = _dot_nt(qh[hh], k_a)
        zb = _dot_nt(qh[hh], k_b)
        spa = _softplus(za)
        spb = _softplus(zb)
        if mask_a is not None:
            spa = jnp.where(mask_a, spa, 0.0)
        if ok_b is not None:
            spb = jnp.where(ok_b, spb, 0.0)
        cuma = jnp.dot(spa.astype(BF16), tri, preferred_element_type=F32) + carry
        cumb = jnp.dot(spb.astype(BF16), tri, preferred_element_type=F32) + cuma[:, 0:1]
        aa = jnp.exp(za - cuma)
        ab = jnp.exp(zb - cumb)
        if mask_a is not None:
            aa = jnp.where(mask_a, aa, 0.0)
        if ok_b is not None:
            ab = jnp.where(ok_b, ab, 0.0)
        acc = (acc + jnp.dot(aa.astype(BF16), v_a, preferred_element_type=F32)
               + jnp.dot(ab.astype(BF16), v_b, preferred_element_type=F32))
        out.extend((cumb[:, 0:1], acc))
    return tuple(out)


def _sb_kernel(q_ref, k_ref, v_ref, tri_ref, o_ref, *, tq):
    i = pl.program_id(2)
    half = lax.broadcasted_iota(jnp.int32, (tq, LANES), 1) >> 6
    q = q_ref[...]
    qh = [jnp.where(half == hh, q, 0) for hh in range(2)]
    tri = tri_ref[...]
    odd = (i & 1) == 1
    causal = lax.broadcasted_iota(jnp.int32, (tq, tq), 1) < lax.broadcasted_iota(jnp.int32, (tq, tq), 0)
    zero = (jnp.zeros((tq, 1), F32), jnp.zeros((tq, LANES), F32))
    state = _sb_two_blocks(qh, k_ref, v_ref, tri, i, jnp.maximum(i - 1, 0), zero + zero, causal, odd, tq)
    first = i - 1 - (i & 1)

    def body(p, st):
        ja = first - 2 * p
        return _sb_two_blocks(qh, k_ref, v_ref, tri, ja, ja - 1, st, None, None, tq)

    st = lax.fori_loop(0, i >> 1, body, state)
    o_ref[...] = jnp.where(half == 0, st[1], st[3])


def _sb(sbq, sbk, sbv, b, s, tq=256):
    tq = min(tq, s)
    nqb = s // tq
    tri = _tri_matrix(tq)
    npair = SB_HEADS // 2
    qspec = pl.BlockSpec((tq, LANES), lambda bi, p, i: (bi * nqb + i, p))
    kspec = pl.BlockSpec((s, LANES), lambda bi, p, i: (bi, p))
    return pl.pallas_call(
        functools.partial(_sb_kernel, tq=tq),
        grid=(b, npair, nqb),
        in_specs=[qspec, kspec, kspec, pl.BlockSpec(tri.shape, lambda bi, p, i: (0, 0))],
        out_specs=qspec,
        out_shape=jax.ShapeDtypeStruct((b * s, SB_W), F32),
        compiler_params=_cparams(("parallel", "parallel", "parallel")),
        name="sb",
    )(sbq, sbk, sbv, tri)


def _prep_merge(w_out, w_router, b_router):
    w_a = [w_out[h * HEAD_DIM:(h + 1) * HEAD_DIM] for h in _HEAD_PERM]
    w_out_p = jnp.concatenate(w_a + [w_out[NSA_W:]], axis=0).astype(BF16)
    e = np.zeros((LANES, 3 * NSA_W), np.float32)
    for br in range(3):
        for p, h in enumerate(_HEAD_PERM):
            e[br * NSA_HEADS + h, br * NSA_W + p * HEAD_DIM: br * NSA_W + (p + 1) * HEAD_DIM] = 1.0
    w_r = jnp.concatenate([w_router, jnp.zeros((D_MODEL, LANES - N_EXPERTS), w_router.dtype)], axis=1)
    b_r = jnp.concatenate([b_router, jnp.full((LANES - N_EXPERTS,), -1e30, b_router.dtype)]).reshape(1, LANES)
    return w_out_p, jnp.asarray(e, BF16), w_r, b_r


def _merge_kernel(x_ref, oc_ref, os_ref, ow_ref, ob_ref, gate_ref, br_ref, e_ref, wo_ref, g_ref, wr_ref, brt_ref,
                  x1_ref, hp_ref, idx_ref, wts_ref):
    gexp = _dot3(br_ref[...], e_ref[...])
    o_a = gexp[:, :NSA_W] * oc_ref[...] + gexp[:, NSA_W:2 * NSA_W] * os_ref[...] + gexp[:, 2 * NSA_W:] * ow_ref[...]
    gate = gate_ref[...].astype(F32)
    cat = jnp.concatenate([gate[:, :NSA_W] * o_a, gate[:, NSA_W:] * ob_ref[...]], axis=1).astype(BF16)
    x1 = x_ref[...] + jnp.dot(cat, wo_ref[...], preferred_element_type=F32)
    x1_ref[...] = x1
    ms = jnp.mean(x1 * x1, axis=-1, keepdims=True)
    h = (x1 * lax.rsqrt(ms + RMS_EPS)) * g_ref[...]
    hb = h.astype(BF16).astype(F32)
    for j in range(D_MODEL // LANES):
        hp_ref[:, j, :] = hb[:, j * LANES:(j + 1) * LANES]
    lg = jnp.dot(h, wr_ref[...], preferred_element_type=F32, precision=lax.Precision.HIGHEST) + brt_ref[...]
    lane = lax.broadcasted_iota(jnp.int32, lg.shape, 1)
    vals, ids = [], []
    for _ in range(TOP_K):
        mx = jnp.max(lg, axis=-1, keepdims=True)
        first = jnp.min(jnp.where(lg == mx, lane, LANES), axis=-1, keepdims=True)
        vals.append(mx)
        ids.append(first)
        lg = jnp.where(lane == first, -jnp.inf, lg)
    es = [jnp.exp(v - vals[0]) for v in vals]
    tot = es[0] + es[1] + es[2] + es[3]
    idx = jnp.zeros(lg.shape, jnp.int32)
    wts = jnp.zeros(lg.shape, F32)
    for k in range(TOP_K):
        idx = jnp.where(lane == k, ids[k], idx)
        wts = jnp.where(lane == k, es[k] / tot, wts)
    idx_ref[...] = idx
    wts_ref[...] = wts


def _merge(x2d, oc, os_, ow, ob, gate, br, prep, g_ffn, tm=256):
    t = x2d.shape[0]
    tm = min(tm, t)
    assert t % tm == 0
    w_out_p, e, w_r, b_r = prep
    row = lambda n: pl.BlockSpec((tm, n), lambda i: (i, 0))
    full = lambda a: pl.BlockSpec(a.shape, lambda i: (0,) * a.ndim)
    return pl.pallas_call(
        _merge_kernel,
        grid=(t // tm,),
        in_specs=[row(D_MODEL), row(NSA_W), row(NSA_W), row(NSA_W), row(SB_W), row(NSA_W + SB_W), row(LANES),
                  full(e), full(w_out_p), full(g_ffn), full(w_r), full(b_r)],
        out_specs=[row(D_MODEL), pl.BlockSpec((tm, ROW_SUB, LANES), lambda i: (i, 0, 0)), row(LANES), row(LANES)],
        out_shape=[jax.ShapeDtypeStruct((t, D_MODEL), F32), jax.ShapeDtypeStruct((t, ROW_SUB, LANES), F32),
                   jax.ShapeDtypeStruct((t, LANES), jnp.int32), jax.ShapeDtypeStruct((t, LANES), F32)],
        compiler_params=_cparams(("parallel",)),
        name="merge",
    )(x2d, oc, os_, ow, ob, gate, br, e, w_out_p, g_ffn, w_r, b_r)


MOE_TILE = 256
COMBINE_ROWS = 128
ROW_SUB = D_MODEL // LANES


def _route_tables(idx):
    t = idx.shape[0]
    a = t * TOP_K
    assert a % MOE_TILE == 0
    n_tiles = a // MOE_TILE + N_EXPERTS
    e = idx.reshape(a)
    onehot = (e[:, None] == jnp.arange(N_EXPERTS, dtype=jnp.int32)[None, :]).astype(jnp.int32)
    csum = jnp.cumsum(onehot, axis=0)
    rank = jnp.take_along_axis(csum, e[:, None], axis=1)[:, 0] - 1
    counts = csum[-1]
    tiles_e = (counts + MOE_TILE - 1) // MOE_TILE
    tile_end = jnp.cumsum(tiles_e)
    tile_start = tile_end - tiles_e
    dest = tile_start[e] * MOE_TILE + rank
    row_token = jnp.zeros((n_tiles * MOE_TILE,), jnp.int32).at[dest].set(jnp.arange(a, dtype=jnp.int32) // TOP_K)
    tile_ids = jnp.arange(n_tiles, dtype=jnp.int32)
    tile_expert = jnp.minimum(jnp.sum((tile_end[None, :] <= tile_ids[:, None]).astype(jnp.int32), axis=1),
                              N_EXPERTS - 1).astype(jnp.int32)
    return (row_token.reshape(n_tiles, 1, MOE_TILE), dest.reshape(t, TOP_K), tile_expert,
            tile_end[-1:].astype(jnp.int32))


def _deinterleave_matrices():
    e = np.zeros((2 * LANES, LANES), np.float32)
    o = np.zeros((2 * LANES, LANES), np.float32)
    e[2 * np.arange(LANES), np.arange(LANES)] = 1.0
    o[2 * np.arange(LANES) + 1, np.arange(LANES)] = 1.0
    return jnp.asarray(e, BF16), jnp.asarray(o, BF16)


def _row_tiles_to_2d(load):
    return jnp.concatenate([load(j) for j in range(ROW_SUB)], axis=1)


def _expert_kernel(te_ref, nu_ref, rt_ref, rtn_ref, h_ref, wgu_ref, wdn_ref, bg_ref, bu_ref, bd_ref, se_ref, so_ref,
                   y_ref, xbuf, sem, wg_s, wu_s, wd_s):
    i = pl.program_id(0)
    n_used = nu_ref[0]
    slot = i % 2

    def copy(idx_ref, r, s):
        return pltpu.make_async_copy(h_ref.at[pl.ds(idx_ref[0, r], 1)], xbuf.at[s, pl.ds(r, 1)], sem.at[s])

    def issue(idx_ref, s):
        def body(r, c):
            copy(idx_ref, r, s).start()
            return c
        lax.fori_loop(0, MOE_TILE, body, 0, unroll=8)

    def wait_rows():
        def wait(r, c):
            copy(rt_ref, r, slot).wait()
            return c
        lax.fori_loop(0, MOE_TILE, wait, 0, unroll=8)

    @pl.when(i == 0)
    def _():
        issue(rt_ref, 0)

    @pl.when(i + 1 < n_used)
    def _():
        issue(rtn_ref, 1 - slot)

    @pl.when(i < n_used)
    def _():
        @pl.when((i == 0) | (te_ref[i] != te_ref[jnp.maximum(i - 1, 0)]))
        def _():
            for k in range(D_FF // LANES):
                wblk = wgu_ref[:, 2 * LANES * k:2 * LANES * (k + 1)].astype(BF16)
                wg_s[:, LANES * k:LANES * (k + 1)] = jnp.dot(wblk, se_ref[...], preferred_element_type=F32).astype(BF16)
                wu_s[:, LANES * k:LANES * (k + 1)] = jnp.dot(wblk, so_ref[...], preferred_element_type=F32).astype(BF16)
            wd_s[...] = wdn_ref[...].astype(BF16)

        wait_rows()
        x = _row_tiles_to_2d(lambda j: xbuf[slot, :, j, :]).astype(BF16)
        g = jnp.minimum(jnp.dot(x, wg_s[...], preferred_element_type=F32) + bg_ref[...], SWIGLU_LIMIT)
        u = jnp.clip(jnp.dot(x, wu_s[...], preferred_element_type=F32) + bu_ref[...], -SWIGLU_LIMIT, SWIGLU_LIMIT)
        act = (u + 1.0) * g * jax.nn.sigmoid(SWIGLU_ALPHA * g)
        y = jnp.dot(act.astype(BF16), wd_s[...], preferred_element_type=F32) + bd_ref[...]
        for j in range(ROW_SUB):
            y_ref[:, j, :] = y[:, j * LANES:(j + 1) * LANES]

    @pl.when(i >= n_used)
    def _():
        y_ref[...] = jnp.zeros(y_ref.shape, F32)


def _experts(h_rows, row_token, tile_expert, n_used, w_gu, w_dn, bg, bu, bd):
    n_tiles = row_token.shape[0]
    se, so = _deinterleave_matrices()
    wspec = lambda shape: pl.BlockSpec((None,) + shape, lambda i, te, nu: (te[i], 0, 0))
    full = lambda a: pl.BlockSpec(a.shape, lambda i, te, nu: (0,) * a.ndim)
    rt_spec = lambda off: pl.BlockSpec((None, 1, MOE_TILE), lambda i, te, nu: (jnp.minimum(i + off, n_tiles - 1), 0, 0),
                                       memory_space=pltpu.SMEM)
    return pl.pallas_call(
        _expert_kernel,
        grid_spec=pltpu.PrefetchScalarGridSpec(
            num_scalar_prefetch=2,
            grid=(n_tiles,),
            in_specs=[rt_spec(0), rt_spec(1), pl.BlockSpec(memory_space=pl.ANY),
                      wspec((D_MODEL, 2 * D_FF)), wspec((D_FF, D_MODEL)), wspec((1, D_FF)), wspec((1, D_FF)),
                      wspec((1, D_MODEL)), full(se), full(so)],
            out_specs=pl.BlockSpec((MOE_TILE, ROW_SUB, LANES), lambda i, te, nu: (i, 0, 0)),
            scratch_shapes=[pltpu.VMEM((2, MOE_TILE, ROW_SUB, LANES), F32), pltpu.SemaphoreType.DMA((2,)),
                            pltpu.VMEM((D_MODEL, D_FF), BF16), pltpu.VMEM((D_MODEL, D_FF), BF16),
                            pltpu.VMEM((D_FF, D_MODEL), BF16)],
        ),
        out_shape=jax.ShapeDtypeStruct((n_tiles * MOE_TILE, ROW_SUB, LANES), F32),
        compiler_params=_cparams(("arbitrary",)),
        name="moe_experts",
    )(tile_expert, n_used, row_token, row_token, h_rows, w_gu, w_dn, bg, bu, bd, se, so)


def _combine_kernel(pos_ref, posn_ref, x_ref, wts_ref, g_ref, y_ref, out_ref, buf, sem, *, n):
    i = pl.program_id(0)
    n_rows = TOP_K * COMBINE_ROWS

    def copy(p_ref, r, slot):
        return pltpu.make_async_copy(y_ref.at[pl.ds(p_ref[0, r], 1)],
                                     buf.at[slot, r & (TOP_K - 1), pl.ds(r >> 2, 1)], sem.at[slot])

    def issue(p_ref, slot):
        def body(r2, c):
            copy(p_ref, 2 * r2, slot).start(priority=0)
            copy(p_ref, 2 * r2 + 1, slot).start(priority=1)
            return c
        lax.fori_loop(0, n_rows // 2, body, 0, unroll=4)

    slot = i % 2

    @pl.when(i == 0)
    def _():
        issue(pos_ref, 0)

    @pl.when(i + 1 < n)
    def _():
        issue(posn_ref, 1 - slot)

    def wait(r, c):
        copy(pos_ref, r, slot).wait()
        return c
    lax.fori_loop(0, n_rows, wait, 0, unroll=8)

    w = wts_ref[...]
    parts = []
    ssq = jnp.zeros((COMBINE_ROWS, 1), F32)
    for j in range(ROW_SUB):
        xs = x_ref[:, j * LANES:(j + 1) * LANES]
        for k in range(TOP_K):
            xs = xs + w[:, k:k + 1] * buf[slot, k, :, j, :]
        parts.append(xs)
        ssq = ssq + jnp.sum(xs * xs, axis=-1, keepdims=True)
    rs = lax.rsqrt(ssq * (1.0 / D_MODEL) + RMS_EPS)
    for j in range(ROW_SUB):
        out_ref[:, j * LANES:(j + 1) * LANES] = (parts[j] * rs) * g_ref[:, j * LANES:(j + 1) * LANES]


def _combine(x1, wts, pos, y_sorted, g_final):
    t = x1.shape[0]
    assert t % COMBINE_ROWS == 0
    steps = t // COMBINE_ROWS
    pos2 = pos.reshape(steps, 1, TOP_K * COMBINE_ROWS)
    row = lambda nn: pl.BlockSpec((COMBINE_ROWS, nn), lambda i: (i, 0))
    return pl.pallas_call(
        functools.partial(_combine_kernel, n=steps),
        grid=(steps,),
        in_specs=[pl.BlockSpec((None, 1, TOP_K * COMBINE_ROWS), lambda i: (i, 0, 0), memory_space=pltpu.SMEM),
                  pl.BlockSpec((None, 1, TOP_K * COMBINE_ROWS), lambda i: (jnp.minimum(i + 1, steps - 1), 0, 0),
                               memory_space=pltpu.SMEM),
                  row(D_MODEL), row(LANES), pl.BlockSpec(g_final.shape, lambda i: (0, 0)),
                  pl.BlockSpec(memory_space=pl.ANY)],
        out_specs=row(D_MODEL),
        out_shape=jax.ShapeDtypeStruct((t, D_MODEL), F32),
        scratch_shapes=[pltpu.VMEM((2, TOP_K, COMBINE_ROWS, ROW_SUB, LANES), F32), pltpu.SemaphoreType.DMA((2,))],
        compiler_params=_cparams(("arbitrary",)),
        name="moe_combine",
    )(pos2, pos2, x1, wts, g_final, y_sorted)


QPAD = 8


def _select_blocks_rows(imp_sel, cur, nsel):
    tq = imp_sel.shape[0]
    blk = lax.broadcasted_iota(jnp.int32, (tq, LANES), 1)
    forced = (blk == 0) | (blk == cur) | (blk == cur - 1)
    score = jnp.where(blk <= cur, imp_sel + jnp.where(forced, FORCE_BONUS, 0.0), -1e9)
    cnt = jnp.zeros((tq, LANES), jnp.int32)
    for k in range(nsel):
        sk = score[:, k:k + 1]
        beats = (sk > score) | ((sk == score) & (blk > k))
        cnt = cnt + jnp.where(beats, 1, 0)
    return jnp.where((cnt < TOP_N) & (blk < nsel), 1.0, 0.0)


def _pad_rows(x, n):
    return jnp.concatenate([x, jnp.zeros((n - x.shape[0], x.shape[1]), x.dtype)], axis=0)


NSA_S_SEQS = 4
SB_S_SEQS = 2


def _nsa_s_kernel(pt_ref, qc_ref, qr_ref, xnew_ref, snew_ref, w1_ref, w2_ref, m_ref, *rest, n_pages, past, dec, seqs):
    ocmp_ref, oslc_ref = rest[2 * seqs * n_pages:]
    pages = [rest[2 * n_pages * s:2 * n_pages * (s + 1)] for s in range(seqs)]
    nc = n_pages * (PAGE_SIZE // CMP_STRIDE)
    xb = jnp.concatenate([p[...] for s in range(seqs) for p in pages[s][:n_pages]], axis=0).astype(BF16)
    y1 = jnp.dot(xb, w1_ref[...], preferred_element_type=F32)
    y2 = jnp.dot(xb, w2_ref[...], preferred_element_type=F32)
    xn = jnp.concatenate([xnew_ref[s] for s in range(seqs)] + [xnew_ref[0]] * (8 - seqs), axis=0)
    ynew = jnp.dot(xn.astype(BF16), w2_ref[...], preferred_element_type=F32)
    rid = lax.broadcasted_iota(jnp.int32, (nc, y2.shape[1]), 0)
    for s in range(seqs):
        y2s = pltpu.roll(y2[s * nc:(s + 1) * nc], nc - 1, 0)
        kcv = y1[s * nc:(s + 1) * nc] + jnp.where(rid < nc - 1, y2s, ynew[s:s + 1])
        _nsa_s_one(qc_ref.at[s], qr_ref.at[s], snew_ref.at[s], m_ref, kcv[:, :KV_W].astype(BF16),
                   kcv[:, KV_W:].astype(BF16), pages[s][n_pages:], ocmp_ref.at[s], oslc_ref.at[s], n_pages, past, dec)


def _nsa_s_one(qc_ref, qr_ref, snew_ref, m_ref, kc, vc, slc_pages, ocmp_ref, oslc_ref, n_pages, past, dec):
    tq = QPAD
    nsel = past // SEL_BLK + 1
    trow = lax.broadcasted_iota(jnp.int32, (NSA_REP * tq, 1), 0) & (tq - 1)
    rpos = past + jnp.minimum(trow, dec - 1)
    qpos = past + jnp.minimum(lax.broadcasted_iota(jnp.int32, (tq, 1), 0), dec - 1)
    lane = lax.broadcasted_iota(jnp.int32, (tq, LANES), 1)
    snew = snew_ref[...]
    k_all = jnp.concatenate([pg[:, :KV_W].astype(BF16) for pg in slc_pages]
                            + [_pad_rows(snew[:, :KV_W], LANES).astype(BF16)], axis=0)
    v_all = jnp.concatenate([pg[:, KV_W:].astype(BF16) for pg in slc_pages]
                            + [_pad_rows(snew[:, KV_W:], LANES).astype(BF16)], axis=0)
    o_c, o_s = [], []
    for g in range(NSA_KV):
        qcg = _stack_group(qc_ref, g, tq)
        qrg = _stack_group(qr_ref, g, tq)
        s = _dot_nt(qcg, kc)
        cidx = lax.broadcasted_iota(jnp.int32, s.shape, 1)
        cmask = cidx * CMP_STRIDE + (CMP_BLK - 1) <= rpos
        s = jnp.where(cmask, s, -1e30)
        e = jnp.where(cmask, jnp.exp(s - jnp.max(s, axis=-1, keepdims=True)), 0.0)
        p = e / jnp.maximum(jnp.sum(e, axis=-1, keepdims=True), 1e-30)
        o_c.append(jnp.dot(p.astype(BF16), vc, preferred_element_type=F32))
        imp = p[0:tq] + p[tq:2 * tq] + p[2 * tq:3 * tq] + p[3 * tq:4 * tq]
        sel = _select_blocks_rows(_dot3(imp, m_ref[...]), qpos >> 6, nsel)
        masks =[jnp.where(lane < SEL_BLK, sel[:, 2 * j:2 * j + 1], sel[:, 2 * j + 1:2 * j + 2]) > 0.5
                 for j in range(n_pages)]
        masks.append((sel[:, nsel - 1:nsel] > 0.5) & (lane < dec) & (past + lane <= qpos))
        valid = jnp.concatenate([jnp.concatenate(masks, axis=1)] * NSA_REP, axis=0)
        sc = jnp.where(valid, _dot_nt(qrg, k_all), -1e30)
        a = jnp.where(valid, jnp.exp(sc - jnp.max(sc, axis=-1, keepdims=True)), 0.0)
        l = jnp.sum(a, axis=-1, keepdims=True)
        o_s.append(jnp.dot(a.astype(BF16), v_all, preferred_element_type=F32) / jnp.maximum(l, 1e-30))
    ocmp_ref[...] = _unstack_pairs(o_c[0], o_c[1], tq)
    oslc_ref[...] = _unstack_pairs(o_s[0], o_s[1], tq)


def _nsa_s(page_table, qc, qr, xnew, snew, w1, w2, cache_cmp, cache_slc, past, dec):
    bd, n_pages = page_table.shape
    seqs = NSA_S_SEQS if bd % NSA_S_SEQS == 0 else 1
    cpp = PAGE_SIZE // CMP_STRIDE
    cmp_pages = cache_cmp.reshape(cache_cmp.shape[0], cpp, CMP_STRIDE * 2 * KV_W)
    m = _imp_matrix(n_pages * cpp, past // SEL_BLK + 1)
    per_b = lambda a: pl.BlockSpec((seqs,) + a.shape[1:], lambda b, pt: (b,) + (0,) * (a.ndim - 1))
    full = lambda a: pl.BlockSpec(a.shape, lambda b, pt: (0,) * a.ndim)
    page = lambda a, s, j: pl.BlockSpec((None,) + a.shape[1:], lambda b, pt: (pt[b * seqs + s, j], 0, 0))
    ospec = pl.BlockSpec((seqs, QPAD, NSA_W), lambda b, pt: (b, 0, 0))
    page_specs, page_args = [], []
    for s in range(seqs):
        page_specs += [page(cmp_pages, s, j) for j in range(n_pages)] + [page(cache_slc, s, j) for j in range(n_pages)]
        page_args += [cmp_pages] * n_pages + [cache_slc] * n_pages
    return pl.pallas_call(
        functools.partial(_nsa_s_kernel, n_pages=n_pages, past=past, dec=dec, seqs=seqs),
        grid_spec=pltpu.PrefetchScalarGridSpec(
            num_scalar_prefetch=1,
            grid=(bd // seqs,),
            in_specs=[per_b(qc), per_b(qr), per_b(xnew), per_b(snew), full(w1), full(w2), full(m)] + page_specs,
            out_specs=[ospec, ospec],
        ),
        out_shape=[jax.ShapeDtypeStruct((bd, QPAD, NSA_W), F32)] * 2,
        compiler_params=_cparams(("arbitrary",), fuse_inputs=[False] * 8 + [True] * len(page_args)),
        name="nsa_sample",
    )(page_table, qc, qr, xnew, snew, w1, w2, m, *page_args)


def _window_s_kernel(qr_ref, st_ref, wnew_ref, o_ref, *, past, dec):
    tq = QPAD
    wbuf = st_ref.shape[0]
    trow = jnp.minimum(lax.broadcasted_iota(jnp.int32, (NSA_REP * tq, 1), 0) & (tq - 1), dec - 1)
    k = st_ref[:, :KV_W].astype(BF16)
    v = st_ref[:, KV_W:].astype(BF16)
    wnew = wnew_ref[...]
    kn = _pad_rows(wnew[:, :KV_W], LANES).astype(BF16)
    vn = _pad_rows(wnew[:, KV_W:], LANES).astype(BF16)
    outs = []
    for g in range(NSA_KV):
        qrg = _stack_group(qr_ref, g, tq)
        s0 = _dot_nt(qrg, k)
        d0 = wbuf + trow - lax.broadcasted_iota(jnp.int32, s0.shape, 1)
        m0 = (d0 >= 0) & (d0 < WINDOW)
        s1 = _dot_nt(qrg, kn)
        i1 = lax.broadcasted_iota(jnp.int32, s1.shape, 1)
        m1 = (i1 < dec) & (trow - i1 >= 0) & (trow - i1 < WINDOW)
        s0 = jnp.where(m0, s0, -1e30)
        s1 = jnp.where(m1, s1, -1e30)
        mx = jnp.maximum(jnp.max(s0, axis=-1, keepdims=True), jnp.max(s1, axis=-1, keepdims=True))
        e0 = jnp.where(m0, jnp.exp(s0 - mx), 0.0)
        e1 = jnp.where(m1, jnp.exp(s1 - mx), 0.0)
        den = jnp.maximum(jnp.sum(e0, axis=-1, keepdims=True) + jnp.sum(e1, axis=-1, keepdims=True), 1e-30)
        o = (jnp.dot(e0.astype(BF16), v, preferred_element_type=F32)
             + jnp.dot(e1.astype(BF16), vn, preferred_element_type=F32))
        outs.append(o / den)
    o_ref[...] = _unstack_pairs(outs[0], outs[1], tq)


def _window_s(qr, state, wnew, past, dec):
    bd = qr.shape[0]
    per_b = lambda a: pl.BlockSpec((None,) + a.shape[1:], lambda b: (b,) + (0,) * (a.ndim - 1))
    return pl.pallas_call(
        functools.partial(_window_s_kernel, past=past, dec=dec),
        grid=(bd,),
        in_specs=[per_b(qr), per_b(state), per_b(wnew)],
        out_specs=pl.BlockSpec((None, QPAD, NSA_W), lambda b: (b, 0, 0)),
        out_shape=jax.ShapeDtypeStruct((bd, QPAD, NSA_W), F32),
        compiler_params=_cparams(("parallel",), fuse_inputs=[False, True, False]),
        name="window_sample",
    )(qr, state, wnew)


def _sb_s_kernel(pt_ref, q_ref, new_ref, tri_ref, *rest, n_pages, seqs):
    o_ref = rest[seqs * n_pages]
    for s in range(seqs):
        _sb_s_one(q_ref.at[s], new_ref.at[s], tri_ref, rest[n_pages * s:n_pages * (s + 1)], o_ref.at[s], n_pages)


def _sb_s_one(q_ref, new_ref, tri_ref, pages, o_ref, n_pages):
    tq = QPAD
    nrow = SB_HEADS * tq
    row = lax.broadcasted_iota(jnp.int32, (nrow, SB_W), 0)
    col = lax.broadcasted_iota(jnp.int32, (nrow, SB_W), 1)
    own = (row >> 3) == (col >> 6)
    qbd = jnp.where(own, jnp.concatenate([q_ref[...]] * SB_HEADS, axis=0), 0.0).astype(BF16)
    new = new_ref[...]
    nblk = n_pages + 1
    k_all = jnp.concatenate([pg[:, :SB_W].astype(BF16) for pg in pages]
                            + [_pad_rows(new[:, :SB_W], PAGE_SIZE).astype(BF16)], axis=0)
    v_all = jnp.concatenate([pg[:, SB_W:].astype(BF16) for pg in pages]
                            + [_pad_rows(new[:, SB_W:], PAGE_SIZE).astype(BF16)], axis=0)
    z_all = _dot_nt(qbd, k_all)
    blk = lambda x, j: x[j * nrow:(j + 1) * nrow]
    zs = jnp.concatenate([z_all[:, j * PAGE_SIZE:(j + 1) * PAGE_SIZE] for j in range(nblk)], axis=0)
    lane = lax.broadcasted_iota(jnp.int32, (nrow, PAGE_SIZE), 1)
    trow = lax.broadcasted_iota(jnp.int32, (nrow, PAGE_SIZE), 0) & (tq - 1)
    causal_new = lane < trow
    sp = _softplus(zs)
    sp = jnp.concatenate([sp[:n_pages * nrow], jnp.where(causal_new, blk(sp, n_pages), 0.0)], axis=0)
    p = jnp.dot(sp.astype(BF16), tri_ref[...], preferred_element_type=F32)
    carry = jnp.zeros((nrow, 1), F32)
    carries = [None] * nblk
    for j in reversed(range(nblk)):
        carries[j] = jnp.broadcast_to(carry, (nrow, PAGE_SIZE))
        carry = carry + blk(p, j)[:, 0:1]
    a = jnp.exp(zs - (p + jnp.concatenate(carries, axis=0)))
    a_wide = jnp.concatenate([blk(a, j) for j in range(n_pages)] + [jnp.where(causal_new, blk(a, n_pages), 0.0)],
                             axis=1)
    acc = jnp.dot(a_wide.astype(BF16), v_all, preferred_element_type=F32)
    acc = jnp.where(own, acc, 0.0)
    out = acc[0:tq]
    for h in range(1, SB_HEADS):
        out = out + acc[h * tq:(h + 1) * tq]
    o_ref[...] = out


def _sb_s(page_table, sbq, sbnew, cache_sb):
    bd, n_pages = page_table.shape
    seqs = SB_S_SEQS if bd % SB_S_SEQS == 0 else 1
    tri = _tri_matrix(PAGE_SIZE)
    per_b = lambda a: pl.BlockSpec((seqs,) + a.shape[1:], lambda b, pt: (b,) + (0,) * (a.ndim - 1))
    page = lambda s, j: pl.BlockSpec((None,) + cache_sb.shape[1:], lambda b, pt: (pt[b * seqs + s, j], 0, 0))
    return pl.pallas_call(
        functools.partial(_sb_s_kernel, n_pages=n_pages, seqs=seqs),
        grid_spec=pltpu.PrefetchScalarGridSpec(
            num_scalar_prefetch=1,
            grid=(bd // seqs,),
            in_specs=[per_b(sbq), per_b(sbnew), pl.BlockSpec(tri.shape, lambda b, pt: (0, 0))]
            + [page(s, j) for s in range(seqs) for j in range(n_pages)],
            out_specs=pl.BlockSpec((seqs, QPAD, SB_W), lambda b, pt: (b, 0, 0)),
        ),
        out_shape=jax.ShapeDtypeStruct((bd, QPAD, SB_W), F32),
        compiler_params=_cparams(("arbitrary",), fuse_inputs=[False] * 4 + [True] * (seqs * n_pages)),
        name="sb_sample",
    )(page_table, sbq, sbnew, tri, *([cache_sb] * (seqs * n_pages)))


def kernel(x_prompt, x_sample, cache_nsa_cmp, cache_nsa_slc, cache_sb, state_nsa_win, page_table, g_mix, w_in, w_ck,
           w_cv, w_out, g_ffn, w_router, b_router, w_gu, b_gu, w_dn, b_dn, g_final):
    b, s, d = x_prompt.shape
    bd, ds, _ = x_sample.shape
    depth = g_mix.shape[0]
    assert depth == 1 and d == D_MODEL and ds <= QPAD
    n_pages = page_table.shape[1]
    past = n_pages * PAGE_SIZE
    wbuf = state_nsa_win.shape[2]
    n_phys = cache_nsa_cmp.shape[1]
    tp, ts = b * s, bd * ds

    w_perm = _prep_w_in(w_in[0])
    w1, w2 = _prep_w_cmp(w_ck[0], w_cv[0])
    prep = _prep_merge(w_out[0], w_router[0], b_router[0])
    g_mix2, g_ffn2, g_fin2 = g_mix[0].reshape(1, d), g_ffn[0].reshape(1, d), g_final.reshape(1, d)
    bg = b_gu[0][:, None, 0::2]
    bu = b_gu[0][:, None, 1::2]
    bdn = b_dn[0][:, None, :]

    pos_p = jnp.arange(s, dtype=jnp.int32)
    (qc, qr, sbq, cmp_p, slc_p, win_p, sbkv_p, kslc, vslc, kwin, vwin, sbk, sbv, gate_p, br_p) = _proj(
        x_prompt.reshape(tp, d), g_mix2, w_perm, _rope_tables(pos_p), 256)
    kc, vc = _compress(cmp_p, w1, w2, b, s)
    o_cmp, o_slc = _nsa(qc, qr, kc, vc, kslc, vslc, b, s)
    o_win = _window(qr, kwin, vwin, b, s)
    o_sb = _sb(sbq, sbk, sbv, b, s)
    x1_p, h_p, idx_p, wts_p = _merge(x_prompt.reshape(tp, d), o_cmp, o_slc, o_win, o_sb, gate_p, br_p, prep, g_ffn2)

    pos_s = past + jnp.arange(ds, dtype=jnp.int32)
    tabs_s = [jnp.tile(t, (bd, 1)) for t in _rope_tables(pos_s)]
    (qc_s, qr_s, sbq_s, cmp_s, slc_s, win_s, sbkv_s, _, _, _, _, _, _, gate_s, br_s) = _proj(
        x_sample.reshape(ts, d), g_mix2, w_perm, tabs_s, min(256, ts))
    pad_q = lambda a: jnp.pad(a.astype(F32).reshape(bd, ds, a.shape[1]), ((0, 0), (0, QPAD - ds), (0, 0)))
    xnew = jnp.pad(cmp_s.reshape(bd, 1, ds * 2 * KV_W), ((0, 0), (0, 0), (0, (CMP_STRIDE - ds) * 2 * KV_W)))
    o_cmp_s, o_slc_s = _nsa_s(page_table, pad_q(qc_s), pad_q(qr_s), xnew, pad_q(slc_s), w1, w2,
                              cache_nsa_cmp[0].reshape(n_phys, PAGE_SIZE, 2 * KV_W),
                              cache_nsa_slc[0].reshape(n_phys, PAGE_SIZE, 2 * KV_W), past, ds)
    o_win_s = _window_s(pad_q(qr_s), state_nsa_win[0].reshape(bd, wbuf, 2 * KV_W), pad_q(win_s), past, ds)
    o_sb_s = _sb_s(page_table, pad_q(sbq_s), pad_q(sbkv_s), cache_sb[0].reshape(n_phys, PAGE_SIZE, 2 * SB_W))
    unpad = lambda a: a[:, :ds].reshape(ts, a.shape[2])
    x1_s, h_s, idx_s, wts_s = _merge(x_sample.reshape(ts, d), unpad(o_cmp_s), unpad(o_slc_s), unpad(o_win_s),
                                     unpad(o_sb_s), gate_s, br_s, prep, g_ffn2)

    h_rows = jnp.concatenate([h_p, h_s], axis=0)
    idx = jnp.concatenate([idx_p[:, :TOP_K], idx_s[:, :TOP_K]], axis=0)
    row_token, pos, tile_expert, n_used = _route_tables(idx)
    y_sorted = _experts(h_rows, row_token, tile_expert, n_used, w_gu[0], w_dn[0], bg, bu, bdn)
    y_p = _combine(x1_p, wts_p, pos[:tp], y_sorted, g_fin2)
    y_s = _combine(x1_s, wts_s, pos[tp:], y_sorted, g_fin2)

    win_all = jnp.concatenate([state_nsa_win[0], win_s.reshape(bd, ds, 2, NSA_KV, HEAD_DIM)], axis=1)
    lead = lambda a, n, h: a.reshape(1, n, -1, 2, h, HEAD_DIM)
    return (y_p.reshape(b, s, d), y_s.reshape(bd, ds, d),
            lead(cmp_p, b, NSA_KV), lead(slc_p, b, NSA_KV), lead(sbkv_p, b, SB_HEADS),
            lead(win_p, b, NSA_KV)[:, :, -wbuf:] if s >= wbuf else
            jnp.pad(lead(win_p, b, NSA_KV), ((0, 0), (0, 0), (wbuf - s, 0), (0, 0), (0, 0), (0, 0))),
            lead(cmp_s, bd, NSA_KV), lead(slc_s, bd, NSA_KV), lead(sbkv_s, bd, SB_HEADS),
            win_all[None, :, ds:])
```

```python
import functools

import numpy as np
import jax
import jax.numpy as jnp
from jax import lax
from jax.experimental import pallas as pl
from jax.experimental.pallas import tpu as pltpu

D_MODEL = 1024
HEAD_DIM = 64
NSA_HEADS = 8
NSA_KV = 2
NSA_REP = NSA_HEADS // NSA_KV
SB_HEADS = 8
NSA_W = NSA_HEADS * HEAD_DIM
SB_W = SB_HEADS * HEAD_DIM
KV_W = NSA_KV * HEAD_DIM
CMP_BLK = 32
CMP_STRIDE = 16
SEL_BLK = 64
TOP_N = 16
WINDOW = 512
FORCE_BONUS = 1.0e4
ROPE_THETA = 500000.0
ROPE_DIMS = HEAD_DIM // 4
N_EXPERTS = 32
TOP_K = 4
D_FF = D_MODEL
SWIGLU_LIMIT = 7.0
SWIGLU_ALPHA = 1.702
RMS_EPS = 1e-5
PAGE_SIZE = 128
SCALE = HEAD_DIM ** -0.5

LANES = 128
VMEM_LIMIT_BYTES = 56 * 1024 * 1024

BF16 = jnp.bfloat16
F32 = jnp.float32

_OFF = dict(q=0, k_cmp=512, v_cmp=640, k_slc=768, v_slc=896, k_win=1024, v_win=1152, br=1280,
            sb_q=1304, sb_k=1816, sb_v=2328, gate_a=2840, gate_b=3352, end=3864)
_HEAD_PERM = [j + 4 * half for j in range(4) for half in range(2)]
_C_Q, _C_SBQ, _C_CMP, _C_SLC, _C_WIN, _C_SBKV, _C_GATE, _C_BR, _NP = 0, 512, 1024, 1280, 1536, 1792, 2816, 3840, 3968


def _cparams(sem):
    return pltpu.CompilerParams(dimension_semantics=sem, vmem_limit_bytes=VMEM_LIMIT_BYTES)


def _prep_w_in(w):
    def head_perm(base):
        return [w[:, base + h * HEAD_DIM: base + (h + 1) * HEAD_DIM] for h in _HEAD_PERM]
    parts = [c * SCALE for c in head_perm(_OFF['q'])]
    parts.append(w[:, _OFF['sb_q']:_OFF['sb_k']] * SCALE)
    parts.append(w[:, _OFF['k_cmp']:_OFF['br']])
    parts.append(w[:, _OFF['sb_k']:_OFF['gate_a']])
    parts.extend(head_perm(_OFF['gate_a']))
    parts.append(w[:, _OFF['gate_b']:_OFF['end']])
    parts.append(w[:, _OFF['br']:_OFF['sb_q']])
    parts.append(jnp.zeros((w.shape[0], LANES - 3 * NSA_HEADS), w.dtype))
    return jnp.concatenate(parts, axis=1).astype(BF16)


def _rope_tables(pos):
    half = ROPE_DIMS // 2
    inv = ROPE_THETA ** (-jnp.arange(half, dtype=F32) / half)
    ang = pos.astype(F32)[:, None] * inv[None, :]
    cos = jnp.cos(ang)
    sin = jnp.sin(ang)
    n = pos.shape[0]
    one = jnp.ones((n, HEAD_DIM - ROPE_DIMS), F32)
    zero = jnp.zeros((n, HEAD_DIM - ROPE_DIMS), F32)
    z8 = jnp.zeros((n, half), F32)
    c = jnp.concatenate([cos, cos, one], axis=1)
    s1 = jnp.concatenate([-sin, z8, zero], axis=1)
    s2 = jnp.concatenate([z8, sin, zero], axis=1)
    tile2 = lambda t: jnp.concatenate([t, t], axis=1)
    return tile2(c), tile2(s1), tile2(s2)


def _rope_apply(x, c, s1, s2):
    outs = []
    for j in range(x.shape[1] // LANES):
        xb = x[:, j * LANES:(j + 1) * LANES]
        outs.append(xb * c + pltpu.roll(xb, LANES - 8, 1) * s1 + pltpu.roll(xb, 8, 1) * s2)
    return outs[0] if len(outs) == 1 else jnp.concatenate(outs, axis=1)


def _proj_kernel(x_ref, g_ref, w_ref, c_ref, s1_ref, s2_ref,
                 qc_ref, qr_ref, sbq_ref, cmp_ref, slc_ref, win_ref, sbkv_ref,
                 kslc_ref, vslc_ref, kwin_ref, vwin_ref, sbk_ref, sbv_ref, gate_ref, br_ref):
    x = x_ref[...]
    ms = jnp.mean(x * x, axis=-1, keepdims=True)
    h = (x * lax.rsqrt(ms + RMS_EPS)) * g_ref[...]
    hb = h.astype(BF16)
    c, s1, s2 = c_ref[...], s1_ref[...], s2_ref[...]

    def mm(c0, c1):
        return jnp.dot(hb, w_ref[:, c0:c1], preferred_element_type=F32)

    q = mm(_C_Q, _C_SBQ)
    qc_ref[...] = q.astype(BF16)
    qr_ref[...] = _rope_apply(q, c, s1, s2).astype(BF16)
    sbq_ref[...] = mm(_C_SBQ, _C_CMP).astype(BF16)
    cmp_ref[...] = mm(_C_CMP, _C_SLC)
    for c0, cache_ref, k_ref, v_ref in ((_C_SLC, slc_ref, kslc_ref, vslc_ref), (_C_WIN, win_ref, kwin_ref, vwin_ref)):
        z = mm(c0, c0 + 2 * KV_W)
        k = _rope_apply(z[:, :KV_W], c, s1, s2)
        v = z[:, KV_W:]
        cache_ref[:, :KV_W] = k
        cache_ref[:, KV_W:] = v
        k_ref[...] = k.astype(BF16)
        v_ref[...] = v.astype(BF16)
    z = mm(_C_SBKV, _C_GATE)
    sbkv_ref[...] = z
    sbk_ref[...] = z[:, :SB_W].astype(BF16)
    sbv_ref[...] = z[:, SB_W:].astype(BF16)
    gate_ref[...] = jax.nn.sigmoid(mm(_C_GATE, _C_BR)).astype(BF16)
    br_ref[...] = jax.nn.sigmoid(mm(_C_BR, _NP))


def _proj(x2d, g, w_perm, tabs, tm):
    t = x2d.shape[0]
    tab_blocks = tabs[0].shape[0] // tm
    assert t % tm == 0 and tabs[0].shape[0] % tm == 0
    row = lambda n: pl.BlockSpec((tm, n), lambda i: (i, 0))
    tab = pl.BlockSpec((tm, LANES), lambda i: (i % tab_blocks, 0))
    full = lambda a: pl.BlockSpec(a.shape, lambda i: (0,) * a.ndim)
    outs = [(512, BF16), (512, BF16), (512, BF16), (256, F32), (256, F32), (256, F32), (1024, F32),
            (128, BF16), (128, BF16), (128, BF16), (128, BF16), (512, BF16), (512, BF16), (1024, BF16), (128, F32)]
    return pl.pallas_call(
        _proj_kernel,
        grid=(t // tm,),
        in_specs=[row(D_MODEL), full(g), full(w_perm), tab, tab, tab],
        out_specs=[row(n) for n, _ in outs],
        out_shape=[jax.ShapeDtypeStruct((t, n), dt) for n, dt in outs],
        compiler_params=_cparams(("parallel",)),
        name="proj",
    )(x2d, g, w_perm, *tabs)


def _prep_w_cmp(w_ck, w_cv):
    eye2 = jnp.eye(NSA_KV, dtype=w_ck.dtype)
    bd = lambda w: jnp.einsum('ab,ide->iadbe', eye2, w).reshape(CMP_BLK, KV_W, KV_W)
    z = jnp.zeros((CMP_BLK, KV_W, KV_W), w_ck.dtype)
    w_all = jnp.concatenate([jnp.concatenate([bd(w_ck), z], axis=2),
                             jnp.concatenate([z, bd(w_cv)], axis=2)], axis=1)
    w_all = w_all.astype(BF16)
    return (w_all[:CMP_STRIDE].reshape(CMP_STRIDE * 2 * KV_W, 2 * KV_W),
            w_all[CMP_STRIDE:].reshape(CMP_STRIDE * 2 * KV_W, 2 * KV_W))


def _compress_rows(xb, w1, w2):
    n = xb.shape[0]
    y1 = jnp.dot(xb, w1, preferred_element_type=F32)
    y2 = jnp.dot(xb, w2, preferred_element_type=F32)
    y2s = pltpu.roll(y2, n - 1, 0)
    rid = lax.broadcasted_iota(jnp.int32, y2.shape, 0)
    return y1 + jnp.where(rid < n - 1, y2s, 0.0)


def _compress_kernel(x_ref, w1_ref, w2_ref, k_ref, v_ref):
    out = _compress_rows(x_ref[...].astype(BF16), w1_ref[...], w2_ref[...])
    k_ref[...] = out[:, :KV_W].astype(BF16)
    v_ref[...] = out[:, KV_W:].astype(BF16)


def _compress(cache2d, w1, w2, b, s):
    nc = s // CMP_STRIDE
    x = cache2d.reshape(b, nc, CMP_STRIDE * 2 * KV_W)
    full = lambda a: pl.BlockSpec(a.shape, lambda i: (0,) * a.ndim)
    blk = pl.BlockSpec((None, nc, KV_W), lambda i: (i, 0, 0))
    return pl.pallas_call(
        _compress_kernel,
        grid=(b,),
        in_specs=[pl.BlockSpec((None, nc, x.shape[2]), lambda i: (i, 0, 0)), full(w1), full(w2)],
        out_specs=[blk, blk],
        out_shape=[jax.ShapeDtypeStruct((b, nc, KV_W), BF16)] * 2,
        compiler_params=_cparams(("parallel",)),
        name="compress",
    )(x, w1, w2)


_NT = (((1,), (1,)), ((), ()))


def _dot_nt(a, b):
    return lax.dot_general(a, b, _NT, preferred_element_type=F32)


def _dot3(x, m, passes=3):
    d = lambda a: jnp.dot(a, m, preferred_element_type=F32)
    hi = x.astype(BF16)
    r1 = x - hi.astype(F32)
    mid = r1.astype(BF16)
    out = d(hi) + d(mid)
    if passes == 3:
        out = out + d((r1 - mid.astype(F32)).astype(BF16))
    return out


def _stack_group(ref, g, tq):
    half = lax.broadcasted_iota(jnp.int32, (tq, LANES), 1) >> 6
    return jnp.concatenate([jnp.where(half == g, ref[:, j * LANES:(j + 1) * LANES], 0)
                            for j in range(NSA_REP)], axis=0).astype(BF16)


def _unstack_pairs(o0, o1, tq):
    half = lax.broadcasted_iota(jnp.int32, (tq, LANES), 1) >> 6
    return jnp.concatenate([jnp.where(half == 0, o0[j * tq:(j + 1) * tq], o1[j * tq:(j + 1) * tq])
                            for j in range(NSA_REP)], axis=1)


def _imp_matrix(nc, nsel=None):
    m = np.zeros((nc, LANES), np.float32)
    for j in range(nc * CMP_STRIDE // SEL_BLK if nsel is None else nsel):
        for c, wgt in ((4 * j - 1, 0.5), (4 * j, 1.0), (4 * j + 1, 1.0), (4 * j + 2, 1.0), (4 * j + 3, 0.5)):
            if 0 <= c < nc:
                m[c, j] += wgt
    return jnp.asarray(m, BF16)


def _select_blocks(imp_sel, cur, nsel):
    tq = imp_sel.shape[0]
    blk = lax.broadcasted_iota(jnp.int32, (tq, LANES), 1)
    forced = (blk == 0) | (blk == cur) | (blk == cur - 1)
    score = jnp.where(blk <= cur, imp_sel + jnp.where(forced, FORCE_BONUS, 0.0), -1e9)
    st = score.T
    nslab = -(-nsel // 8)
    slabs = [st[8 * v:8 * v + 8] for v in range(nslab)]
    sub = lax.broadcasted_iota(jnp.int32, (8, tq), 0)
    cnts = [jnp.zeros((8, tq), jnp.int32) for _ in range(nslab)]
    for k in range(nsel):
        sk = st[k:k + 1, :]
        for v in range(nslab):
            if 8 * v + 7 < k:
                inc = jnp.where(sk > slabs[v], 1, 0)
            elif 8 * v > k:
                inc = jnp.where(sk >= slabs[v], 1, 0)
            else:
                inc = jnp.where(sub + 8 * v > k, jnp.where(sk >= slabs[v], 1, 0), jnp.where(sk > slabs[v], 1, 0))
            cnts[v] = cnts[v] + inc
    parts = [jnp.where((c < TOP_N) & (sub + 8 * v < nsel), 1.0, 0.0) for v, c in enumerate(cnts)]
    if 8 * nslab < LANES:
        parts.append(jnp.zeros((LANES - 8 * nslab, tq), F32))
    return jnp.concatenate(parts, axis=0).T.astype(BF16)


def _nsa_kernel(qc_ref, qr_ref, kc_ref, vc_ref, ks_ref, vs_ref, m_ref, ocmp_ref, oslc_ref, *, tq, kt):
    q0 = pl.program_id(1) * tq
    nc = kc_ref.shape[0]
    nsel = nc * CMP_STRIDE // SEL_BLK
    qpos = q0 + lax.broadcasted_iota(jnp.int32, (tq, 1), 0)
    tile4 = lambda a: jnp.concatenate([a] * NSA_REP, axis=0)
    cidx = lax.broadcasted_iota(jnp.int32, (tq, nc), 1)
    cbias = tile4(jnp.where(cidx * CMP_STRIDE + (CMP_BLK - 1) <= qpos, 0.0, -1e30))
    row_ok = tile4(jnp.where(qpos >= CMP_BLK - 1, 1.0, 0.0))
    kcol = lax.broadcasted_iota(jnp.int32, (tq, kt), 1)
    eblk = (lax.broadcasted_iota(jnp.int32, (LANES, kt), 0)
            - (lax.broadcasted_iota(jnp.int32, (LANES, kt), 1) >> 6))
    n_full = q0 // kt
    o_c, o_s = [], []
    for g in range(NSA_KV):
        qcg = _stack_group(qc_ref, g, tq)
        qrg = _stack_group(qr_ref, g, tq)
        s = _dot_nt(qcg, kc_ref[...]) + cbias
        e = jnp.exp(s - jnp.max(s, axis=-1, keepdims=True))
        p = e * (row_ok / jnp.maximum(jnp.sum(e, axis=-1, keepdims=True), 1e-30))
        o_c.append(jnp.dot(p.astype(BF16), vc_ref[...], preferred_element_type=F32))
        imp = p[0:tq] + p[tq:2 * tq] + p[2 * tq:3 * tq] + p[3 * tq:4 * tq]
        sel = _select_blocks(_dot3(imp, m_ref[...]), qpos >> 6, nsel)

        def tile_step(t, carry, causal):
            m, l, acc = carry
            k0 = pl.multiple_of(t * kt, kt)
            sc = _dot_nt(qrg, ks_ref[pl.ds(k0, kt), :])
            expand = jnp.where(eblk == t * (kt // SEL_BLK), 1.0, 0.0).astype(BF16)
            bias = (jnp.dot(sel, expand, preferred_element_type=F32) - 1.0) * 1e30
            if causal:
                bias = jnp.where(k0 + kcol <= qpos, bias, -1e30)
            sc = sc + tile4(bias)
            m_new = jnp.maximum(m, jnp.max(sc, axis=-1, keepdims=True))
            a = jnp.exp(sc - m_new)
            alpha = jnp.exp(m - m_new)
            l = alpha * l + jnp.sum(a, axis=-1, keepdims=True)
            acc = alpha * acc + jnp.dot(a.astype(BF16), vs_ref[pl.ds(k0, kt), :], preferred_element_type=F32)
            return m_new, l, acc

        init = (jnp.full((NSA_REP * tq, 1), -1e30, F32), jnp.zeros((NSA_REP * tq, 1), F32),
                jnp.zeros((NSA_REP * tq, LANES), F32))
        carry = lax.fori_loop(0, n_full, lambda t, c: tile_step(t, c, False), init)
        _, l, acc = tile_step(n_full, carry, True)
        o_s.append(acc / jnp.maximum(l, 1e-30))
    ocmp_ref[...] = _unstack_pairs(o_c[0], o_c[1], tq)
    oslc_ref[...] = _unstack_pairs(o_s[0], o_s[1], tq)


def _nsa(qc, qr, kc, vc, ks, vs, b, s, tq=128, kt=512):
    nqb = s // tq
    kt = min(kt, s)
    m = _imp_matrix(s // CMP_STRIDE)
    qspec = pl.BlockSpec((tq, NSA_W), lambda bi, i: (bi * nqb + i, 0))
    cspec = pl.BlockSpec((None, s // CMP_STRIDE, KV_W), lambda bi, i: (bi, 0, 0))
    kspec = pl.BlockSpec((s, KV_W), lambda bi, i: (bi, 0))
    return pl.pallas_call(
        functools.partial(_nsa_kernel, tq=tq, kt=kt),
        grid=(b, nqb),
        in_specs=[qspec, qspec, cspec, cspec, kspec, kspec, pl.BlockSpec(m.shape, lambda bi, i: (0, 0))],
        out_specs=[qspec, qspec],
        out_shape=[jax.ShapeDtypeStruct((b * s, NSA_W), F32)] * 2,
        compiler_params=_cparams(("parallel", "parallel")),
        name="nsa",
    )(qc, qr, kc, vc, ks, vs, m)


def _window_kernel(qr_ref, k_ref, v_ref, o_ref, *, tq, span):
    q0 = pl.program_id(1) * tq
    s_len = k_ref.shape[0]
    start = pl.multiple_of(jnp.minimum(jnp.maximum(q0 - WINDOW, 0), s_len - span), tq)
    qpos = q0 + lax.broadcasted_iota(jnp.int32, (tq, 1), 0)
    k = k_ref[pl.ds(start, span), :]
    v = v_ref[pl.ds(start, span), :]
    diff = qpos - (start + lax.broadcasted_iota(jnp.int32, (tq, span), 1))
    bias = jnp.where((diff >= 0) & (diff < WINDOW), 0.0, -1e30)
    bias = jnp.concatenate([bias] * NSA_REP, axis=0)
    outs = []
    for g in range(NSA_KV):
        qrg = _stack_group(qr_ref, g, tq)
        s = _dot_nt(qrg, k) + bias
        e = jnp.exp(s - jnp.max(s, axis=-1, keepdims=True))
        den = jnp.maximum(jnp.sum(e, axis=-1, keepdims=True), 1e-30)
        outs.append(jnp.dot(e.astype(BF16), v, preferred_element_type=F32) / den)
    o_ref[...] = _unstack_pairs(outs[0], outs[1], tq)


def _window(qr, kw, vw, b, s, tq=128):
    nqb = s // tq
    span = min(WINDOW + tq, s)
    qspec = pl.BlockSpec((tq, NSA_W), lambda bi, i: (bi * nqb + i, 0))
    kspec = pl.BlockSpec((s, KV_W), lambda bi, i: (bi, 0))
    return pl.pallas_call(
        functools.partial(_window_kernel, tq=tq, span=span),
        grid=(b, nqb),
        in_specs=[qspec, kspec, kspec],
        out_specs=qspec,
        out_shape=jax.ShapeDtypeStruct((b * s, NSA_W), F32),
        compiler_params=_cparams(("parallel", "parallel")),
        name="window",
    )(qr, kw, vw)


def _softplus(z):
    return jnp.maximum(z, 0.0) + jnp.log(1.0 + jnp.exp(-jnp.abs(z)))


def _tri_matrix(n):
    return jnp.asarray(np.tril(np.ones((n, n), np.float32)), BF16)


def _sb_block(qh, k_t, v_t, tri, carry, acc, valid):
    z = _dot_nt(qh, k_t)
    sp = _softplus(z)
    if valid is not None:
        sp = jnp.where(valid, sp, 0.0)
    cum = jnp.dot(sp.astype(BF16), tri, preferred_element_type=F32) + carry
    a = jnp.exp(z - cum)
    if valid is not None:
        a = jnp.where(valid, a, 0.0)
    acc = acc + jnp.dot(a.astype(BF16), v_t, preferred_element_type=F32)
    return cum[:, 0:1], acc


def _sb_two_blocks(qh, k_ref, v_ref, tri, ja, jb, state, mask_a, ok_b, tq):
    a0 = pl.multiple_of(ja * tq, tq)
    b0 = pl.multiple_of(jb * tq, tq)
    k_a, v_a = k_ref[pl.ds(a0, tq), :], v_ref[pl.ds(a0, tq), :]
    k_b, v_b = k_ref[pl.ds(b0, tq), :], v_ref[pl.ds(b0, tq), :]
    out = []
    for hh in range(2):
        carry, acc = state[2 * hh], state[2 * hh + 1]
        za = _dot_nt(qh[hh], k_a)
        zb = _dot_nt(qh[hh], k_b)
        spa = _softplus(za)
        spb = _softplus(zb)
        if mask_a is not None:
            spa = jnp.where(mask_a, spa, 0.0)
        if ok_b is not None:
            spb = jnp.where(ok_b, spb, 0.0)
        cuma = jnp.dot(spa.astype(BF16), tri, preferred_element_type=F32) + carry
        cumb = jnp.dot(spb.astype(BF16), tri, preferred_element_type=F32) + cuma[:, 0:1]
        aa = jnp.exp(za - cuma)
        ab = jnp.exp(zb - cumb)
        if mask_a is not None:
            aa = jnp.where(mask_a, aa, 0.0)
        if ok_b is not None:
            ab = jnp.where(ok_b, ab, 0.0)
        acc = (acc + jnp.dot(aa.astype(BF16), v_a, preferred_element_type=F32)
               + jnp.dot(ab.astype(BF16), v_b, preferred_element_type=F32))
        out.extend((cumb[:, 0:1], acc))
    return tuple(out)


def _sb_kernel(q_ref, k_ref, v_ref, tri_ref, o_ref, *, tq):
    i = pl.program_id(2)
    half = lax.broadcasted_iota(jnp.int32, (tq, LANES), 1) >> 6
    q = q_ref[...]
    qh = [jnp.where(half == hh, q, 0) for hh in range(2)]
    tri = tri_ref[...]
    odd = (i & 1) == 1
    causal = lax.broadcasted_iota(jnp.int32, (tq, tq), 1) < lax.broadcasted_iota(jnp.int32, (tq, tq), 0)
    zero = (jnp.zeros((tq, 1), F32), jnp.zeros((tq, LANES), F32))
    state = _sb_two_blocks(qh, k_ref, v_ref, tri, i, jnp.maximum(i - 1, 0), zero + zero, causal, odd, tq)
    first = i - 1 - (i & 1)

    def body(p, st):
        ja = first - 2 * p
        return _sb_two_blocks(qh, k_ref, v_ref, tri, ja, ja - 1, st, None, None, tq)

    st = lax.fori_loop(0, i >> 1, body, state)
    o_ref[...] = jnp.where(half == 0, st[1], st[3])


def _sb(sbq, sbk, sbv, b, s, tq=256):
    tq = min(tq, s)
    nqb = s // tq
    tri = _tri_matrix(tq)
    npair = SB_HEADS // 2
    qspec = pl.BlockSpec((tq, LANES), lambda bi, p, i: (bi * nqb + i, p))
    kspec = pl.BlockSpec((s, LANES), lambda bi, p, i: (bi, p))
    return pl.pallas_call(
        functools.partial(_sb_kernel, tq=tq),
        grid=(b, npair, nqb),
        in_specs=[qspec, kspec, kspec, pl.BlockSpec(tri.shape, lambda bi, p, i: (0, 0))],
        out_specs=qspec,
        out_shape=jax.ShapeDtypeStruct((b * s, SB_W), F32),
        compiler_params=_cparams(("parallel", "parallel", "parallel")),
        name="sb",
    )(sbq, sbk, sbv, tri)


def _prep_merge(w_out, w_router, b_router):
    w_a = [w_out[h * HEAD_DIM:(h + 1) * HEAD_DIM] for h in _HEAD_PERM]
    w_out_p = jnp.concatenate(w_a + [w_out[NSA_W:]], axis=0).astype(BF16)
    e = np.zeros((LANES, 3 * NSA_W), np.float32)
    for br in range(3):
        for p, h in enumerate(_HEAD_PERM):
            e[br * NSA_HEADS + h, br * NSA_W + p * HEAD_DIM: br * NSA_W + (p + 1) * HEAD_DIM] = 1.0
    w_r = jnp.concatenate([w_router, jnp.zeros((D_MODEL, LANES - N_EXPERTS), w_router.dtype)], axis=1)
    b_r = jnp.concatenate([b_router, jnp.full((LANES - N_EXPERTS,), -1e30, b_router.dtype)]).reshape(1, LANES)
    return w_out_p, jnp.asarray(e, BF16), w_r, b_r


def _merge_kernel(x_ref, oc_ref, os_ref, ow_ref, ob_ref, gate_ref, br_ref, e_ref, wo_ref, g_ref, wr_ref, brt_ref,
                  x1_ref, hp_ref, idx_ref, wts_ref):
    gexp = _dot3(br_ref[...], e_ref[...], passes=2)
    o_a = gexp[:, :NSA_W] * oc_ref[...] + gexp[:, NSA_W:2 * NSA_W] * os_ref[...] + gexp[:, 2 * NSA_W:] * ow_ref[...]
    gate = gate_ref[...].astype(F32)
    cat = jnp.concatenate([gate[:, :NSA_W] * o_a, gate[:, NSA_W:] * ob_ref[...]], axis=1).astype(BF16)
    x1 = x_ref[...] + jnp.dot(cat, wo_ref[...], preferred_element_type=F32)
    x1_ref[...] = x1
    ms = jnp.mean(x1 * x1, axis=-1, keepdims=True)
    h = (x1 * lax.rsqrt(ms + RMS_EPS)) * g_ref[...]
    hb = h.astype(BF16).astype(F32)
    for j in range(D_MODEL // LANES):
        hp_ref[:, j, :] = hb[:, j * LANES:(j + 1) * LANES]
    lg = jnp.dot(h, wr_ref[...], preferred_element_type=F32, precision=lax.Precision.HIGHEST) + brt_ref[...]
    lane = lax.broadcasted_iota(jnp.int32, lg.shape, 1)
    vals, ids = [], []
    for _ in range(TOP_K):
        mx = jnp.max(lg, axis=-1, keepdims=True)
        first = jnp.min(jnp.where(lg == mx, lane, LANES), axis=-1, keepdims=True)
        vals.append(mx)
        ids.append(first)
        lg = jnp.where(lane == first, -jnp.inf, lg)
    es = [jnp.exp(v - vals[0]) for v in vals]
    tot = es[0] + es[1] + es[2] + es[3]
    idx = jnp.zeros(lg.shape, jnp.int32)
    wts = jnp.zeros(lg.shape, F32)
    for k in range(TOP_K):
        idx = jnp.where(lane == k, ids[k], idx)
        wts = jnp.where(lane == k, es[k] / tot, wts)
    idx_ref[...] = idx
    wts_ref[...] = wts


def _merge(x2d, oc, os_, ow, ob, gate, br, prep, g_ffn, tm=256):
    t = x2d.shape[0]
    tm = min(tm, t)
    assert t % tm == 0
    w_out_p, e, w_r, b_r = prep
    row = lambda n: pl.BlockSpec((tm, n), lambda i: (i, 0))
    full = lambda a: pl.BlockSpec(a.shape, lambda i: (0,) * a.ndim)
    return pl.pallas_call(
        _merge_kernel,
        grid=(t // tm,),
        in_specs=[row(D_MODEL), row(NSA_W), row(NSA_W), row(NSA_W), row(SB_W), row(NSA_W + SB_W), row(LANES),
                  full(e), full(w_out_p), full(g_ffn), full(w_r), full(b_r)],
        out_specs=[row(D_MODEL), pl.BlockSpec((tm, ROW_SUB, LANES), lambda i: (i, 0, 0)), row(LANES), row(LANES)],
        out_shape=[jax.ShapeDtypeStruct((t, D_MODEL), F32), jax.ShapeDtypeStruct((t, ROW_SUB, LANES), F32),
                   jax.ShapeDtypeStruct((t, LANES), jnp.int32), jax.ShapeDtypeStruct((t, LANES), F32)],
        compiler_params=_cparams(("parallel",)),
        name="merge",
    )(x2d, oc, os_, ow, ob, gate, br, e, w_out_p, g_ffn, w_r, b_r)


MOE_TILE = 256
COMBINE_ROWS = 128
ROW_SUB = D_MODEL // LANES


def _route_tables(idx):
    t = idx.shape[0]
    a = t * TOP_K
    assert a % MOE_TILE == 0
    n_tiles = a // MOE_TILE + N_EXPERTS
    e = idx.reshape(a)
    onehot = (e[:, None] == jnp.arange(N_EXPERTS, dtype=jnp.int32)[None, :]).astype(jnp.int32)
    csum = jnp.cumsum(onehot, axis=0)
    rank = jnp.take_along_axis(csum, e[:, None], axis=1)[:, 0] - 1
    counts = csum[-1]
    tiles_e = (counts + MOE_TILE - 1) // MOE_TILE
    tile_end = jnp.cumsum(tiles_e)
    tile_start = tile_end - tiles_e
    dest = tile_start[e] * MOE_TILE + rank
    row_token = jnp.zeros((n_tiles * MOE_TILE,), jnp.int32).at[dest].set(jnp.arange(a, dtype=jnp.int32) // TOP_K)
    tile_ids = jnp.arange(n_tiles, dtype=jnp.int32)
    tile_expert = jnp.minimum(jnp.sum((tile_end[None, :] <= tile_ids[:, None]).astype(jnp.int32), axis=1),
                              N_EXPERTS - 1).astype(jnp.int32)
    return (row_token.reshape(n_tiles, 1, MOE_TILE), dest.reshape(t, TOP_K), tile_expert,
            tile_end[-1:].astype(jnp.int32))


def _deinterleave_matrix():
    m = np.zeros((2 * LANES, 2 * LANES), np.float32)
    m[2 * np.arange(LANES), np.arange(LANES)] = 1.0
    m[2 * np.arange(LANES) + 1, LANES + np.arange(LANES)] = 1.0
    return jnp.asarray(m, BF16)


def _row_tiles_to_2d(load):
    return jnp.concatenate([load(j) for j in range(ROW_SUB)], axis=1)


def _expert_kernel(te_ref, nu_ref, rt_ref, rtn_ref, h_ref, wgu_ref, wdn_ref, bg_ref, bu_ref, bd_ref, seo_ref,
                   y_ref, xbuf, sem, wg_s, wu_s, wd_s):
    i = pl.program_id(0)
    n_used = nu_ref[0]
    slot = i % 2

    def copy(idx_ref, r, s):
        return pltpu.make_async_copy(h_ref.at[pl.ds(idx_ref[0, r], 1)], xbuf.at[s, pl.ds(r, 1)], sem.at[s])

    def issue(idx_ref, s):
        def body(r, c):
            copy(idx_ref, r, s).start()
            return c
        lax.fori_loop(0, MOE_TILE, body, 0, unroll=8)

    def wait_rows():
        def wait(r, c):
            copy(rt_ref, r, slot).wait()
            return c
        lax.fori_loop(0, MOE_TILE, wait, 0, unroll=8)

    @pl.when(i == 0)
    def _():
        issue(rt_ref, 0)

    @pl.when(i + 1 < n_used)
    def _():
        issue(rtn_ref, 1 - slot)

    @pl.when(i < n_used)
    def _():
        @pl.when((i == 0) | (te_ref[i] != te_ref[jnp.maximum(i - 1, 0)]))
        def _():
            for k in range(D_FF // LANES):
                wblk = wgu_ref[:, 2 * LANES * k:2 * LANES * (k + 1)].astype(BF16)
                split = jnp.dot(wblk, seo_ref[...], preferred_element_type=F32).astype(BF16)
                wg_s[:, LANES * k:LANES * (k + 1)] = split[:, :LANES]
                wu_s[:, LANES * k:LANES * (k + 1)] = split[:, LANES:]
            wd_s[...] = wdn_ref[...].astype(BF16)

        wait_rows()
        x = _row_tiles_to_2d(lambda j: xbuf[slot, :, j, :]).astype(BF16)
        g = jnp.minimum(jnp.dot(x, wg_s[...], preferred_element_type=F32) + bg_ref[...], SWIGLU_LIMIT)
        u = jnp.clip(jnp.dot(x, wu_s[...], preferred_element_type=F32) + bu_ref[...], -SWIGLU_LIMIT, SWIGLU_LIMIT)
        act = (u + 1.0) * g * jax.nn.sigmoid(SWIGLU_ALPHA * g)
        y = jnp.dot(act.astype(BF16), wd_s[...], preferred_element_type=F32) + bd_ref[...]
        for j in range(ROW_SUB):
            y_ref[:, j, :] = y[:, j * LANES:(j + 1) * LANES]

    @pl.when(i >= n_used)
    def _():
        y_ref[...] = jnp.zeros(y_ref.shape, F32)


def _experts(h_rows, row_token, tile_expert, n_used, w_gu, w_dn, bg, bu, bd):
    n_tiles = row_token.shape[0]
    seo = _deinterleave_matrix()
    wspec = lambda shape: pl.BlockSpec((None,) + shape, lambda i, te, nu: (te[i], 0, 0))
    full = lambda a: pl.BlockSpec(a.shape, lambda i, te, nu: (0,) * a.ndim)
    rt_spec = lambda off: pl.BlockSpec((None, 1, MOE_TILE), lambda i, te, nu: (jnp.minimum(i + off, n_tiles - 1), 0, 0),
                                       memory_space=pltpu.SMEM)
    return pl.pallas_call(
        _expert_kernel,
        grid_spec=pltpu.PrefetchScalarGridSpec(
            num_scalar_prefetch=2,
            grid=(n_tiles,),
            in_specs=[rt_spec(0), rt_spec(1), pl.BlockSpec(memory_space=pl.ANY),
                      wspec((D_MODEL, 2 * D_FF)), wspec((D_FF, D_MODEL)), wspec((1, D_FF)), wspec((1, D_FF)),
                      wspec((1, D_MODEL)), full(seo)],
            out_specs=pl.BlockSpec((MOE_TILE, ROW_SUB, LANES), lambda i, te, nu: (i, 0, 0)),
            scratch_shapes=[pltpu.VMEM((2, MOE_TILE, ROW_SUB, LANES), F32), pltpu.SemaphoreType.DMA((2,)),
                            pltpu.VMEM((D_MODEL, D_FF), BF16), pltpu.VMEM((D_MODEL, D_FF), BF16),
                            pltpu.VMEM((D_FF, D_MODEL), BF16)],
        ),
        out_shape=jax.ShapeDtypeStruct((n_tiles * MOE_TILE, ROW_SUB, LANES), F32),
        compiler_params=_cparams(("arbitrary",)),
        name="moe_experts",
    )(tile_expert, n_used, row_token, row_token, h_rows, w_gu, w_dn, bg, bu, bd, seo)


def _combine_kernel(pos_ref, posn_ref, x_ref, wts_ref, g_ref, y_ref, out_ref, buf, sem, *, n):
    i = pl.program_id(0)
    n_rows = TOP_K * COMBINE_ROWS

    def copy(p_ref, r, slot):
        return pltpu.make_async_copy(y_ref.at[pl.ds(p_ref[0, r], 1)],
                                     buf.at[slot, r & (TOP_K - 1), pl.ds(r >> 2, 1)], sem.at[slot])

    def issue(p_ref, slot):
        def body(r2, c):
            copy(p_ref, 2 * r2, slot).start(priority=0)
            copy(p_ref, 2 * r2 + 1, slot).start(priority=1)
            return c
        lax.fori_loop(0, n_rows // 2, body, 0, unroll=4)

    slot = i % 2

    @pl.when(i == 0)
    def _():
        issue(pos_ref, 0)

    @pl.when(i + 1 < n)
    def _():
        issue(posn_ref, 1 - slot)

    def wait(r, c):
        copy(pos_ref, r, slot).wait()
        return c
    lax.fori_loop(0, n_rows, wait, 0, unroll=8)

    w = wts_ref[...]
    parts = []
    ssq = jnp.zeros((COMBINE_ROWS, 1), F32)
    for j in range(ROW_SUB):
        xs = x_ref[:, j * LANES:(j + 1) * LANES]
        for k in range(TOP_K):
            xs = xs + w[:, k:k + 1] * buf[slot, k, :, j, :]
        parts.append(xs)
        ssq = ssq + jnp.sum(xs * xs, axis=-1, keepdims=True)
    rs = lax.rsqrt(ssq * (1.0 / D_MODEL) + RMS_EPS)
    for j in range(ROW_SUB):
        out_ref[:, j * LANES:(j + 1) * LANES] = (parts[j] * rs) * g_ref[:, j * LANES:(j + 1) * LANES]


def _combine(x1, wts, pos, y_sorted, g_final):
    t = x1.shape[0]
    assert t % COMBINE_ROWS == 0
    steps = t // COMBINE_ROWS
    pos2 = pos.reshape(steps, 1, TOP_K * COMBINE_ROWS)
    row = lambda nn: pl.BlockSpec((COMBINE_ROWS, nn), lambda i: (i, 0))
    return pl.pallas_call(
        functools.partial(_combine_kernel, n=steps),
        grid=(steps,),
        in_specs=[pl.BlockSpec((None, 1, TOP_K * COMBINE_ROWS), lambda i: (i, 0, 0), memory_space=pltpu.SMEM),
                  pl.BlockSpec((None, 1, TOP_K * COMBINE_ROWS), lambda i: (jnp.minimum(i + 1, steps - 1), 0, 0),
                               memory_space=pltpu.SMEM),
                  row(D_MODEL), row(LANES), pl.BlockSpec(g_final.shape, lambda i: (0, 0)),
                  pl.BlockSpec(memory_space=pl.ANY)],
        out_specs=row(D_MODEL),
        out_shape=jax.ShapeDtypeStruct((t, D_MODEL), F32),
        scratch_shapes=[pltpu.VMEM((2, TOP_K, COMBINE_ROWS, ROW_SUB, LANES), F32), pltpu.SemaphoreType.DMA((2,))],
        compiler_params=_cparams(("arbitrary",)),
        name="moe_combine",
    )(pos2, pos2, x1, wts, g_final, y_sorted)


QPAD = 8


def _select_blocks_rows(imp_sel, cur, nsel):
    tq = imp_sel.shape[0]
    blk = lax.broadcasted_iota(jnp.int32, (tq, LANES), 1)
    forced = (blk == 0) | (blk == cur) | (blk == cur - 1)
    score = jnp.where(blk <= cur, imp_sel + jnp.where(forced, FORCE_BONUS, 0.0), -1e9)
    cnt = jnp.zeros((tq, LANES), jnp.int32)
    for k in range(nsel):
        sk = score[:, k:k + 1]
        beats = (sk > score) | ((sk == score) & (blk > k))
        cnt = cnt + jnp.where(beats, 1, 0)
    return jnp.where((cnt < TOP_N) & (blk < nsel), 1.0, 0.0)


def _pad_rows(x, n):
    return jnp.concatenate([x, jnp.zeros((n - x.shape[0], x.shape[1]), x.dtype)], axis=0)


NSA_S_SEQS = 4
SB_S_SEQS = 2


def _nsa_s_kernel(pt_ref, qc_ref, qr_ref, xnew_ref, snew_ref, w1_ref, w2_ref, m_ref, *rest, n_pages, past, dec, seqs):
    ocmp_ref, oslc_ref = rest[2 * seqs * n_pages:]
    pages = [rest[2 * n_pages * s:2 * n_pages * (s + 1)] for s in range(seqs)]
    nc = n_pages * (PAGE_SIZE // CMP_STRIDE)
    xb = jnp.concatenate([p[...] for s in range(seqs) for p in pages[s][:n_pages]], axis=0).astype(BF16)
    y1 = jnp.dot(xb, w1_ref[...], preferred_element_type=F32)
    y2 = jnp.dot(xb, w2_ref[...], preferred_element_type=F32)
    xn = jnp.concatenate([xnew_ref[s] for s in range(seqs)] + [xnew_ref[0]] * (8 - seqs), axis=0)
    ynew = jnp.dot(xn.astype(BF16), w2_ref[...], preferred_element_type=F32)
    rid = lax.broadcasted_iota(jnp.int32, (nc, y2.shape[1]), 0)
    for s in range(seqs):
        y2s = pltpu.roll(y2[s * nc:(s + 1) * nc], nc - 1, 0)
        kcv = y1[s * nc:(s + 1) * nc] + jnp.where(rid < nc - 1, y2s, ynew[s:s + 1])
        _nsa_s_one(qc_ref.at[s], qr_ref.at[s], snew_ref.at[s], m_ref, kcv[:, :KV_W].astype(BF16),
                   kcv[:, KV_W:].astype(BF16), pages[s][n_pages:], ocmp_ref.at[s], oslc_ref.at[s], n_pages, past, dec)


def _nsa_s_one(qc_ref, qr_ref, snew_ref, m_ref, kc, vc, slc_pages, ocmp_ref, oslc_ref, n_pages, past, dec):
    tq = QPAD
    nsel = past // SEL_BLK + 1
    trow = lax.broadcasted_iota(jnp.int32, (NSA_REP * tq, 1), 0) & (tq - 1)
    rpos = past + jnp.minimum(trow, dec - 1)
    qpos = past + jnp.minimum(lax.broadcasted_iota(jnp.int32, (tq, 1), 0), dec - 1)
    lane = lax.broadcasted_iota(jnp.int32, (tq, LANES), 1)
    snew = snew_ref[...]
    k_all = jnp.concatenate([pg[:, :KV_W].astype(BF16) for pg in slc_pages]
                            + [_pad_rows(snew[:, :KV_W], LANES).astype(BF16)], axis=0)
    v_all = jnp.concatenate([pg[:, KV_W:].astype(BF16) for pg in slc_pages]
                            + [_pad_rows(snew[:, KV_W:], LANES).astype(BF16)], axis=0)
    o_c, o_s = [], []
    for g in range(NSA_KV):
        qcg = _stack_group(qc_ref, g, tq)
        qrg = _stack_group(qr_ref, g, tq)
        s = _dot_nt(qcg, kc)
        cidx = lax.broadcasted_iota(jnp.int32, s.shape, 1)
        cmask = cidx * CMP_STRIDE + (CMP_BLK - 1) <= rpos
        s = jnp.where(cmask, s, -1e30)
        e = jnp.where(cmask, jnp.exp(s - jnp.max(s, axis=-1, keepdims=True)), 0.0)
        p = e / jnp.maximum(jnp.sum(e, axis=-1, keepdims=True), 1e-30)
        o_c.append(jnp.dot(p.astype(BF16), vc, preferred_element_type=F32))
        imp = p[0:tq] + p[tq:2 * tq] + p[2 * tq:3 * tq] + p[3 * tq:4 * tq]
        sel = _select_blocks_rows(_dot3(imp, m_ref[...]), qpos >> 6, nsel)
        masks =[jnp.where(lane < SEL_BLK, sel[:, 2 * j:2 * j + 1], sel[:, 2 * j + 1:2 * j + 2]) > 0.5
                 for j in range(n_pages)]
        masks.append((sel[:, nsel - 1:nsel] > 0.5) & (lane < dec) & (past + lane <= qpos))
        valid = jnp.concatenate([jnp.concatenate(masks, axis=1)] * NSA_REP, axis=0)
        sc = jnp.where(valid, _dot_nt(qrg, k_all), -1e30)
        a = jnp.where(valid, jnp.exp(sc - jnp.max(sc, axis=-1, keepdims=True)), 0.0)
        l = jnp.sum(a, axis=-1, keepdims=True)
        o_s.append(jnp.dot(a.astype(BF16), v_all, preferred_element_type=F32) / jnp.maximum(l, 1e-30))
    ocmp_ref[...] = _unstack_pairs(o_c[0], o_c[1], tq)
    oslc_ref[...] = _unstack_pairs(o_s[0], o_s[1], tq)


def _nsa_s(page_table, qc, qr, xnew, snew, w1, w2, cache_cmp, cache_slc, past, dec):
    bd, n_pages = page_table.shape
    seqs = NSA_S_SEQS if bd % NSA_S_SEQS == 0 else 1
    cpp = PAGE_SIZE // CMP_STRIDE
    cmp_pages = cache_cmp.reshape(cache_cmp.shape[0], cpp, CMP_STRIDE * 2 * KV_W)
    m = _imp_matrix(n_pages * cpp, past // SEL_BLK + 1)
    per_b = lambda a: pl.BlockSpec((seqs,) + a.shape[1:], lambda b, pt: (b,) + (0,) * (a.ndim - 1))
    full = lambda a: pl.BlockSpec(a.shape, lambda b, pt: (0,) * a.ndim)
    page = lambda a, s, j: pl.BlockSpec((None,) + a.shape[1:], lambda b, pt: (pt[b * seqs + s, j], 0, 0))
    ospec = pl.BlockSpec((seqs, QPAD, NSA_W), lambda b, pt: (b, 0, 0))
    page_specs, page_args = [], []
    for s in range(seqs):
        page_specs += [page(cmp_pages, s, j) for j in range(n_pages)] + [page(cache_slc, s, j) for j in range(n_pages)]
        page_args += [cmp_pages] * n_pages + [cache_slc] * n_pages
    return pl.pallas_call(
        functools.partial(_nsa_s_kernel, n_pages=n_pages, past=past, dec=dec, seqs=seqs),
        grid_spec=pltpu.PrefetchScalarGridSpec(
            num_scalar_prefetch=1,
            grid=(bd // seqs,),
            in_specs=[per_b(qc), per_b(qr), per_b(xnew), per_b(snew), full(w1), full(w2), full(m)] + page_specs,
            out_specs=[ospec, ospec],
        ),
        out_shape=[jax.ShapeDtypeStruct((bd, QPAD, NSA_W), F32)] * 2,
        compiler_params=_cparams(("arbitrary",)),
        name="nsa_sample",
    )(page_table, qc, qr, xnew, snew, w1, w2, m, *page_args)


def _window_s_kernel(qr_ref, st_ref, wnew_ref, o_ref, *, past, dec):
    tq = QPAD
    wbuf = st_ref.shape[0]
    trow = jnp.minimum(lax.broadcasted_iota(jnp.int32, (NSA_REP * tq, 1), 0) & (tq - 1), dec - 1)
    k = st_ref[:, :KV_W].astype(BF16)
    v = st_ref[:, KV_W:].astype(BF16)
    wnew = wnew_ref[...]
    kn = _pad_rows(wnew[:, :KV_W], LANES).astype(BF16)
    vn = _pad_rows(wnew[:, KV_W:], LANES).astype(BF16)
    outs = []
    for g in range(NSA_KV):
        qrg = _stack_group(qr_ref, g, tq)
        s0 = _dot_nt(qrg, k)
        d0 = wbuf + trow - lax.broadcasted_iota(jnp.int32, s0.shape, 1)
        m0 = (d0 >= 0) & (d0 < WINDOW)
        s1 = _dot_nt(qrg, kn)
        i1 = lax.broadcasted_iota(jnp.int32, s1.shape, 1)
        m1 = (i1 < dec) & (trow - i1 >= 0) & (trow - i1 < WINDOW)
        s0 = jnp.where(m0, s0, -1e30)
        s1 = jnp.where(m1, s1, -1e30)
        mx = jnp.maximum(jnp.max(s0, axis=-1, keepdims=True), jnp.max(s1, axis=-1, keepdims=True))
        e0 = jnp.where(m0, jnp.exp(s0 - mx), 0.0)
        e1 = jnp.where(m1, jnp.exp(s1 - mx), 0.0)
        den = jnp.maximum(jnp.sum(e0, axis=-1, keepdims=True) + jnp.sum(e1, axis=-1, keepdims=True), 1e-30)
        o = (jnp.dot(e0.astype(BF16), v, preferred_element_type=F32)
             + jnp.dot(e1.astype(BF16), vn, preferred_element_type=F32))
        outs.append(o / den)
    o_ref[...] = _unstack_pairs(outs[0], outs[1], tq)


def _window_s(qr, state, wnew, past, dec):
    bd = qr.shape[0]
    per_b = lambda a: pl.BlockSpec((None,) + a.shape[1:], lambda b: (b,) + (0,) * (a.ndim - 1))
    return pl.pallas_call(
        functools.partial(_window_s_kernel, past=past, dec=dec),
        grid=(bd,),
        in_specs=[per_b(qr), per_b(state), per_b(wnew)],
        out_specs=pl.BlockSpec((None, QPAD, NSA_W), lambda b: (b, 0, 0)),
        out_shape=jax.ShapeDtypeStruct((bd, QPAD, NSA_W), F32),
        compiler_params=_cparams(("parallel",)),
        name="window_sample",
    )(qr, state, wnew)


def _sb_s_kernel(pt_ref, q_ref, new_ref, tri_ref, *rest, n_pages, seqs):
    o_ref = rest[seqs * n_pages]
    for s in range(seqs):
        _sb_s_one(q_ref.at[s], new_ref.at[s], tri_ref, rest[n_pages * s:n_pages * (s + 1)], o_ref.at[s], n_pages)


def _sb_s_one(q_ref, new_ref, tri_ref, pages, o_ref, n_pages):
    tq = QPAD
    nrow = SB_HEADS * tq
    row = lax.broadcasted_iota(jnp.int32, (nrow, SB_W), 0)
    col = lax.broadcasted_iota(jnp.int32, (nrow, SB_W), 1)
    own = (row >> 3) == (col >> 6)
    qbd = jnp.where(own, jnp.concatenate([q_ref[...]] * SB_HEADS, axis=0), 0.0).astype(BF16)
    new = new_ref[...]
    nblk = n_pages + 1
    k_all = jnp.concatenate([pg[:, :SB_W].astype(BF16) for pg in pages]
                            + [_pad_rows(new[:, :SB_W], PAGE_SIZE).astype(BF16)], axis=0)
    v_all = jnp.concatenate([pg[:, SB_W:].astype(BF16) for pg in pages]
                            + [_pad_rows(new[:, SB_W:], PAGE_SIZE).astype(BF16)], axis=0)
    z_all = _dot_nt(qbd, k_all)
    blk = lambda x, j: x[j * nrow:(j + 1) * nrow]
    zs = jnp.concatenate([z_all[:, j * PAGE_SIZE:(j + 1) * PAGE_SIZE] for j in range(nblk)], axis=0)
    lane = lax.broadcasted_iota(jnp.int32, (nrow, PAGE_SIZE), 1)
    trow = lax.broadcasted_iota(jnp.int32, (nrow, PAGE_SIZE), 0) & (tq - 1)
    causal_new = lane < trow
    sp = _softplus(zs)
    sp = jnp.concatenate([sp[:n_pages * nrow], jnp.where(causal_new, blk(sp, n_pages), 0.0)], axis=0)
    p = jnp.dot(sp.astype(BF16), tri_ref[...], preferred_element_type=F32)
    carry = jnp.zeros((nrow, 1), F32)
    carries = [None] * nblk
    for j in reversed(range(nblk)):
        carries[j] = jnp.broadcast_to(carry, (nrow, PAGE_SIZE))
        carry = carry + blk(p, j)[:, 0:1]
    a = jnp.exp(zs - (p + jnp.concatenate(carries, axis=0)))
    a_wide = jnp.concatenate([blk(a, j) for j in range(n_pages)] + [jnp.where(causal_new, blk(a, n_pages), 0.0)],
                             axis=1)
    acc = jnp.dot(a_wide.astype(BF16), v_all, preferred_element_type=F32)
    acc = jnp.where(own, acc, 0.0)
    out = acc[0:tq]
    for h in range(1, SB_HEADS):
        out = out + acc[h * tq:(h + 1) * tq]
    o_ref[...] = out


def _sb_s(page_table, sbq, sbnew, cache_sb):
    bd, n_pages = page_table.shape
    seqs = SB_S_SEQS if bd % SB_S_SEQS == 0 else 1
    tri = _tri_matrix(PAGE_SIZE)
    per_b = lambda a: pl.BlockSpec((seqs,) + a.shape[1:], lambda b, pt: (b,) + (0,) * (a.ndim - 1))
    page = lambda s, j: pl.BlockSpec((None,) + cache_sb.shape[1:], lambda b, pt: (pt[b * seqs + s, j], 0, 0))
    return pl.pallas_call(
        functools.partial(_sb_s_kernel, n_pages=n_pages, seqs=seqs),
        grid_spec=pltpu.PrefetchScalarGridSpec(
            num_scalar_prefetch=1,
            grid=(bd // seqs,),
            in_specs=[per_b(sbq), per_b(sbnew), pl.BlockSpec(tri.shape, lambda b, pt: (0, 0))]
            + [page(s, j) for s in range(seqs) for j in range(n_pages)],
            out_specs=pl.BlockSpec((seqs, QPAD, SB_W), lambda b, pt: (b, 0, 0)),
        ),
        out_shape=jax.ShapeDtypeStruct((bd, QPAD, SB_W), F32),
        compiler_params=_cparams(("arbitrary",)),
        name="sb_sample",
    )(page_table, sbq, sbnew, tri, *([cache_sb] * (seqs * n_pages)))


def kernel(x_prompt, x_sample, cache_nsa_cmp, cache_nsa_slc, cache_sb, state_nsa_win, page_table, g_mix, w_in, w_ck,
           w_cv, w_out, g_ffn, w_router, b_router, w_gu, b_gu, w_dn, b_dn, g_final):
    b, s, d = x_prompt.shape
    bd, ds, _ = x_sample.shape
    depth = g_mix.shape[0]
    assert depth == 1 and d == D_MODEL and ds <= QPAD
    n_pages = page_table.shape[1]
    past = n_pages * PAGE_SIZE
    wbuf = state_nsa_win.shape[2]
    n_phys = cache_nsa_cmp.shape[1]
    tp, ts = b * s, bd * ds

    w_perm = _prep_w_in(w_in[0])
    w1, w2 = _prep_w_cmp(w_ck[0], w_cv[0])
    prep = _prep_merge(w_out[0], w_router[0], b_router[0])
    g_mix2, g_ffn2, g_fin2 = g_mix[0].reshape(1, d), g_ffn[0].reshape(1, d), g_final.reshape(1, d)
    bg = b_gu[0][:, None, 0::2]
    bu = b_gu[0][:, None, 1::2]
    bdn = b_dn[0][:, None, :]

    pos_p = jnp.arange(s, dtype=jnp.int32)
    (qc, qr, sbq, cmp_p, slc_p, win_p, sbkv_p, kslc, vslc, kwin, vwin, sbk, sbv, gate_p, br_p) = _proj(
        x_prompt.reshape(tp, d), g_mix2, w_perm, _rope_tables(pos_p), 256)
    kc, vc = _compress(cmp_p, w1, w2, b, s)
    o_cmp, o_slc = _nsa(qc, qr, kc, vc, kslc, vslc, b, s)
    o_win = _window(qr, kwin, vwin, b, s)
    o_sb = _sb(sbq, sbk, sbv, b, s)
    x1_p, h_p, idx_p, wts_p = _merge(x_prompt.reshape(tp, d), o_cmp, o_slc, o_win, o_sb, gate_p, br_p, prep, g_ffn2)

    pos_s = past + jnp.arange(ds, dtype=jnp.int32)
    tabs_s = [jnp.tile(t, (bd, 1)) for t in _rope_tables(pos_s)]
    (qc_s, qr_s, sbq_s, cmp_s, slc_s, win_s, sbkv_s, _, _, _, _, _, _, gate_s, br_s) = _proj(
        x_sample.reshape(ts, d), g_mix2, w_perm, tabs_s, min(256, ts))
    pad_q = lambda a: jnp.pad(a.astype(F32).reshape(bd, ds, a.shape[1]), ((0, 0), (0, QPAD - ds), (0, 0)))
    xnew = jnp.pad(cmp_s.reshape(bd, 1, ds * 2 * KV_W), ((0, 0), (0, 0), (0, (CMP_STRIDE - ds) * 2 * KV_W)))
    o_cmp_s, o_slc_s = _nsa_s(page_table, pad_q(qc_s), pad_q(qr_s), xnew, pad_q(slc_s), w1, w2,
                              cache_nsa_cmp[0].reshape(n_phys, PAGE_SIZE, 2 * KV_W),
                              cache_nsa_slc[0].reshape(n_phys, PAGE_SIZE, 2 * KV_W), past, ds)
    o_win_s = _window_s(pad_q(qr_s), state_nsa_win[0].reshape(bd, wbuf, 2 * KV_W), pad_q(win_s), past, ds)
    o_sb_s = _sb_s(page_table, pad_q(sbq_s), pad_q(sbkv_s), cache_sb[0].reshape(n_phys, PAGE_SIZE, 2 * SB_W))
    unpad = lambda a: a[:, :ds].reshape(ts, a.shape[2])
    x1_s, h_s, idx_s, wts_s = _merge(x_sample.reshape(ts, d), unpad(o_cmp_s), unpad(o_slc_s), unpad(o_win_s),
                                     unpad(o_sb_s), gate_s, br_s, prep, g_ffn2)

    h_rows = jnp.concatenate([h_p, h_s], axis=0)
    idx = jnp.concatenate([idx_p[:, :TOP_K], idx_s[:, :TOP_K]], axis=0)
    row_token, pos, tile_expert, n_used = _route_tables(idx)
    y_sorted = _experts(h_rows, row_token, tile_expert, n_used, w_gu[0], w_dn[0], bg, bu, bdn)
    y_p = _combine(x1_p, wts_p, pos[:tp], y_sorted, g_fin2)
    y_s = _combine(x1_s, wts_s, pos[tp:], y_sorted, g_fin2)

    win_all = jnp.concatenate([state_nsa_win[0], win_s.reshape(bd, ds, 2, NSA_KV, HEAD_DIM)], axis=1)
    lead = lambda a, n, h: a.reshape(1, n, -1, 2, h, HEAD_DIM)
    return (y_p.reshape(b, s, d), y_s.reshape(bd, ds, d),
            lead(cmp_p, b, NSA_KV), lead(slc_p, b, NSA_KV), lead(sbkv_p, b, SB_HEADS),
            lead(win_p, b, NSA_KV)[:, :, -wbuf:] if s >= wbuf else
            jnp.pad(lead(win_p, b, NSA_KV), ((0, 0), (0, 0), (wbuf - s, 0), (0, 0), (0, 0), (0, 0))),
            lead(cmp_s, bd, NSA_KV), lead(slc_s, bd, NSA_KV), lead(sbkv_s, bd, SB_HEADS),
            win_all[None, :, ds:])
```

```python
import functools

import numpy as np
import jax
import jax.numpy as jnp
from jax import lax
from jax.experimental import pallas as pl
from jax.experimental.pallas import tpu as pltpu

D_MODEL = 1024
HEAD_DIM = 64
NSA_HEADS = 8
NSA_KV = 2
NSA_REP = NSA_HEADS // NSA_KV
SB_HEADS = 8
NSA_W = NSA_HEADS * HEAD_DIM
SB_W = SB_HEADS * HEAD_DIM
KV_W = NSA_KV * HEAD_DIM
CMP_BLK = 32
CMP_STRIDE = 16
SEL_BLK = 64
TOP_N = 16
WINDOW = 512
FORCE_BONUS = 1.0e4
ROPE_THETA = 500000.0
ROPE_DIMS = HEAD_DIM // 4
N_EXPERTS = 32
TOP_K = 4
D_FF = D_MODEL
SWIGLU_LIMIT = 7.0
SWIGLU_ALPHA = 1.702
RMS_EPS = 1e-5
PAGE_SIZE = 128
SCALE = HEAD_DIM ** -0.5

LANES = 128
VMEM_LIMIT_BYTES = 56 * 1024 * 1024

BF16 = jnp.bfloat16
F32 = jnp.float32

_OFF = dict(q=0, k_cmp=512, v_cmp=640, k_slc=768, v_slc=896, k_win=1024, v_win=1152, br=1280,
            sb_q=1304, sb_k=1816, sb_v=2328, gate_a=2840, gate_b=3352, end=3864)
_HEAD_PERM = [j + 4 * half for j in range(4) for half in range(2)]
_C_Q, _C_SBQ, _C_CMP, _C_SLC, _C_WIN, _C_SBKV, _C_GATE, _C_BR, _NP = 0, 512, 1024, 1280, 1536, 1792, 2816, 3840, 3968


def _cparams(sem):
    return pltpu.CompilerParams(dimension_semantics=sem, vmem_limit_bytes=VMEM_LIMIT_BYTES)


def _prep_w_in(w):
    def head_perm(base):
        return [w[:, base + h * HEAD_DIM: base + (h + 1) * HEAD_DIM] for h in _HEAD_PERM]
    parts = [c * SCALE for c in head_perm(_OFF['q'])]
    parts.append(w[:, _OFF['sb_q']:_OFF['sb_k']] * SCALE)
    parts.append(w[:, _OFF['k_cmp']:_OFF['br']])
    parts.append(w[:, _OFF['sb_k']:_OFF['gate_a']])
    parts.extend(head_perm(_OFF['gate_a']))
    parts.append(w[:, _OFF['gate_b']:_OFF['end']])
    parts.append(w[:, _OFF['br']:_OFF['sb_q']])
    parts.append(jnp.zeros((w.shape[0], LANES - 3 * NSA_HEADS), w.dtype))
    return jnp.concatenate(parts, axis=1).astype(BF16)


def _rope_tables(pos):
    half = ROPE_DIMS // 2
    inv = ROPE_THETA ** (-jnp.arange(half, dtype=F32) / half)
    ang = pos.astype(F32)[:, None] * inv[None, :]
    cos = jnp.cos(ang)
    sin = jnp.sin(ang)
    n = pos.shape[0]
    one = jnp.ones((n, HEAD_DIM - ROPE_DIMS), F32)
    zero = jnp.zeros((n, HEAD_DIM - ROPE_DIMS), F32)
    z8 = jnp.zeros((n, half), F32)
    c = jnp.concatenate([cos, cos, one], axis=1)
    s1 = jnp.concatenate([-sin, z8, zero], axis=1)
    s2 = jnp.concatenate([z8, sin, zero], axis=1)
    tile2 = lambda t: jnp.concatenate([t, t], axis=1)
    return tile2(c), tile2(s1), tile2(s2)


def _rope_apply(x, c, s1, s2):
    outs = []
    for j in range(x.shape[1] // LANES):
        xb = x[:, j * LANES:(j + 1) * LANES]
        outs.append(xb * c + pltpu.roll(xb, LANES - 8, 1) * s1 + pltpu.roll(xb, 8, 1) * s2)
    return outs[0] if len(outs) == 1 else jnp.concatenate(outs, axis=1)


def _proj_kernel(x_ref, g_ref, w_ref, c_ref, s1_ref, s2_ref,
                 qc_ref, qr_ref, sbq_ref, cmp_ref, slc_ref, win_ref, sbkv_ref,
                 kslc_ref, vslc_ref, kwin_ref, vwin_ref, sbk_ref, sbv_ref, gate_ref, br_ref):
    x = x_ref[...]
    ms = jnp.mean(x * x, axis=-1, keepdims=True)
    h = (x * lax.rsqrt(ms + RMS_EPS)) * g_ref[...]
    hb = h.astype(BF16)
    c, s1, s2 = c_ref[...], s1_ref[...], s2_ref[...]

    def mm(c0, c1):
        return jnp.dot(hb, w_ref[:, c0:c1], preferred_element_type=F32)

    q = mm(_C_Q, _C_SBQ)
    qc_ref[...] = q.astype(BF16)
    qr_ref[...] = _rope_apply(q, c, s1, s2).astype(BF16)
    sbq_ref[...] = mm(_C_SBQ, _C_CMP).astype(BF16)
    cmp_ref[...] = mm(_C_CMP, _C_SLC)
    for c0, cache_ref, k_ref, v_ref in ((_C_SLC, slc_ref, kslc_ref, vslc_ref), (_C_WIN, win_ref, kwin_ref, vwin_ref)):
        z = mm(c0, c0 + 2 * KV_W)
        k = _rope_apply(z[:, :KV_W], c, s1, s2)
        v = z[:, KV_W:]
        cache_ref[:, :KV_W] = k
        cache_ref[:, KV_W:] = v
        k_ref[...] = k.astype(BF16)
        v_ref[...] = v.astype(BF16)
    z = mm(_C_SBKV, _C_GATE)
    sbkv_ref[...] = z
    sbk_ref[...] = z[:, :SB_W].astype(BF16)
    sbv_ref[...] = z[:, SB_W:].astype(BF16)
    gate_ref[...] = jax.nn.sigmoid(mm(_C_GATE, _C_BR)).astype(BF16)
    br_ref[...] = jax.nn.sigmoid(mm(_C_BR, _NP))


def _proj(x2d, g, w_perm, tabs, tm):
    t = x2d.shape[0]
    tab_blocks = tabs[0].shape[0] // tm
    assert t % tm == 0 and tabs[0].shape[0] % tm == 0
    row = lambda n: pl.BlockSpec((tm, n), lambda i: (i, 0))
    tab = pl.BlockSpec((tm, LANES), lambda i: (i % tab_blocks, 0))
    full = lambda a: pl.BlockSpec(a.shape, lambda i: (0,) * a.ndim)
    outs = [(512, BF16), (512, BF16), (512, BF16), (256, F32), (256, F32), (256, F32), (1024, F32),
            (128, BF16), (128, BF16), (128, BF16), (128, BF16), (512, BF16), (512, BF16), (1024, BF16), (128, F32)]
    return pl.pallas_call(
        _proj_kernel,
        grid=(t // tm,),
        in_specs=[row(D_MODEL), full(g), full(w_perm), tab, tab, tab],
        out_specs=[row(n) for n, _ in outs],
        out_shape=[jax.ShapeDtypeStruct((t, n), dt) for n, dt in outs],
        compiler_params=_cparams(("parallel",)),
        name="proj",
    )(x2d, g, w_perm, *tabs)


def _prep_w_cmp(w_ck, w_cv):
    eye2 = jnp.eye(NSA_KV, dtype=w_ck.dtype)
    bd = lambda w: jnp.einsum('ab,ide->iadbe', eye2, w).reshape(CMP_BLK, KV_W, KV_W)
    z = jnp.zeros((CMP_BLK, KV_W, KV_W), w_ck.dtype)
    w_all = jnp.concatenate([jnp.concatenate([bd(w_ck), z], axis=2),
                             jnp.concatenate([z, bd(w_cv)], axis=2)], axis=1)
    w_all = w_all.astype(BF16)
    return (w_all[:CMP_STRIDE].reshape(CMP_STRIDE * 2 * KV_W, 2 * KV_W),
            w_all[CMP_STRIDE:].reshape(CMP_STRIDE * 2 * KV_W, 2 * KV_W))


def _compress_rows(xb, w1, w2):
    n = xb.shape[0]
    y1 = jnp.dot(xb, w1, preferred_element_type=F32)
    y2 = jnp.dot(xb, w2, preferred_element_type=F32)
    y2s = pltpu.roll(y2, n - 1, 0)
    rid = lax.broadcasted_iota(jnp.int32, y2.shape, 0)
    return y1 + jnp.where(rid < n - 1, y2s, 0.0)


def _compress_kernel(x_ref, w1_ref, w2_ref, k_ref, v_ref):
    out = _compress_rows(x_ref[...].astype(BF16), w1_ref[...], w2_ref[...])
    k_ref[...] = out[:, :KV_W].astype(BF16)
    v_ref[...] = out[:, KV_W:].astype(BF16)


def _compress(cache2d, w1, w2, b, s):
    nc = s // CMP_STRIDE
    x = cache2d.reshape(b, nc, CMP_STRIDE * 2 * KV_W)
    full = lambda a: pl.BlockSpec(a.shape, lambda i: (0,) * a.ndim)
    blk = pl.BlockSpec((None, nc, KV_W), lambda i: (i, 0, 0))
    return pl.pallas_call(
        _compress_kernel,
        grid=(b,),
        in_specs=[pl.BlockSpec((None, nc, x.shape[2]), lambda i: (i, 0, 0)), full(w1), full(w2)],
        out_specs=[blk, blk],
        out_shape=[jax.ShapeDtypeStruct((b, nc, KV_W), BF16)] * 2,
        compiler_params=_cparams(("parallel",)),
        name="compress",
    )(x, w1, w2)


_NT = (((1,), (1,)), ((), ()))


def _dot_nt(a, b):
    return lax.dot_general(a, b, _NT, preferred_element_type=F32)


def _dot3(x, m, passes=3):
    d = lambda a: jnp.dot(a, m, preferred_element_type=F32)
    hi = x.astype(BF16)
    r1 = x - hi.astype(F32)
    mid = r1.astype(BF16)
    out = d(hi) + d(mid)
    if passes == 3:
        out = out + d((r1 - mid.astype(F32)).astype(BF16))
    return out


def _stack_group(ref, g, tq):
    half = lax.broadcasted_iota(jnp.int32, (tq, LANES), 1) >> 6
    return jnp.concatenate([jnp.where(half == g, ref[:, j * LANES:(j + 1) * LANES], 0)
                            for j in range(NSA_REP)], axis=0).astype(BF16)


def _unstack_pairs(o0, o1, tq):
    half = lax.broadcasted_iota(jnp.int32, (tq, LANES), 1) >> 6
    return jnp.concatenate([jnp.where(half == 0, o0[j * tq:(j + 1) * tq], o1[j * tq:(j + 1) * tq])
                            for j in range(NSA_REP)], axis=1)


def _imp_matrix(nc, nsel=None):
    m = np.zeros((nc, LANES), np.float32)
    for j in range(nc * CMP_STRIDE // SEL_BLK if nsel is None else nsel):
        for c, wgt in ((4 * j - 1, 0.5), (4 * j, 1.0), (4 * j + 1, 1.0), (4 * j + 2, 1.0), (4 * j + 3, 0.5)):
            if 0 <= c < nc:
                m[c, j] += wgt
    return jnp.asarray(m, BF16)


def _select_blocks(imp_sel, cur, nsel):
    tq = imp_sel.shape[0]
    blk = lax.broadcasted_iota(jnp.int32, (tq, LANES), 1)
    forced = (blk == 0) | (blk == cur) | (blk == cur - 1)
    score = jnp.where(blk <= cur, imp_sel + jnp.where(forced, FORCE_BONUS, 0.0), -1e9)
    st = score.T
    nslab = -(-nsel // 8)
    slabs = [st[8 * v:8 * v + 8] for v in range(nslab)]
    sub = lax.broadcasted_iota(jnp.int32, (8, tq), 0)
    cnts = [jnp.zeros((8, tq), jnp.int32) for _ in range(nslab)]
    for k in range(nsel):
        sk = st[k:k + 1, :]
        for v in range(nslab):
            if 8 * v + 7 < k:
                inc = jnp.where(sk > slabs[v], 1, 0)
            elif 8 * v > k:
                inc = jnp.where(sk >= slabs[v], 1, 0)
            else:
                inc = jnp.where(sub + 8 * v > k, jnp.where(sk >= slabs[v], 1, 0), jnp.where(sk > slabs[v], 1, 0))
            cnts[v] = cnts[v] + inc
    parts = [jnp.where((c < TOP_N) & (sub + 8 * v < nsel), 1.0, 0.0) for v, c in enumerate(cnts)]
    if 8 * nslab < LANES:
        parts.append(jnp.zeros((LANES - 8 * nslab, tq), F32))
    return jnp.concatenate(parts, axis=0).T.astype(BF16)


def _nsa_kernel(qc_ref, qr_ref, kc_ref, vc_ref, ks_ref, vs_ref, m_ref, ocmp_ref, oslc_ref, *, tq, kt):
    q0 = pl.program_id(1) * tq
    nc = kc_ref.shape[0]
    nsel = nc * CMP_STRIDE // SEL_BLK
    qpos = q0 + lax.broadcasted_iota(jnp.int32, (tq, 1), 0)
    tile4 = lambda a: jnp.concatenate([a] * NSA_REP, axis=0)
    cidx = lax.broadcasted_iota(jnp.int32, (tq, nc), 1)
    cbias = tile4(jnp.where(cidx * CMP_STRIDE + (CMP_BLK - 1) <= qpos, 0.0, -1e30))
    row_ok = tile4(jnp.where(qpos >= CMP_BLK - 1, 1.0, 0.0))
    kcol = lax.broadcasted_iota(jnp.int32, (tq, kt), 1)
    eblk = (lax.broadcasted_iota(jnp.int32, (LANES, kt), 0)
            - (lax.broadcasted_iota(jnp.int32, (LANES, kt), 1) >> 6))
    n_full = q0 // kt
    o_c, o_s = [], []
    for g in range(NSA_KV):
        qcg = _stack_group(qc_ref, g, tq)
        qrg = _stack_group(qr_ref, g, tq)
        s = _dot_nt(qcg, kc_ref[...]) + cbias
        e = jnp.exp(s - jnp.max(s, axis=-1, keepdims=True))
        p = e * (row_ok / jnp.maximum(jnp.sum(e, axis=-1, keepdims=True), 1e-30))
        o_c.append(jnp.dot(p.astype(BF16), vc_ref[...], preferred_element_type=F32))
        imp = p[0:tq] + p[tq:2 * tq] + p[2 * tq:3 * tq] + p[3 * tq:4 * tq]
        sel = _select_blocks(_dot3(imp, m_ref[...]), qpos >> 6, nsel)

        def tile_step(t, carry, causal):
            m, l, acc = carry
            k0 = pl.multiple_of(t * kt, kt)
            sc = _dot_nt(qrg, ks_ref[pl.ds(k0, kt), :])
            expand = jnp.where(eblk == t * (kt // SEL_BLK), 1.0, 0.0).astype(BF16)
            bias = (jnp.dot(sel, expand, preferred_element_type=F32) - 1.0) * 1e30
            if causal:
                bias = jnp.where(k0 + kcol <= qpos, bias, -1e30)
            sc = sc + tile4(bias)
            m_new = jnp.maximum(m, jnp.max(sc, axis=-1, keepdims=True))
            a = jnp.exp(sc - m_new)
            alpha = jnp.exp(m - m_new)
            l = alpha * l + jnp.sum(a, axis=-1, keepdims=True)
            acc = alpha * acc + jnp.dot(a.astype(BF16), vs_ref[pl.ds(k0, kt), :], preferred_element_type=F32)
            return m_new, l, acc

        init = (jnp.full((NSA_REP * tq, 1), -1e30, F32), jnp.zeros((NSA_REP * tq, 1), F32),
                jnp.zeros((NSA_REP * tq, LANES), F32))
        carry = lax.fori_loop(0, n_full, lambda t, c: tile_step(t, c, False), init)
        _, l, acc = tile_step(n_full, carry, True)
        o_s.append(acc / jnp.maximum(l, 1e-30))
    ocmp_ref[...] = _unstack_pairs(o_c[0], o_c[1], tq)
    oslc_ref[...] = _unstack_pairs(o_s[0], o_s[1], tq)


def _nsa(qc, qr, kc, vc, ks, vs, b, s, tq=128, kt=512):
    nqb = s // tq
    kt = min(kt, s)
    m = _imp_matrix(s // CMP_STRIDE)
    qspec = pl.BlockSpec((tq, NSA_W), lambda bi, i: (bi * nqb + i, 0))
    cspec = pl.BlockSpec((None, s // CMP_STRIDE, KV_W), lambda bi, i: (bi, 0, 0))
    kspec = pl.BlockSpec((s, KV_W), lambda bi, i: (bi, 0))
    return pl.pallas_call(
        functools.partial(_nsa_kernel, tq=tq, kt=kt),
        grid=(b, nqb),
        in_specs=[qspec, qspec, cspec, cspec, kspec, kspec, pl.BlockSpec(m.shape, lambda bi, i: (0, 0))],
        out_specs=[qspec, qspec],
        out_shape=[jax.ShapeDtypeStruct((b * s, NSA_W), F32)] * 2,
        compiler_params=_cparams(("parallel", "parallel")),
        name="nsa",
    )(qc, qr, kc, vc, ks, vs, m)


def _window_kernel(qr_ref, k_ref, v_ref, o_ref, *, tq, span):
    q0 = pl.program_id(1) * tq
    s_len = k_ref.shape[0]
    start = pl.multiple_of(jnp.minimum(jnp.maximum(q0 - WINDOW, 0), s_len - span), tq)
    qpos = q0 + lax.broadcasted_iota(jnp.int32, (tq, 1), 0)
    k = k_ref[pl.ds(start, span), :]
    v = v_ref[pl.ds(start, span), :]
    diff = qpos - (start + lax.broadcasted_iota(jnp.int32, (tq, span), 1))
    bias = jnp.where((diff >= 0) & (diff < WINDOW), 0.0, -1e30)
    bias = jnp.concatenate([bias] * NSA_REP, axis=0)
    outs = []
    for g in range(NSA_KV):
        qrg = _stack_group(qr_ref, g, tq)
        s = _dot_nt(qrg, k) + bias
        e = jnp.exp(s - jnp.max(s, axis=-1, keepdims=True))
        den = jnp.maximum(jnp.sum(e, axis=-1, keepdims=True), 1e-30)
        outs.append(jnp.dot(e.astype(BF16), v, preferred_element_type=F32) / den)
    o_ref[...] = _unstack_pairs(outs[0], outs[1], tq)


def _window(qr, kw, vw, b, s, tq=128):
    nqb = s // tq
    span = min(WINDOW + tq, s)
    qspec = pl.BlockSpec((tq, NSA_W), lambda bi, i: (bi * nqb + i, 0))
    kspec = pl.BlockSpec((s, KV_W), lambda bi, i: (bi, 0))
    return pl.pallas_call(
        functools.partial(_window_kernel, tq=tq, span=span),
        grid=(b, nqb),
        in_specs=[qspec, kspec, kspec],
        out_specs=qspec,
        out_shape=jax.ShapeDtypeStruct((b * s, NSA_W), F32),
        compiler_params=_cparams(("parallel", "parallel")),
        name="window",
    )(qr, kw, vw)


def _softplus(z):
    return jnp.maximum(z, 0.0) + jnp.log(1.0 + jnp.exp(-jnp.abs(z)))


def _tri_matrix(n):
    return jnp.asarray(np.tril(np.ones((n, n), np.float32)), BF16)


def _sb_two_blocks(qh, k_ref, v_ref, tri, ja, jb, state, mask_a, ok_b, tq):
    a0 = pl.multiple_of(ja * tq, tq)
    b0 = pl.multiple_of(jb * tq, tq)
    k_a, v_a = k_ref[pl.ds(a0, tq), :], v_ref[pl.ds(a0, tq), :]
    k_b, v_b = k_ref[pl.ds(b0, tq), :], v_ref[pl.ds(b0, tq), :]
    out = []
    for hh in range(2):
        carry, acc = state[2 * hh], state[2 * hh + 1]
        za = _dot_nt(qh[hh], k_a)
        zb = _dot_nt(qh[hh], k_b)
        spa = _softplus(za)
        spb = _softplus(zb)
        if mask_a is not None:
            spa = jnp.where(mask_a, spa, 0.0)
        if ok_b is not None:
            spb = jnp.where(ok_b, spb, 0.0)
        cuma = jnp.dot(spa.astype(BF16), tri, preferred_element_type=F32) + carry
        cumb = jnp.dot(spb.astype(BF16), tri, preferred_element_type=F32) + cuma[:, 0:1]
        aa = jnp.exp(za - cuma)
        ab = jnp.exp(zb - cumb)
        if mask_a is not None:
            aa = jnp.where(mask_a, aa, 0.0)
        if ok_b is not None:
            ab = jnp.where(ok_b, ab, 0.0)
        acc = (acc + jnp.dot(aa.astype(BF16), v_a, preferred_element_type=F32)
               + jnp.dot(ab.astype(BF16), v_b, preferred_element_type=F32))
        out.extend((cumb[:, 0:1], acc))
    return tuple(out)


def _sb_kernel(q_ref, k_ref, v_ref, tri_ref, o_ref, *, tq):
    i = pl.program_id(2)
    half = lax.broadcasted_iota(jnp.int32, (tq, LANES), 1) >> 6
    q = q_ref[...]
    qh = [jnp.where(half == hh, q, 0) for hh in range(2)]
    tri = tri_ref[...]
    odd = (i & 1) == 1
    causal = lax.broadcasted_iota(jnp.int32, (tq, tq), 1) < lax.broadcasted_iota(jnp.int32, (tq, tq), 0)
    zero = (jnp.zeros((tq, 1), F32), jnp.zeros((tq, LANES), F32))
    state = _sb_two_blocks(qh, k_ref, v_ref, tri, i, jnp.maximum(i - 1, 0), zero + zero, causal, odd, tq)
    first = i - 1 - (i & 1)

    def body(p, st):
        ja = first - 2 * p
        return _sb_two_blocks(qh, k_ref, v_ref, tri, ja, ja - 1, st, None, None, tq)

    st = lax.fori_loop(0, i >> 1, body, state)
    o_ref[...] = jnp.where(half == 0, st[1], st[3])


def _sb(sbq, sbk, sbv, b, s, tq=256):
    tq = min(tq, s)
    nqb = s // tq
    tri = _tri_matrix(tq)
    npair = SB_HEADS // 2
    qspec = pl.BlockSpec((tq, LANES), lambda bi, p, i: (bi * nqb + i, p))
    kspec = pl.BlockSpec((s, LANES), lambda bi, p, i: (bi, p))
    return pl.pallas_call(
        functools.partial(_sb_kernel, tq=tq),
        grid=(b, npair, nqb),
        in_specs=[qspec, kspec, kspec, pl.BlockSpec(tri.shape, lambda bi, p, i: (0, 0))],
        out_specs=qspec,
        out_shape=jax.ShapeDtypeStruct((b * s, SB_W), F32),
        compiler_params=_cparams(("parallel", "parallel", "parallel")),
        name="sb",
    )(sbq, sbk, sbv, tri)


def _prep_merge(w_out, w_router, b_router):
    w_a = [w_out[h * HEAD_DIM:(h + 1) * HEAD_DIM] for h in _HEAD_PERM]
    w_out_p = jnp.concatenate(w_a + [w_out[NSA_W:]], axis=0).astype(BF16)
    e = np.zeros((LANES, 3 * NSA_W), np.float32)
    for br in range(3):
        for p, h in enumerate(_HEAD_PERM):
            e[br * NSA_HEADS + h, br * NSA_W + p * HEAD_DIM: br * NSA_W + (p + 1) * HEAD_DIM] = 1.0
    w_r = jnp.concatenate([w_router, jnp.zeros((D_MODEL, LANES - N_EXPERTS), w_router.dtype)], axis=1)
    b_r = jnp.concatenate([b_router, jnp.full((LANES - N_EXPERTS,), -1e30, b_router.dtype)]).reshape(1, LANES)
    return w_out_p, jnp.asarray(e, BF16), w_r, b_r


def _merge_kernel(x_ref, oc_ref, os_ref, ow_ref, ob_ref, gate_ref, br_ref, e_ref, wo_ref, g_ref, wr_ref, brt_ref,
                  x1_ref, hp_ref, idx_ref, wts_ref):
    gexp = _dot3(br_ref[...], e_ref[...], passes=2)
    o_a = gexp[:, :NSA_W] * oc_ref[...] + gexp[:, NSA_W:2 * NSA_W] * os_ref[...] + gexp[:, 2 * NSA_W:] * ow_ref[...]
    gate = gate_ref[...].astype(F32)
    cat = jnp.concatenate([gate[:, :NSA_W] * o_a, gate[:, NSA_W:] * ob_ref[...]], axis=1).astype(BF16)
    x1 = x_ref[...] + jnp.dot(cat, wo_ref[...], preferred_element_type=F32)
    x1_ref[...] = x1
    ms = jnp.mean(x1 * x1, axis=-1, keepdims=True)
    h = (x1 * lax.rsqrt(ms + RMS_EPS)) * g_ref[...]
    hb = h.astype(BF16).astype(F32)
    for j in range(D_MODEL // LANES):
        hp_ref[:, j, :] = hb[:, j * LANES:(j + 1) * LANES]
    lg = jnp.dot(h, wr_ref[...], preferred_element_type=F32, precision=lax.Precision.HIGHEST) + brt_ref[...]
    lane = lax.broadcasted_iota(jnp.int32, lg.shape, 1)
    vals, ids = [], []
    for _ in range(TOP_K):
        mx = jnp.max(lg, axis=-1, keepdims=True)
        first = jnp.min(jnp.where(lg == mx, lane, LANES), axis=-1, keepdims=True)
        vals.append(mx)
        ids.append(first)
        lg = jnp.where(lane == first, -jnp.inf, lg)
    es = [jnp.exp(v - vals[0]) for v in vals]
    tot = es[0] + es[1] + es[2] + es[3]
    idx = jnp.zeros(lg.shape, jnp.int32)
    wts = jnp.zeros(lg.shape, F32)
    for k in range(TOP_K):
        idx = jnp.where(lane == k, ids[k], idx)
        wts = jnp.where(lane == k, es[k] / tot, wts)
    idx_ref[...] = idx
    wts_ref[...] = wts


def _merge(x2d, oc, os_, ow, ob, gate, br, prep, g_ffn, tm=256):
    t = x2d.shape[0]
    tm = min(tm, t)
    assert t % tm == 0
    w_out_p, e, w_r, b_r = prep
    row = lambda n: pl.BlockSpec((tm, n), lambda i: (i, 0))
    full = lambda a: pl.BlockSpec(a.shape, lambda i: (0,) * a.ndim)
    return pl.pallas_call(
        _merge_kernel,
        grid=(t // tm,),
        in_specs=[row(D_MODEL), row(NSA_W), row(NSA_W), row(NSA_W), row(SB_W), row(NSA_W + SB_W), row(LANES),
                  full(e), full(w_out_p), full(g_ffn), full(w_r), full(b_r)],
        out_specs=[row(D_MODEL), pl.BlockSpec((tm, ROW_SUB, LANES), lambda i: (i, 0, 0)), row(LANES), row(LANES)],
        out_shape=[jax.ShapeDtypeStruct((t, D_MODEL), F32), jax.ShapeDtypeStruct((t, ROW_SUB, LANES), F32),
                   jax.ShapeDtypeStruct((t, LANES), jnp.int32), jax.ShapeDtypeStruct((t, LANES), F32)],
        compiler_params=_cparams(("parallel",)),
        name="merge",
    )(x2d, oc, os_, ow, ob, gate, br, e, w_out_p, g_ffn, w_r, b_r)


MOE_TILE = 256
COMBINE_ROWS = 128
ROW_SUB = D_MODEL // LANES


def _route_tables(idx):
    t = idx.shape[0]
    a = t * TOP_K
    assert a % MOE_TILE == 0
    n_tiles = a // MOE_TILE + N_EXPERTS
    e = idx.reshape(a)
    onehot = (e[:, None] == jnp.arange(N_EXPERTS, dtype=jnp.int32)[None, :]).astype(jnp.int32)
    csum = jnp.cumsum(onehot, axis=0)
    rank = jnp.take_along_axis(csum, e[:, None], axis=1)[:, 0] - 1
    counts = csum[-1]
    tiles_e = (counts + MOE_TILE - 1) // MOE_TILE
    tile_end = jnp.cumsum(tiles_e)
    tile_start = tile_end - tiles_e
    dest = tile_start[e] * MOE_TILE + rank
    row_token = jnp.zeros((n_tiles * MOE_TILE,), jnp.int32).at[dest].set(jnp.arange(a, dtype=jnp.int32) // TOP_K)
    tile_ids = jnp.arange(n_tiles, dtype=jnp.int32)
    tile_expert = jnp.minimum(jnp.sum((tile_end[None, :] <= tile_ids[:, None]).astype(jnp.int32), axis=1),
                              N_EXPERTS - 1).astype(jnp.int32)
    return (row_token.reshape(n_tiles, 1, MOE_TILE), dest.reshape(t, TOP_K), tile_expert,
            tile_end[-1:].astype(jnp.int32))


def _deinterleave_matrix():
    m = np.zeros((2 * LANES, 2 * LANES), np.float32)
    m[2 * np.arange(LANES), np.arange(LANES)] = 1.0
    m[2 * np.arange(LANES) + 1, LANES + np.arange(LANES)] = 1.0
    return jnp.asarray(m, BF16)


def _row_tiles_to_2d(load):
    return jnp.concatenate([load(j) for j in range(ROW_SUB)], axis=1)


def _expert_kernel(te_ref, nu_ref, rt_ref, rtn_ref, h_ref, wgu_ref, wdn_ref, bg_ref, bu_ref, bd_ref, seo_ref,
                   y_ref, xbuf, sem, wg_s, wu_s, wd_s):
    i = pl.program_id(0)
    n_used = nu_ref[0]
    slot = i % 2

    def copy(idx_ref, r, s):
        return pltpu.make_async_copy(h_ref.at[pl.ds(idx_ref[0, r], 1)], xbuf.at[s, pl.ds(r, 1)], sem.at[s])

    def issue(idx_ref, s):
        def body(r, c):
            copy(idx_ref, r, s).start()
            return c
        lax.fori_loop(0, MOE_TILE, body, 0, unroll=8)

    def wait_rows():
        def wait(r, c):
            copy(rt_ref, r, slot).wait()
            return c
        lax.fori_loop(0, MOE_TILE, wait, 0, unroll=8)

    @pl.when(i == 0)
    def _():
        issue(rt_ref, 0)

    @pl.when(i + 1 < n_used)
    def _():
        issue(rtn_ref, 1 - slot)

    @pl.when(i < n_used)
    def _():
        @pl.when((i == 0) | (te_ref[i] != te_ref[jnp.maximum(i - 1, 0)]))
        def _():
            for k in range(D_FF // LANES):
                wblk = wgu_ref[:, 2 * LANES * k:2 * LANES * (k + 1)].astype(BF16)
                split = jnp.dot(wblk, seo_ref[...], preferred_element_type=F32).astype(BF16)
                wg_s[:, LANES * k:LANES * (k + 1)] = split[:, :LANES]
                wu_s[:, LANES * k:LANES * (k + 1)] = split[:, LANES:]
            wd_s[...] = wdn_ref[...].astype(BF16)

        wait_rows()
        x = _row_tiles_to_2d(lambda j: xbuf[slot, :, j, :]).astype(BF16)
        g = jnp.minimum(jnp.dot(x, wg_s[...], preferred_element_type=F32) + bg_ref[...], SWIGLU_LIMIT)
        u = jnp.clip(jnp.dot(x, wu_s[...], preferred_element_type=F32) + bu_ref[...], -SWIGLU_LIMIT, SWIGLU_LIMIT)
        act = (u + 1.0) * g * jax.nn.sigmoid(SWIGLU_ALPHA * g)
        y = jnp.dot(act.astype(BF16), wd_s[...], preferred_element_type=F32) + bd_ref[...]
        for j in range(ROW_SUB):
            y_ref[:, j, :] = y[:, j * LANES:(j + 1) * LANES]

    @pl.when(i >= n_used)
    def _():
        y_ref[...] = jnp.zeros(y_ref.shape, F32)


def _experts(h_rows, row_token, tile_expert, n_used, w_gu, w_dn, bg, bu, bd):
    n_tiles = row_token.shape[0]
    seo = _deinterleave_matrix()
    wspec = lambda shape: pl.BlockSpec((None,) + shape, lambda i, te, nu: (te[i], 0, 0))
    full = lambda a: pl.BlockSpec(a.shape, lambda i, te, nu: (0,) * a.ndim)
    rt_spec = lambda off: pl.BlockSpec((None, 1, MOE_TILE), lambda i, te, nu: (jnp.minimum(i + off, n_tiles - 1), 0, 0),
                                       memory_space=pltpu.SMEM)
    return pl.pallas_call(
        _expert_kernel,
        grid_spec=pltpu.PrefetchScalarGridSpec(
            num_scalar_prefetch=2,
            grid=(n_tiles,),
            in_specs=[rt_spec(0), rt_spec(1), pl.BlockSpec(memory_space=pl.ANY),
                      wspec((D_MODEL, 2 * D_FF)), wspec((D_FF, D_MODEL)), wspec((1, D_FF)), wspec((1, D_FF)),
                      wspec((1, D_MODEL)), full(seo)],
            out_specs=pl.BlockSpec((MOE_TILE, ROW_SUB, LANES), lambda i, te, nu: (i, 0, 0)),
            scratch_shapes=[pltpu.VMEM((2, MOE_TILE, ROW_SUB, LANES), F32), pltpu.SemaphoreType.DMA((2,)),
                            pltpu.VMEM((D_MODEL, D_FF), BF16), pltpu.VMEM((D_MODEL, D_FF), BF16),
                            pltpu.VMEM((D_FF, D_MODEL), BF16)],
        ),
        out_shape=jax.ShapeDtypeStruct((n_tiles * MOE_TILE, ROW_SUB, LANES), F32),
        compiler_params=_cparams(("arbitrary",)),
        name="moe_experts",
    )(tile_expert, n_used, row_token, row_token, h_rows, w_gu, w_dn, bg, bu, bd, seo)


def _combine_kernel(pos_ref, posn_ref, x_ref, wts_ref, g_ref, y_ref, out_ref, buf, sem, *, n):
    i = pl.program_id(0)
    n_rows = TOP_K * COMBINE_ROWS

    def copy(p_ref, r, slot):
        return pltpu.make_async_copy(y_ref.at[pl.ds(p_ref[0, r], 1)],
                                     buf.at[slot, r & (TOP_K - 1), pl.ds(r >> 2, 1)], sem.at[slot])

    def issue(p_ref, slot):
        def body(r2, c):
            copy(p_ref, 2 * r2, slot).start(priority=0)
            copy(p_ref, 2 * r2 + 1, slot).start(priority=1)
            return c
        lax.fori_loop(0, n_rows // 2, body, 0, unroll=4)

    slot = i % 2

    @pl.when(i == 0)
    def _():
        issue(pos_ref, 0)

    @pl.when(i + 1 < n)
    def _():
        issue(posn_ref, 1 - slot)

    def wait(r, c):
        copy(pos_ref, r, slot).wait()
        return c
    lax.fori_loop(0, n_rows, wait, 0, unroll=8)

    w = wts_ref[...]
    parts = []
    ssq = jnp.zeros((COMBINE_ROWS, 1), F32)
    for j in range(ROW_SUB):
        xs = x_ref[:, j * LANES:(j + 1) * LANES]
        for k in range(TOP_K):
            xs = xs + w[:, k:k + 1] * buf[slot, k, :, j, :]
        parts.append(xs)
        ssq = ssq + jnp.sum(xs * xs, axis=-1, keepdims=True)
    rs = lax.rsqrt(ssq * (1.0 / D_MODEL) + RMS_EPS)
    for j in range(ROW_SUB):
        out_ref[:, j * LANES:(j + 1) * LANES] = (parts[j] * rs) * g_ref[:, j * LANES:(j + 1) * LANES]


def _combine(x1, wts, pos, y_sorted, g_final):
    t = x1.shape[0]
    assert t % COMBINE_ROWS == 0
    steps = t // COMBINE_ROWS
    pos2 = pos.reshape(steps, 1, TOP_K * COMBINE_ROWS)
    row = lambda nn: pl.BlockSpec((COMBINE_ROWS, nn), lambda i: (i, 0))
    return pl.pallas_call(
        functools.partial(_combine_kernel, n=steps),
        grid=(steps,),
        in_specs=[pl.BlockSpec((None, 1, TOP_K * COMBINE_ROWS), lambda i: (i, 0, 0), memory_space=pltpu.SMEM),
                  pl.BlockSpec((None, 1, TOP_K * COMBINE_ROWS), lambda i: (jnp.minimum(i + 1, steps - 1), 0, 0),
                               memory_space=pltpu.SMEM),
                  row(D_MODEL), row(LANES), pl.BlockSpec(g_final.shape, lambda i: (0, 0)),
                  pl.BlockSpec(memory_space=pl.ANY)],
        out_specs=row(D_MODEL),
        out_shape=jax.ShapeDtypeStruct((t, D_MODEL), F32),
        scratch_shapes=[pltpu.VMEM((2, TOP_K, COMBINE_ROWS, ROW_SUB, LANES), F32), pltpu.SemaphoreType.DMA((2,))],
        compiler_params=_cparams(("arbitrary",)),
        name="moe_combine",
    )(pos2, pos2, x1, wts, g_final, y_sorted)


QPAD = 8


def _select_blocks_rows(imp_sel, cur, nsel):
    tq = imp_sel.shape[0]
    blk = lax.broadcasted_iota(jnp.int32, (tq, LANES), 1)
    forced = (blk == 0) | (blk == cur) | (blk == cur - 1)
    score = jnp.where(blk <= cur, imp_sel + jnp.where(forced, FORCE_BONUS, 0.0), -1e9)
    cnt = jnp.zeros((tq, LANES), jnp.int32)
    for k in range(nsel):
        sk = score[:, k:k + 1]
        beats = (sk > score) | ((sk == score) & (blk > k))
        cnt = cnt + jnp.where(beats, 1, 0)
    return jnp.where((cnt < TOP_N) & (blk < nsel), 1.0, 0.0)


def _pad_rows(x, n):
    return jnp.concatenate([x, jnp.zeros((n - x.shape[0], x.shape[1]), x.dtype)], axis=0)


NSA_S_SEQS = 4


def _nsa_s_kernel(pt_ref, qc_ref, qr_ref, xnew_ref, snew_ref, w1_ref, w2_ref, m_ref, *rest, n_pages, past, dec, seqs):
    ocmp_ref, oslc_ref = rest[2 * seqs * n_pages:]
    pages = [rest[2 * n_pages * s:2 * n_pages * (s + 1)] for s in range(seqs)]
    nc = n_pages * (PAGE_SIZE // CMP_STRIDE)
    xb = jnp.concatenate([p[...] for s in range(seqs) for p in pages[s][:n_pages]], axis=0).astype(BF16)
    y1 = jnp.dot(xb, w1_ref[...], preferred_element_type=F32)
    y2 = jnp.dot(xb, w2_ref[...], preferred_element_type=F32)
    xn = jnp.concatenate([xnew_ref[s] for s in range(seqs)] + [xnew_ref[0]] * (8 - seqs), axis=0)
    ynew = jnp.dot(xn.astype(BF16), w2_ref[...], preferred_element_type=F32)
    rid = lax.broadcasted_iota(jnp.int32, (nc, y2.shape[1]), 0)
    for s in range(seqs):
        y2s = pltpu.roll(y2[s * nc:(s + 1) * nc], nc - 1, 0)
        kcv = y1[s * nc:(s + 1) * nc] + jnp.where(rid < nc - 1, y2s, ynew[s:s + 1])
        _nsa_s_one(qc_ref.at[s], qr_ref.at[s], snew_ref.at[s], m_ref, kcv[:, :KV_W].astype(BF16),
                   kcv[:, KV_W:].astype(BF16), pages[s][n_pages:], ocmp_ref.at[s], oslc_ref.at[s], n_pages, past, dec)


def _nsa_s_one(qc_ref, qr_ref, snew_ref, m_ref, kc, vc, slc_pages, ocmp_ref, oslc_ref, n_pages, past, dec):
    tq = QPAD
    nsel = past // SEL_BLK + 1
    trow = lax.broadcasted_iota(jnp.int32, (NSA_REP * tq, 1), 0) & (tq - 1)
    rpos = past + jnp.minimum(trow, dec - 1)
    qpos = past + jnp.minimum(lax.broadcasted_iota(jnp.int32, (tq, 1), 0), dec - 1)
    lane = lax.broadcasted_iota(jnp.int32, (tq, LANES), 1)
    snew = snew_ref[...]
    k_all = jnp.concatenate([pg[:, :KV_W].astype(BF16) for pg in slc_pages]
                            + [_pad_rows(snew[:, :KV_W], LANES).astype(BF16)], axis=0)
    v_all = jnp.concatenate([pg[:, KV_W:].astype(BF16) for pg in slc_pages]
                            + [_pad_rows(snew[:, KV_W:], LANES).astype(BF16)], axis=0)
    o_c, o_s = [], []
    for g in range(NSA_KV):
        qcg = _stack_group(qc_ref, g, tq)
        qrg = _stack_group(qr_ref, g, tq)
        s = _dot_nt(qcg, kc)
        cidx = lax.broadcasted_iota(jnp.int32, s.shape, 1)
        cmask = cidx * CMP_STRIDE + (CMP_BLK - 1) <= rpos
        s = jnp.where(cmask, s, -1e30)
        e = jnp.where(cmask, jnp.exp(s - jnp.max(s, axis=-1, keepdims=True)), 0.0)
        p = e / jnp.maximum(jnp.sum(e, axis=-1, keepdims=True), 1e-30)
        o_c.append(jnp.dot(p.astype(BF16), vc, preferred_element_type=F32))
        imp = p[0:tq] + p[tq:2 * tq] + p[2 * tq:3 * tq] + p[3 * tq:4 * tq]
        sel = _select_blocks_rows(_dot3(imp, m_ref[...]), qpos >> 6, nsel)
        masks =[jnp.where(lane < SEL_BLK, sel[:, 2 * j:2 * j + 1], sel[:, 2 * j + 1:2 * j + 2]) > 0.5
                 for j in range(n_pages)]
        masks.append((sel[:, nsel - 1:nsel] > 0.5) & (lane < dec) & (past + lane <= qpos))
        valid = jnp.concatenate([jnp.concatenate(masks, axis=1)] * NSA_REP, axis=0)
        sc = jnp.where(valid, _dot_nt(qrg, k_all), -1e30)
        a = jnp.where(valid, jnp.exp(sc - jnp.max(sc, axis=-1, keepdims=True)), 0.0)
        l = jnp.sum(a, axis=-1, keepdims=True)
        o_s.append(jnp.dot(a.astype(BF16), v_all, preferred_element_type=F32) / jnp.maximum(l, 1e-30))
    ocmp_ref[...] = _unstack_pairs(o_c[0], o_c[1], tq)
    oslc_ref[...] = _unstack_pairs(o_s[0], o_s[1], tq)


def _nsa_s(page_table, qc, qr, xnew, snew, w1, w2, cache_cmp, cache_slc, past, dec):
    bd, n_pages = page_table.shape
    seqs = NSA_S_SEQS if bd % NSA_S_SEQS == 0 else 1
    cpp = PAGE_SIZE // CMP_STRIDE
    cmp_pages = cache_cmp.reshape(cache_cmp.shape[0], cpp, CMP_STRIDE * 2 * KV_W)
    m = _imp_matrix(n_pages * cpp, past // SEL_BLK + 1)
    per_b = lambda a: pl.BlockSpec((seqs,) + a.shape[1:], lambda b, pt: (b,) + (0,) * (a.ndim - 1))
    full = lambda a: pl.BlockSpec(a.shape, lambda b, pt: (0,) * a.ndim)
    page = lambda a, s, j: pl.BlockSpec((None,) + a.shape[1:], lambda b, pt: (pt[b * seqs + s, j], 0, 0))
    ospec = pl.BlockSpec((seqs, QPAD, NSA_W), lambda b, pt: (b, 0, 0))
    page_specs, page_args = [], []
    for s in range(seqs):
        page_specs += [page(cmp_pages, s, j) for j in range(n_pages)] + [page(cache_slc, s, j) for j in range(n_pages)]
        page_args += [cmp_pages] * n_pages + [cache_slc] * n_pages
    return pl.pallas_call(
        functools.partial(_nsa_s_kernel, n_pages=n_pages, past=past, dec=dec, seqs=seqs),
        grid_spec=pltpu.PrefetchScalarGridSpec(
            num_scalar_prefetch=1,
            grid=(bd // seqs,),
            in_specs=[per_b(qc), per_b(qr), per_b(xnew), per_b(snew), full(w1), full(w2), full(m)] + page_specs,
            out_specs=[ospec, ospec],
        ),
        out_shape=[jax.ShapeDtypeStruct((bd, QPAD, NSA_W), F32)] * 2,
        compiler_params=_cparams(("arbitrary",)),
        name="nsa_sample",
    )(page_table, qc, qr, xnew, snew, w1, w2, m, *page_args)


def _window_s_kernel(qr_ref, st_ref, wnew_ref, o_ref, *, past, dec):
    tq = QPAD
    wbuf = st_ref.shape[0]
    trow = jnp.minimum(lax.broadcasted_iota(jnp.int32, (NSA_REP * tq, 1), 0) & (tq - 1), dec - 1)
    k = st_ref[:, :KV_W].astype(BF16)
    v = st_ref[:, KV_W:].astype(BF16)
    wnew = wnew_ref[...]
    kn = _pad_rows(wnew[:, :KV_W], LANES).astype(BF16)
    vn = _pad_rows(wnew[:, KV_W:], LANES).astype(BF16)
    outs = []
    for g in range(NSA_KV):
        qrg = _stack_group(qr_ref, g, tq)
        s0 = _dot_nt(qrg, k)
        d0 = wbuf + trow - lax.broadcasted_iota(jnp.int32, s0.shape, 1)
        m0 = (d0 >= 0) & (d0 < WINDOW)
        s1 = _dot_nt(qrg, kn)
        i1 = lax.broadcasted_iota(jnp.int32, s1.shape, 1)
        m1 = (i1 < dec) & (trow - i1 >= 0) & (trow - i1 < WINDOW)
        s0 = jnp.where(m0, s0, -1e30)
        s1 = jnp.where(m1, s1, -1e30)
        mx = jnp.maximum(jnp.max(s0, axis=-1, keepdims=True), jnp.max(s1, axis=-1, keepdims=True))
        e0 = jnp.where(m0, jnp.exp(s0 - mx), 0.0)
        e1 = jnp.where(m1, jnp.exp(s1 - mx), 0.0)
        den = jnp.maximum(jnp.sum(e0, axis=-1, keepdims=True) + jnp.sum(e1, axis=-1, keepdims=True), 1e-30)
        o = (jnp.dot(e0.astype(BF16), v, preferred_element_type=F32)
             + jnp.dot(e1.astype(BF16), vn, preferred_element_type=F32))
        outs.append(o / den)
    o_ref[...] = _unstack_pairs(outs[0], outs[1], tq)


def _window_s(qr, state, wnew, past, dec):
    bd = qr.shape[0]
    per_b = lambda a: pl.BlockSpec((None,) + a.shape[1:], lambda b: (b,) + (0,) * (a.ndim - 1))
    return pl.pallas_call(
        functools.partial(_window_s_kernel, past=past, dec=dec),
        grid=(bd,),
        in_specs=[per_b(qr), per_b(state), per_b(wnew)],
        out_specs=pl.BlockSpec((None, QPAD, NSA_W), lambda b: (b, 0, 0)),
        out_shape=jax.ShapeDtypeStruct((bd, QPAD, NSA_W), F32),
        compiler_params=_cparams(("parallel",)),
        name="window_sample",
    )(qr, state, wnew)


def _sb_n_kernel(pt_ref, q_ref, new_ref, tri_ref, *rest, n_pages):
    pages = rest[:n_pages]
    o_ref = rest[n_pages]
    tq = QPAD
    nrow = SB_HEADS * tq
    rows_pp = PAGE_SIZE * SB_HEADS
    q = q_ref[...]
    new = new_ref[...]
    tri = tri_ref[...]
    hd = lambda x, h: x[:, h * HEAD_DIM:(h + 1) * HEAD_DIM]
    qm = jnp.concatenate([hd(q, h) for h in range(SB_HEADS)], axis=0).astype(BF16)
    k_all = jnp.concatenate([pg[:, 0].reshape(rows_pp, HEAD_DIM) for pg in pages], axis=0).astype(BF16)
    v_all = jnp.concatenate([pg[:, 1].reshape(rows_pp, HEAD_DIM) for pg in pages], axis=0).astype(BF16)
    s_all = _dot_nt(qm, k_all)
    nblk = n_pages * rows_pp // LANES
    blk = lambda x, j: x[j * nrow:(j + 1) * nrow]
    zs = jnp.concatenate([s_all[:, j * LANES:(j + 1) * LANES] for j in range(nblk)], axis=0)
    lane = lax.broadcasted_iota(jnp.int32, (nrow, LANES), 1)
    rowi = lax.broadcasted_iota(jnp.int32, (nrow, LANES), 0)
    own = jnp.where((lane & (SB_HEADS - 1)) == (rowi >> 3), 1.0, 0.0)
    own_all = jnp.concatenate([own] * nblk, axis=0) > 0.5
    col = lax.broadcasted_iota(jnp.int32, (nrow, SB_W), 1)
    rowd = lax.broadcasted_iota(jnp.int32, (nrow, SB_W), 0)
    ownd = (rowd >> 3) == (col >> 6)
    qbd = jnp.where(ownd, jnp.concatenate([q] * SB_HEADS, axis=0), 0.0).astype(BF16)
    kn = _pad_rows(new[:, :SB_W], LANES).astype(BF16)
    vn = _pad_rows(new[:, SB_W:], LANES).astype(BF16)
    zn = _dot_nt(qbd, kn)
    causal_new = lane < (rowi & (tq - 1))
    spn = jnp.where(causal_new, _softplus(zn), 0.0)
    pn = jnp.dot(spn.astype(BF16), tri, preferred_element_type=F32)
    an = jnp.where(causal_new, jnp.exp(zn - pn), 0.0)
    accn = jnp.where(ownd, jnp.dot(an.astype(BF16), vn, preferred_element_type=F32), 0.0)
    out = accn[0:tq]
    for h in range(1, SB_HEADS):
        out = out + accn[h * tq:(h + 1) * tq]
    sp = jnp.where(own_all, _softplus(zs), 0.0)
    p = jnp.dot(sp.astype(BF16), tri, preferred_element_type=F32)
    carry = pn[:, 0:1]
    carries = [None] * nblk
    for j in reversed(range(nblk)):
        carries[j] = jnp.broadcast_to(carry, (nrow, LANES))
        carry = carry + blk(p, j)[:, 0:1]
    a = jnp.where(own_all, jnp.exp(zs - (p + jnp.concatenate(carries, axis=0))), 0.0)
    a_wide = jnp.concatenate([blk(a, j) for j in range(nblk)], axis=1)
    acc = jnp.dot(a_wide.astype(BF16), v_all, preferred_element_type=F32)
    o_ref[...] = out + jnp.concatenate([acc[h * tq:(h + 1) * tq] for h in range(SB_HEADS)], axis=1)


def _sb_n(page_table, sbq, sbnew, cache_sb):
    bd, n_pages = page_table.shape
    tri = _tri_matrix(LANES)
    per_b = lambda a: pl.BlockSpec((None,) + a.shape[1:], lambda b, pt: (b,) + (0,) * (a.ndim - 1))
    page = lambda j: pl.BlockSpec((None,) + cache_sb.shape[1:], lambda b, pt: (pt[b, j], 0, 0, 0, 0))
    return pl.pallas_call(
        functools.partial(_sb_n_kernel, n_pages=n_pages),
        grid_spec=pltpu.PrefetchScalarGridSpec(
            num_scalar_prefetch=1,
            grid=(bd,),
            in_specs=[per_b(sbq), per_b(sbnew), pl.BlockSpec(tri.shape, lambda b, pt: (0, 0))]
            + [page(j) for j in range(n_pages)],
            out_specs=pl.BlockSpec((None, QPAD, SB_W), lambda b, pt: (b, 0, 0)),
        ),
        out_shape=jax.ShapeDtypeStruct((bd, QPAD, SB_W), F32),
        compiler_params=_cparams(("arbitrary",)),
        name="sb_sample",
    )(page_table, sbq, sbnew, tri, *([cache_sb] * n_pages))


def kernel(x_prompt, x_sample, cache_nsa_cmp, cache_nsa_slc, cache_sb, state_nsa_win, page_table, g_mix, w_in, w_ck,
           w_cv, w_out, g_ffn, w_router, b_router, w_gu, b_gu, w_dn, b_dn, g_final):
    b, s, d = x_prompt.shape
    bd, ds, _ = x_sample.shape
    depth = g_mix.shape[0]
    assert depth == 1 and d == D_MODEL and ds <= QPAD
    n_pages = page_table.shape[1]
    past = n_pages * PAGE_SIZE
    wbuf = state_nsa_win.shape[2]
    n_phys = cache_nsa_cmp.shape[1]
    tp, ts = b * s, bd * ds

    w_perm = _prep_w_in(w_in[0])
    w1, w2 = _prep_w_cmp(w_ck[0], w_cv[0])
    prep = _prep_merge(w_out[0], w_router[0], b_router[0])
    g_mix2, g_ffn2, g_fin2 = g_mix[0].reshape(1, d), g_ffn[0].reshape(1, d), g_final.reshape(1, d)
    bg = b_gu[0][:, None, 0::2]
    bu = b_gu[0][:, None, 1::2]
    bdn = b_dn[0][:, None, :]

    pos_p = jnp.arange(s, dtype=jnp.int32)
    (qc, qr, sbq, cmp_p, slc_p, win_p, sbkv_p, kslc, vslc, kwin, vwin, sbk, sbv, gate_p, br_p) = _proj(
        x_prompt.reshape(tp, d), g_mix2, w_perm, _rope_tables(pos_p), 256)
    kc, vc = _compress(cmp_p, w1, w2, b, s)
    o_cmp, o_slc = _nsa(qc, qr, kc, vc, kslc, vslc, b, s)
    o_win = _window(qr, kwin, vwin, b, s)
    o_sb = _sb(sbq, sbk, sbv, b, s)
    x1_p, h_p, idx_p, wts_p = _merge(x_prompt.reshape(tp, d), o_cmp, o_slc, o_win, o_sb, gate_p, br_p, prep, g_ffn2)

    pos_s = past + jnp.arange(ds, dtype=jnp.int32)
    tabs_s = [jnp.tile(t, (bd, 1)) for t in _rope_tables(pos_s)]
    (qc_s, qr_s, sbq_s, cmp_s, slc_s, win_s, sbkv_s, _, _, _, _, _, _, gate_s, br_s) = _proj(
        x_sample.reshape(ts, d), g_mix2, w_perm, tabs_s, min(256, ts))
    pad_q = lambda a: jnp.pad(a.astype(F32).reshape(bd, ds, a.shape[1]), ((0, 0), (0, QPAD - ds), (0, 0)))
    xnew = jnp.pad(cmp_s.reshape(bd, 1, ds * 2 * KV_W), ((0, 0), (0, 0), (0, (CMP_STRIDE - ds) * 2 * KV_W)))
    o_cmp_s, o_slc_s = _nsa_s(page_table, pad_q(qc_s), pad_q(qr_s), xnew, pad_q(slc_s), w1, w2,
                              cache_nsa_cmp[0].reshape(n_phys, PAGE_SIZE, 2 * KV_W),
                              cache_nsa_slc[0].reshape(n_phys, PAGE_SIZE, 2 * KV_W), past, ds)
    o_win_s = _window_s(pad_q(qr_s), state_nsa_win[0].reshape(bd, wbuf, 2 * KV_W), pad_q(win_s), past, ds)
    o_sb_s = _sb_n(page_table, pad_q(sbq_s), pad_q(sbkv_s), cache_sb[0])
    unpad = lambda a: a[:, :ds].reshape(ts, a.shape[2])
    x1_s, h_s, idx_s, wts_s = _merge(x_sample.reshape(ts, d), unpad(o_cmp_s), unpad(o_slc_s), unpad(o_win_s),
                                     unpad(o_sb_s), gate_s, br_s, prep, g_ffn2)

    h_rows = jnp.concatenate([h_p, h_s], axis=0)
    idx = jnp.concatenate([idx_p[:, :TOP_K], idx_s[:, :TOP_K]], axis=0)
    row_token, pos, tile_expert, n_used = _route_tables(idx)
    y_sorted = _experts(h_rows, row_token, tile_expert, n_used, w_gu[0], w_dn[0], bg, bu, bdn)
    y_p = _combine(x1_p, wts_p, pos[:tp], y_sorted, g_fin2)
    y_s = _combine(x1_s, wts_s, pos[tp:], y_sorted, g_fin2)

    win_all = jnp.concatenate([state_nsa_win[0], win_s.reshape(bd, ds, 2, NSA_KV, HEAD_DIM)], axis=1)
    lead = lambda a, n, h: a.reshape(1, n, -1, 2, h, HEAD_DIM)
    return (y_p.reshape(b, s, d), y_s.reshape(bd, ds, d),
            lead(cmp_p, b, NSA_KV), lead(slc_p, b, NSA_KV), lead(sbkv_p, b, SB_HEADS),
            lead(win_p, b, NSA_KV)[:, :, -wbuf:] if s >= wbuf else
            jnp.pad(lead(win_p, b, NSA_KV), ((0, 0), (0, 0), (wbuf - s, 0), (0, 0), (0, 0), (0, 0))),
            lead(cmp_s, bd, NSA_KV), lead(slc_s, bd, NSA_KV), lead(sbkv_s, bd, SB_HEADS),
            win_all[None, :, ds:])
```

```python
import functools

import numpy as np
import jax
import jax.numpy as jnp
from jax import lax
from jax.experimental import pallas as pl
from jax.experimental.pallas import tpu as pltpu

D_MODEL = 1024
HEAD_DIM = 64
NSA_HEADS = 8
NSA_KV = 2
NSA_REP = NSA_HEADS // NSA_KV
SB_HEADS = 8
NSA_W = NSA_HEADS * HEAD_DIM
SB_W = SB_HEADS * HEAD_DIM
KV_W = NSA_KV * HEAD_DIM
CMP_BLK = 32
CMP_STRIDE = 16
SEL_BLK = 64
TOP_N = 16
WINDOW = 512
FORCE_BONUS = 1.0e4
ROPE_THETA = 500000.0
ROPE_DIMS = HEAD_DIM // 4
N_EXPERTS = 32
TOP_K = 4
D_FF = D_MODEL
SWIGLU_LIMIT = 7.0
SWIGLU_ALPHA = 1.702
RMS_EPS = 1e-5
PAGE_SIZE = 128
SCALE = HEAD_DIM ** -0.5

LANES = 128
VMEM_LIMIT_BYTES = 56 * 1024 * 1024

BF16 = jnp.bfloat16
F32 = jnp.float32

_OFF = dict(q=0, k_cmp=512, v_cmp=640, k_slc=768, v_slc=896, k_win=1024, v_win=1152, br=1280,
            sb_q=1304, sb_k=1816, sb_v=2328, gate_a=2840, gate_b=3352, end=3864)
_HEAD_PERM = [j + 4 * half for j in range(4) for half in range(2)]
_C_Q, _C_SBQ, _C_CMP, _C_SLC, _C_WIN, _C_SBKV, _C_GATE, _C_BR, _NP = 0, 512, 1024, 1280, 1536, 1792, 2816, 3840, 3968


def _cparams(sem):
    return pltpu.CompilerParams(dimension_semantics=sem, vmem_limit_bytes=VMEM_LIMIT_BYTES)


def _prep_w_in(w):
    def head_perm(base):
        return [w[:, base + h * HEAD_DIM: base + (h + 1) * HEAD_DIM] for h in _HEAD_PERM]
    parts = [c * SCALE for c in head_perm(_OFF['q'])]
    parts.append(w[:, _OFF['sb_q']:_OFF['sb_k']] * SCALE)
    parts.append(w[:, _OFF['k_cmp']:_OFF['br']])
    parts.append(w[:, _OFF['sb_k']:_OFF['gate_a']])
    parts.extend(head_perm(_OFF['gate_a']))
    parts.append(w[:, _OFF['gate_b']:_OFF['end']])
    parts.append(w[:, _OFF['br']:_OFF['sb_q']])
    parts.append(jnp.zeros((w.shape[0], LANES - 3 * NSA_HEADS), w.dtype))
    return jnp.concatenate(parts, axis=1).astype(BF16)


def _rope_tables(pos):
    half = ROPE_DIMS // 2
    inv = ROPE_THETA ** (-jnp.arange(half, dtype=F32) / half)
    ang = pos.astype(F32)[:, None] * inv[None, :]
    cos = jnp.cos(ang)
    sin = jnp.sin(ang)
    n = pos.shape[0]
    one = jnp.ones((n, HEAD_DIM - ROPE_DIMS), F32)
    zero = jnp.zeros((n, HEAD_DIM - ROPE_DIMS), F32)
    z8 = jnp.zeros((n, half), F32)
    c = jnp.concatenate([cos, cos, one], axis=1)
    s1 = jnp.concatenate([-sin, z8, zero], axis=1)
    s2 = jnp.concatenate([z8, sin, zero], axis=1)
    tile2 = lambda t: jnp.concatenate([t, t], axis=1)
    return tile2(c), tile2(s1), tile2(s2)


def _rope_apply(x, c, s1, s2):
    outs = []
    for j in range(x.shape[1] // LANES):
        xb = x[:, j * LANES:(j + 1) * LANES]
        outs.append(xb * c + pltpu.roll(xb, LANES - 8, 1) * s1 + pltpu.roll(xb, 8, 1) * s2)
    return outs[0] if len(outs) == 1 else jnp.concatenate(outs, axis=1)


def _proj_kernel(x_ref, g_ref, w_ref, c_ref, s1_ref, s2_ref,
                 qc_ref, qr_ref, sbq_ref, cmp_ref, slc_ref, win_ref, sbkv_ref,
                 kslc_ref, vslc_ref, kwin_ref, vwin_ref, sbk_ref, sbv_ref, gate_ref, br_ref):
    x = x_ref[...]
    ms = jnp.mean(x * x, axis=-1, keepdims=True)
    h = (x * lax.rsqrt(ms + RMS_EPS)) * g_ref[...]
    hb = h.astype(BF16)
    c, s1, s2 = c_ref[...], s1_ref[...], s2_ref[...]

    def mm(c0, c1):
        return jnp.dot(hb, w_ref[:, c0:c1], preferred_element_type=F32)

    q = mm(_C_Q, _C_SBQ)
    qc_ref[...] = q.astype(BF16)
    qr_ref[...] = _rope_apply(q, c, s1, s2).astype(BF16)
    sbq_ref[...] = mm(_C_SBQ, _C_CMP).astype(BF16)
    cmp_ref[...] = mm(_C_CMP, _C_SLC)
    for c0, cache_ref, k_ref, v_ref in ((_C_SLC, slc_ref, kslc_ref, vslc_ref), (_C_WIN, win_ref, kwin_ref, vwin_ref)):
        z = mm(c0, c0 + 2 * KV_W)
        k = _rope_apply(z[:, :KV_W], c, s1, s2)
        v = z[:, KV_W:]
        cache_ref[:, :KV_W] = k
        cache_ref[:, KV_W:] = v
        k_ref[...] = k.astype(BF16)
        v_ref[...] = v.astype(BF16)
    z = mm(_C_SBKV, _C_GATE)
    sbkv_ref[...] = z
    sbk_ref[...] = z[:, :SB_W].astype(BF16)
    sbv_ref[...] = z[:, SB_W:].astype(BF16)
    gate_ref[...] = jax.nn.sigmoid(mm(_C_GATE, _C_BR)).astype(BF16)
    br_ref[...] = jax.nn.sigmoid(mm(_C_BR, _NP))


def _proj(x2d, g, w_perm, tabs, tm):
    t = x2d.shape[0]
    tab_blocks = tabs[0].shape[0] // tm
    assert t % tm == 0 and tabs[0].shape[0] % tm == 0
    row = lambda n: pl.BlockSpec((tm, n), lambda i: (i, 0))
    tab = pl.BlockSpec((tm, LANES), lambda i: (i % tab_blocks, 0))
    full = lambda a: pl.BlockSpec(a.shape, lambda i: (0,) * a.ndim)
    outs = [(512, BF16), (512, BF16), (512, BF16), (256, F32), (256, F32), (256, F32), (1024, F32),
            (128, BF16), (128, BF16), (128, BF16), (128, BF16), (512, BF16), (512, BF16), (1024, BF16), (128, F32)]
    return pl.pallas_call(
        _proj_kernel,
        grid=(t // tm,),
        in_specs=[row(D_MODEL), full(g), full(w_perm), tab, tab, tab],
        out_specs=[row(n) for n, _ in outs],
        out_shape=[jax.ShapeDtypeStruct((t, n), dt) for n, dt in outs],
        compiler_params=_cparams(("parallel",)),
        name="proj",
    )(x2d, g, w_perm, *tabs)


def _prep_w_cmp(w_ck, w_cv):
    eye2 = jnp.eye(NSA_KV, dtype=w_ck.dtype)
    bd = lambda w: jnp.einsum('ab,ide->iadbe', eye2, w).reshape(CMP_BLK, KV_W, KV_W)
    z = jnp.zeros((CMP_BLK, KV_W, KV_W), w_ck.dtype)
    w_all = jnp.concatenate([jnp.concatenate([bd(w_ck), z], axis=2),
                             jnp.concatenate([z, bd(w_cv)], axis=2)], axis=1)
    w_all = w_all.astype(BF16)
    return (w_all[:CMP_STRIDE].reshape(CMP_STRIDE * 2 * KV_W, 2 * KV_W),
            w_all[CMP_STRIDE:].reshape(CMP_STRIDE * 2 * KV_W, 2 * KV_W))


def _compress_rows(xb, w1, w2):
    n = xb.shape[0]
    y1 = jnp.dot(xb, w1, preferred_element_type=F32)
    y2 = jnp.dot(xb, w2, preferred_element_type=F32)
    y2s = pltpu.roll(y2, n - 1, 0)
    rid = lax.broadcasted_iota(jnp.int32, y2.shape, 0)
    return y1 + jnp.where(rid < n - 1, y2s, 0.0)


def _compress_kernel(x_ref, w1_ref, w2_ref, k_ref, v_ref):
    out = _compress_rows(x_ref[...].astype(BF16), w1_ref[...], w2_ref[...])
    k_ref[...] = out[:, :KV_W].astype(BF16)
    v_ref[...] = out[:, KV_W:].astype(BF16)


def _compress(cache2d, w1, w2, b, s):
    nc = s // CMP_STRIDE
    x = cache2d.reshape(b, nc, CMP_STRIDE * 2 * KV_W)
    full = lambda a: pl.BlockSpec(a.shape, lambda i: (0,) * a.ndim)
    blk = pl.BlockSpec((None, nc, KV_W), lambda i: (i, 0, 0))
    return pl.pallas_call(
        _compress_kernel,
        grid=(b,),
        in_specs=[pl.BlockSpec((None, nc, x.shape[2]), lambda i: (i, 0, 0)), full(w1), full(w2)],
        out_specs=[blk, blk],
        out_shape=[jax.ShapeDtypeStruct((b, nc, KV_W), BF16)] * 2,
        compiler_params=_cparams(("parallel",)),
        name="compress",
    )(x, w1, w2)


_NT = (((1,), (1,)), ((), ()))


def _dot_nt(a, b):
    return lax.dot_general(a, b, _NT, preferred_element_type=F32)


def _dot3(x, m, passes=3):
    d = lambda a: jnp.dot(a, m, preferred_element_type=F32)
    hi = x.astype(BF16)
    r1 = x - hi.astype(F32)
    mid = r1.astype(BF16)
    out = d(hi) + d(mid)
    if passes == 3:
        out = out + d((r1 - mid.astype(F32)).astype(BF16))
    return out


def _stack_group(ref, g, tq):
    half = lax.broadcasted_iota(jnp.int32, (tq, LANES), 1) >> 6
    return jnp.concatenate([jnp.where(half == g, ref[:, j * LANES:(j + 1) * LANES], 0)
                            for j in range(NSA_REP)], axis=0).astype(BF16)


def _unstack_pairs(o0, o1, tq):
    half = lax.broadcasted_iota(jnp.int32, (tq, LANES), 1) >> 6
    return jnp.concatenate([jnp.where(half == 0, o0[j * tq:(j + 1) * tq], o1[j * tq:(j + 1) * tq])
                            for j in range(NSA_REP)], axis=1)


def _imp_matrix(nc, nsel=None):
    m = np.zeros((nc, LANES), np.float32)
    for j in range(nc * CMP_STRIDE // SEL_BLK if nsel is None else nsel):
        for c, wgt in ((4 * j - 1, 0.5), (4 * j, 1.0), (4 * j + 1, 1.0), (4 * j + 2, 1.0), (4 * j + 3, 0.5)):
            if 0 <= c < nc:
                m[c, j] += wgt
    return jnp.asarray(m, BF16)


def _select_blocks(imp_sel, cur, nsel):
    tq = imp_sel.shape[0]
    blk = lax.broadcasted_iota(jnp.int32, (tq, LANES), 1)
    forced = (blk == 0) | (blk == cur) | (blk == cur - 1)
    score = jnp.where(blk <= cur, imp_sel + jnp.where(forced, FORCE_BONUS, 0.0), -1e9)
    st = score.T
    nslab = -(-nsel // 8)
    slabs = [st[8 * v:8 * v + 8] for v in range(nslab)]
    sub = lax.broadcasted_iota(jnp.int32, (8, tq), 0)
    cnts = [jnp.zeros((8, tq), jnp.int32) for _ in range(nslab)]
    for k in range(nsel):
        sk = st[k:k + 1, :]
        for v in range(nslab):
            if 8 * v + 7 < k:
                inc = jnp.where(sk > slabs[v], 1, 0)
            elif 8 * v > k:
                inc = jnp.where(sk >= slabs[v], 1, 0)
            else:
                inc = jnp.where(sub + 8 * v > k, jnp.where(sk >= slabs[v], 1, 0), jnp.where(sk > slabs[v], 1, 0))
            cnts[v] = cnts[v] + inc
    parts = [jnp.where((c < TOP_N) & (sub + 8 * v < nsel), 1.0, 0.0) for v, c in enumerate(cnts)]
    if 8 * nslab < LANES:
        parts.append(jnp.zeros((LANES - 8 * nslab, tq), F32))
    return jnp.concatenate(parts, axis=0).T.astype(BF16)


def _nsa_kernel(qc_ref, qr_ref, kc_ref, vc_ref, ks_ref, vs_ref, m_ref, ocmp_ref, oslc_ref, *, tq, kt):
    q0 = pl.program_id(1) * tq
    nc = kc_ref.shape[0]
    nsel = nc * CMP_STRIDE // SEL_BLK
    qpos = q0 + lax.broadcasted_iota(jnp.int32, (tq, 1), 0)
    tile4 = lambda a: jnp.concatenate([a] * NSA_REP, axis=0)
    cidx = lax.broadcasted_iota(jnp.int32, (tq, nc), 1)
    cbias = tile4(jnp.where(cidx * CMP_STRIDE + (CMP_BLK - 1) <= qpos, 0.0, -1e30))
    row_ok = tile4(jnp.where(qpos >= CMP_BLK - 1, 1.0, 0.0))
    kcol = lax.broadcasted_iota(jnp.int32, (tq, kt), 1)
    eblk = (lax.broadcasted_iota(jnp.int32, (LANES, kt), 0)
            - (lax.broadcasted_iota(jnp.int32, (LANES, kt), 1) >> 6))
    n_full = q0 // kt
    o_c, o_s = [], []
    for g in range(NSA_KV):
        qcg = _stack_group(qc_ref, g, tq)
        qrg = _stack_group(qr_ref, g, tq)
        s = _dot_nt(qcg, kc_ref[...]) + cbias
        e = jnp.exp(s - jnp.max(s, axis=-1, keepdims=True))
        p = e * (row_ok / jnp.maximum(jnp.sum(e, axis=-1, keepdims=True), 1e-30))
        o_c.append(jnp.dot(p.astype(BF16), vc_ref[...], preferred_element_type=F32))
        imp = p[0:tq] + p[tq:2 * tq] + p[2 * tq:3 * tq] + p[3 * tq:4 * tq]
        sel = _select_blocks(_dot3(imp, m_ref[...]), qpos >> 6, nsel)

        def tile_step(t, carry, causal):
            m, l, acc = carry
            k0 = pl.multiple_of(t * kt, kt)
            sc = _dot_nt(qrg, ks_ref[pl.ds(k0, kt), :])
            expand = jnp.where(eblk == t * (kt // SEL_BLK), 1.0, 0.0).astype(BF16)
            bias = (jnp.dot(sel, expand, preferred_element_type=F32) - 1.0) * 1e30
            if causal:
                bias = jnp.where(k0 + kcol <= qpos, bias, -1e30)
            sc = sc + tile4(bias)
            m_new = jnp.maximum(m, jnp.max(sc, axis=-1, keepdims=True))
            a = jnp.exp(sc - m_new)
            alpha = jnp.exp(m - m_new)
            l = alpha * l + jnp.sum(a, axis=-1, keepdims=True)
            acc = alpha * acc + jnp.dot(a.astype(BF16), vs_ref[pl.ds(k0, kt), :], preferred_element_type=F32)
            return m_new, l, acc

        init = (jnp.full((NSA_REP * tq, 1), -1e30, F32), jnp.zeros((NSA_REP * tq, 1), F32),
                jnp.zeros((NSA_REP * tq, LANES), F32))
        carry = lax.fori_loop(0, n_full, lambda t, c: tile_step(t, c, False), init)
        _, l, acc = tile_step(n_full, carry, True)
        o_s.append(acc / jnp.maximum(l, 1e-30))
    ocmp_ref[...] = _unstack_pairs(o_c[0], o_c[1], tq)
    oslc_ref[...] = _unstack_pairs(o_s[0], o_s[1], tq)


def _nsa(qc, qr, kc, vc, ks, vs, b, s, tq=128, kt=512):
    nqb = s // tq
    kt = min(kt, s)
    m = _imp_matrix(s // CMP_STRIDE)
    qspec = pl.BlockSpec((tq, NSA_W), lambda bi, i: (bi * nqb + i, 0))
    cspec = pl.BlockSpec((None, s // CMP_STRIDE, KV_W), lambda bi, i: (bi, 0, 0))
    kspec = pl.BlockSpec((s, KV_W), lambda bi, i: (bi, 0))
    return pl.pallas_call(
        functools.partial(_nsa_kernel, tq=tq, kt=kt),
        grid=(b, nqb),
        in_specs=[qspec, qspec, cspec, cspec, kspec, kspec, pl.BlockSpec(m.shape, lambda bi, i: (0, 0))],
        out_specs=[qspec, qspec],
        out_shape=[jax.ShapeDtypeStruct((b * s, NSA_W), F32)] * 2,
        compiler_params=_cparams(("parallel", "parallel")),
        name="nsa",
    )(qc, qr, kc, vc, ks, vs, m)


def _window_kernel(qr_ref, k_ref, v_ref, o_ref, *, tq, span):
    q0 = pl.program_id(1) * tq
    s_len = k_ref.shape[0]
    start = pl.multiple_of(jnp.minimum(jnp.maximum(q0 - WINDOW, 0), s_len - span), tq)
    qpos = q0 + lax.broadcasted_iota(jnp.int32, (tq, 1), 0)
    k = k_ref[pl.ds(start, span), :]
    v = v_ref[pl.ds(start, span), :]
    diff = qpos - (start + lax.broadcasted_iota(jnp.int32, (tq, span), 1))
    bias = jnp.where((diff >= 0) & (diff < WINDOW), 0.0, -1e30)
    bias = jnp.concatenate([bias] * NSA_REP, axis=0)
    outs = []
    for g in range(NSA_KV):
        qrg = _stack_group(qr_ref, g, tq)
        s = _dot_nt(qrg, k) + bias
        e = jnp.exp(s - jnp.max(s, axis=-1, keepdims=True))
        den = jnp.maximum(jnp.sum(e, axis=-1, keepdims=True), 1e-30)
        outs.append(jnp.dot(e.astype(BF16), v, preferred_element_type=F32) / den)
    o_ref[...] = _unstack_pairs(outs[0], outs[1], tq)


def _window(qr, kw, vw, b, s, tq=128):
    nqb = s // tq
    span = min(WINDOW + tq, s)
    qspec = pl.BlockSpec((tq, NSA_W), lambda bi, i: (bi * nqb + i, 0))
    kspec = pl.BlockSpec((s, KV_W), lambda bi, i: (bi, 0))
    return pl.pallas_call(
        functools.partial(_window_kernel, tq=tq, span=span),
        grid=(b, nqb),
        in_specs=[qspec, kspec, kspec],
        out_specs=qspec,
        out_shape=jax.ShapeDtypeStruct((b * s, NSA_W), F32),
        compiler_params=_cparams(("parallel", "parallel")),
        name="window",
    )(qr, kw, vw)


def _softplus(z):
    return jnp.maximum(z, 0.0) + jnp.log(1.0 + jnp.exp(-jnp.abs(z)))


def _tri_matrix(n):
    return jnp.asarray(np.tril(np.ones((n, n), np.float32)), BF16)


def _sb_block(qh, k_t, v_t, tri, carry, acc, valid):
    z = _dot_nt(qh, k_t)
    sp = _softplus(z)
    if valid is not None:
        sp = jnp.where(valid, sp, 0.0)
    cum = jnp.dot(sp.astype(BF16), tri, preferred_element_type=F32) + carry
    a = jnp.exp(z - cum)
    if valid is not None:
        a = jnp.where(valid, a, 0.0)
    acc = acc + jnp.dot(a.astype(BF16), v_t, preferred_element_type=F32)
    return cum[:, 0:1], acc


def _sb_blocks(qh, k_ref, v_ref, tri, js, state, mask0, ok_last, tq):
    n = len(js)
    offs = [pl.multiple_of(j * tq, tq) for j in js]
    ks = [k_ref[pl.ds(o, tq), :] for o in offs]
    vs = [v_ref[pl.ds(o, tq), :] for o in offs]
    out = []
    for hh in range(2):
        carry, acc = state[2 * hh], state[2 * hh + 1]
        zs = [_dot_nt(qh[hh], k) for k in ks]
        sps = [_softplus(z) for z in zs]
        if mask0 is not None:
            sps[0] = jnp.where(mask0, sps[0], 0.0)
        if ok_last is not None:
            sps[-1] = jnp.where(ok_last, sps[-1], 0.0)
        ps = [jnp.dot(sp.astype(BF16), tri, preferred_element_type=F32) for sp in sps]
        for b in range(n):
            cum = ps[b] + carry
            a = jnp.exp(zs[b] - cum)
            if b == 0 and mask0 is not None:
                a = jnp.where(mask0, a, 0.0)
            if b == n - 1 and ok_last is not None:
                a = jnp.where(ok_last, a, 0.0)
            acc = acc + jnp.dot(a.astype(BF16), vs[b], preferred_element_type=F32)
            carry = cum[:, 0:1]
        out.extend((carry, acc))
    return tuple(out)


def _sb_kernel(q_ref, k_ref, v_ref, tri_ref, o_ref, *, tq):
    i = pl.program_id(2)
    half = lax.broadcasted_iota(jnp.int32, (tq, LANES), 1) >> 6
    q = q_ref[...]
    qh = [jnp.where(half == hh, q, 0) for hh in range(2)]
    tri = tri_ref[...]
    odd = (i & 1) == 1
    causal = lax.broadcasted_iota(jnp.int32, (tq, tq), 1) < lax.broadcasted_iota(jnp.int32, (tq, tq), 0)
    zero = (jnp.zeros((tq, 1), F32), jnp.zeros((tq, LANES), F32))
    state = _sb_blocks(qh, k_ref, v_ref, tri, [i, jnp.maximum(i - 1, 0)], zero + zero, causal, odd, tq)
    first = i - 1 - (i & 1)
    n_two = (first + 1 >> 1) & 1

    def two(p, st):
        return _sb_blocks(qh, k_ref, v_ref, tri, [first, first - 1], st, None, None, tq)

    def four(p, st):
        ja = first - 2 * n_two - 4 * p
        return _sb_blocks(qh, k_ref, v_ref, tri, [ja, ja - 1, ja - 2, ja - 3], st, None, None, tq)

    st = lax.fori_loop(0, n_two, two, state)
    st = lax.fori_loop(0, first + 1 >> 2, four, st)
    o_ref[...] = jnp.where(half == 0, st[1], st[3])


def _sb(sbq, sbk, sbv, b, s, tq=256):
    tq = min(tq, s)
    nqb = s // tq
    tri = _tri_matrix(tq)
    npair = SB_HEADS // 2
    qspec = pl.BlockSpec((tq, LANES), lambda bi, p, i: (bi * nqb + i, p))
    kspec = pl.BlockSpec((s, LANES), lambda bi, p, i: (bi, p))
    return pl.pallas_call(
        functools.partial(_sb_kernel, tq=tq),
        grid=(b, npair, nqb),
        in_specs=[qspec, kspec, kspec, pl.BlockSpec(tri.shape, lambda bi, p, i: (0, 0))],
        out_specs=qspec,
        out_shape=jax.ShapeDtypeStruct((b * s, SB_W), F32),
        compiler_params=_cparams(("parallel", "parallel", "parallel")),
        name="sb",
    )(sbq, sbk, sbv, tri)


def _prep_merge(w_out, w_router, b_router):
    w_a = [w_out[h * HEAD_DIM:(h + 1) * HEAD_DIM] for h in _HEAD_PERM]
    w_out_p = jnp.concatenate(w_a + [w_out[NSA_W:]], axis=0).astype(BF16)
    e = np.zeros((LANES, 3 * NSA_W), np.float32)
    for br in range(3):
        for p, h in enumerate(_HEAD_PERM):
            e[br * NSA_HEADS + h, br * NSA_W + p * HEAD_DIM: br * NSA_W + (p + 1) * HEAD_DIM] = 1.0
    w_r = jnp.concatenate([w_router, jnp.zeros((D_MODEL, LANES - N_EXPERTS), w_router.dtype)], axis=1)
    b_r = jnp.concatenate([b_router, jnp.full((LANES - N_EXPERTS,), -1e30, b_router.dtype)]).reshape(1, LANES)
    return w_out_p, jnp.asarray(e, BF16), w_r, b_r


def _merge_kernel(x_ref, oc_ref, os_ref, ow_ref, ob_ref, gate_ref, br_ref, e_ref, wo_ref, g_ref, wr_ref, brt_ref,
                  x1_ref, hp_ref, idx_ref, wts_ref):
    gexp = _dot3(br_ref[...], e_ref[...], passes=2)
    o_a = gexp[:, :NSA_W] * oc_ref[...] + gexp[:, NSA_W:2 * NSA_W] * os_ref[...] + gexp[:, 2 * NSA_W:] * ow_ref[...]
    gate = gate_ref[...].astype(F32)
    cat = jnp.concatenate([gate[:, :NSA_W] * o_a, gate[:, NSA_W:] * ob_ref[...]], axis=1).astype(BF16)
    x1 = x_ref[...] + jnp.dot(cat, wo_ref[...], preferred_element_type=F32)
    x1_ref[...] = x1
    ms = jnp.mean(x1 * x1, axis=-1, keepdims=True)
    h = (x1 * lax.rsqrt(ms + RMS_EPS)) * g_ref[...]
    hb = h.astype(BF16).astype(F32)
    for j in range(D_MODEL // LANES):
        hp_ref[:, j, :] = hb[:, j * LANES:(j + 1) * LANES]
    lg = jnp.dot(h, wr_ref[...], preferred_element_type=F32, precision=lax.Precision.HIGHEST) + brt_ref[...]
    lane = lax.broadcasted_iota(jnp.int32, lg.shape, 1)
    vals, ids = [], []
    for _ in range(TOP_K):
        mx = jnp.max(lg, axis=-1, keepdims=True)
        first = jnp.min(jnp.where(lg == mx, lane, LANES), axis=-1, keepdims=True)
        vals.append(mx)
        ids.append(first)
        lg = jnp.where(lane == first, -jnp.inf, lg)
    es = [jnp.exp(v - vals[0]) for v in vals]
    tot = es[0] + es[1] + es[2] + es[3]
    idx = jnp.zeros(lg.shape, jnp.int32)
    wts = jnp.zeros(lg.shape, F32)
    for k in range(TOP_K):
        idx = jnp.where(lane == k, ids[k], idx)
        wts = jnp.where(lane == k, es[k] / tot, wts)
    idx_ref[...] = idx
    wts_ref[...] = wts


def _merge(x2d, oc, os_, ow, ob, gate, br, prep, g_ffn, tm=256):
    t = x2d.shape[0]
    tm = min(tm, t)
    assert t % tm == 0
    w_out_p, e, w_r, b_r = prep
    row = lambda n: pl.BlockSpec((tm, n), lambda i: (i, 0))
    full = lambda a: pl.BlockSpec(a.shape, lambda i: (0,) * a.ndim)
    return pl.pallas_call(
        _merge_kernel,
        grid=(t // tm,),
        in_specs=[row(D_MODEL), row(NSA_W), row(NSA_W), row(NSA_W), row(SB_W), row(NSA_W + SB_W), row(LANES),
                  full(e), full(w_out_p), full(g_ffn), full(w_r), full(b_r)],
        out_specs=[row(D_MODEL), pl.BlockSpec((tm, ROW_SUB, LANES), lambda i: (i, 0, 0)), row(LANES), row(LANES)],
        out_shape=[jax.ShapeDtypeStruct((t, D_MODEL), F32), jax.ShapeDtypeStruct((t, ROW_SUB, LANES), F32),
                   jax.ShapeDtypeStruct((t, LANES), jnp.int32), jax.ShapeDtypeStruct((t, LANES), F32)],
        compiler_params=_cparams(("parallel",)),
        name="merge",
    )(x2d, oc, os_, ow, ob, gate, br, e, w_out_p, g_ffn, w_r, b_r)


MOE_TILE = 256
COMBINE_ROWS = 128
ROW_SUB = D_MODEL // LANES


def _route_tables(idx):
    t = idx.shape[0]
    a = t * TOP_K
    assert a % MOE_TILE == 0
    n_tiles = a // MOE_TILE + N_EXPERTS
    e = idx.reshape(a)
    onehot = (e[:, None] == jnp.arange(N_EXPERTS, dtype=jnp.int32)[None, :]).astype(jnp.int32)
    csum = jnp.cumsum(onehot, axis=0)
    rank = jnp.take_along_axis(csum, e[:, None], axis=1)[:, 0] - 1
    counts = csum[-1]
    tiles_e = (counts + MOE_TILE - 1) // MOE_TILE
    tile_end = jnp.cumsum(tiles_e)
    tile_start = tile_end - tiles_e
    dest = tile_start[e] * MOE_TILE + rank
    row_token = jnp.zeros((n_tiles * MOE_TILE,), jnp.int32).at[dest].set(jnp.arange(a, dtype=jnp.int32) // TOP_K)
    tile_ids = jnp.arange(n_tiles, dtype=jnp.int32)
    tile_expert = jnp.minimum(jnp.sum((tile_end[None, :] <= tile_ids[:, None]).astype(jnp.int32), axis=1),
                              N_EXPERTS - 1).astype(jnp.int32)
    return (row_token.reshape(n_tiles, 1, MOE_TILE), dest.reshape(t, TOP_K), tile_expert,
            tile_end[-1:].astype(jnp.int32))


def _deinterleave_matrix():
    m = np.zeros((2 * LANES, 2 * LANES), np.float32)
    m[2 * np.arange(LANES), np.arange(LANES)] = 1.0
    m[2 * np.arange(LANES) + 1, LANES + np.arange(LANES)] = 1.0
    return jnp.asarray(m, BF16)


def _row_tiles_to_2d(load):
    return jnp.concatenate([load(j) for j in range(ROW_SUB)], axis=1)


def _expert_kernel(te_ref, nu_ref, rt_ref, rtn_ref, h_ref, wgu_ref, wdn_ref, bg_ref, bu_ref, bd_ref, seo_ref,
                   y_ref, xbuf, sem, wg_s, wu_s, wd_s):
    i = pl.program_id(0)
    n_used = nu_ref[0]
    slot = i % 2

    def copy(idx_ref, r, s):
        return pltpu.make_async_copy(h_ref.at[pl.ds(idx_ref[0, r], 1)], xbuf.at[s, pl.ds(r, 1)], sem.at[s])

    def issue(idx_ref, s):
        def body(r, c):
            copy(idx_ref, r, s).start()
            return c
        lax.fori_loop(0, MOE_TILE, body, 0, unroll=8)

    def wait_rows():
        def wait(r, c):
            copy(rt_ref, r, slot).wait()
            return c
        lax.fori_loop(0, MOE_TILE, wait, 0, unroll=8)

    @pl.when(i == 0)
    def _():
        issue(rt_ref, 0)

    @pl.when(i + 1 < n_used)
    def _():
        issue(rtn_ref, 1 - slot)

    @pl.when(i < n_used)
    def _():
        @pl.when((i == 0) | (te_ref[i] != te_ref[jnp.maximum(i - 1, 0)]))
        def _():
            for k in range(D_FF // LANES):
                wblk = wgu_ref[:, 2 * LANES * k:2 * LANES * (k + 1)].astype(BF16)
                split = jnp.dot(wblk, seo_ref[...], preferred_element_type=F32).astype(BF16)
                wg_s[:, LANES * k:LANES * (k + 1)] = split[:, :LANES]
                wu_s[:, LANES * k:LANES * (k + 1)] = split[:, LANES:]
            wd_s[...] = wdn_ref[...].astype(BF16)

        wait_rows()
        x = _row_tiles_to_2d(lambda j: xbuf[slot, :, j, :]).astype(BF16)
        g = jnp.minimum(jnp.dot(x, wg_s[...], preferred_element_type=F32) + bg_ref[...], SWIGLU_LIMIT)
        u = jnp.clip(jnp.dot(x, wu_s[...], preferred_element_type=F32) + bu_ref[...], -SWIGLU_LIMIT, SWIGLU_LIMIT)
        act = (u + 1.0) * g * jax.nn.sigmoid(SWIGLU_ALPHA * g)
        y = jnp.dot(act.astype(BF16), wd_s[...], preferred_element_type=F32) + bd_ref[...]
        for j in range(ROW_SUB):
            y_ref[:, j, :] = y[:, j * LANES:(j + 1) * LANES]

    @pl.when(i >= n_used)
    def _():
        y_ref[...] = jnp.zeros(y_ref.shape, F32)


def _experts(h_rows, row_token, tile_expert, n_used, w_gu, w_dn, bg, bu, bd):
    n_tiles = row_token.shape[0]
    seo = _deinterleave_matrix()
    wspec = lambda shape: pl.BlockSpec((None,) + shape, lambda i, te, nu: (te[i], 0, 0))
    full = lambda a: pl.BlockSpec(a.shape, lambda i, te, nu: (0,) * a.ndim)
    rt_spec = lambda off: pl.BlockSpec((None, 1, MOE_TILE), lambda i, te, nu: (jnp.minimum(i + off, n_tiles - 1), 0, 0),
                                       memory_space=pltpu.SMEM)
    return pl.pallas_call(
        _expert_kernel,
        grid_spec=pltpu.PrefetchScalarGridSpec(
            num_scalar_prefetch=2,
            grid=(n_tiles,),
            in_specs=[rt_spec(0), rt_spec(1), pl.BlockSpec(memory_space=pl.ANY),
                      wspec((D_MODEL, 2 * D_FF)), wspec((D_FF, D_MODEL)), wspec((1, D_FF)), wspec((1, D_FF)),
                      wspec((1, D_MODEL)), full(seo)],
            out_specs=pl.BlockSpec((MOE_TILE, ROW_SUB, LANES), lambda i, te, nu: (i, 0, 0)),
            scratch_shapes=[pltpu.VMEM((2, MOE_TILE, ROW_SUB, LANES), F32), pltpu.SemaphoreType.DMA((2,)),
                            pltpu.VMEM((D_MODEL, D_FF), BF16), pltpu.VMEM((D_MODEL, D_FF), BF16),
                            pltpu.VMEM((D_FF, D_MODEL), BF16)],
        ),
        out_shape=jax.ShapeDtypeStruct((n_tiles * MOE_TILE, ROW_SUB, LANES), F32),
        compiler_params=_cparams(("arbitrary",)),
        name="moe_experts",
    )(tile_expert, n_used, row_token, row_token, h_rows, w_gu, w_dn, bg, bu, bd, seo)


def _combine_kernel(pos_ref, posn_ref, x_ref, wts_ref, g_ref, y_ref, out_ref, buf, sem, *, n):
    i = pl.program_id(0)
    n_rows = TOP_K * COMBINE_ROWS

    def copy(p_ref, r, slot):
        return pltpu.make_async_copy(y_ref.at[pl.ds(p_ref[0, r], 1)],
                                     buf.at[slot, r & (TOP_K - 1), pl.ds(r >> 2, 1)], sem.at[slot])

    def issue(p_ref, slot):
        def body(r2, c):
            copy(p_ref, 2 * r2, slot).start(priority=0)
            copy(p_ref, 2 * r2 + 1, slot).start(priority=1)
            return c
        lax.fori_loop(0, n_rows // 2, body, 0, unroll=4)

    slot = i % 2

    @pl.when(i == 0)
    def _():
        issue(pos_ref, 0)

    @pl.when(i + 1 < n)
    def _():
        issue(posn_ref, 1 - slot)

    def wait(r, c):
        copy(pos_ref, r, slot).wait()
        return c
    lax.fori_loop(0, n_rows, wait, 0, unroll=8)

    w = wts_ref[...]
    parts = []
    ssq = jnp.zeros((COMBINE_ROWS, 1), F32)
    for j in range(ROW_SUB):
        xs = x_ref[:, j * LANES:(j + 1) * LANES]
        for k in range(TOP_K):
            xs = xs + w[:, k:k + 1] * buf[slot, k, :, j, :]
        parts.append(xs)
        ssq = ssq + jnp.sum(xs * xs, axis=-1, keepdims=True)
    rs = lax.rsqrt(ssq * (1.0 / D_MODEL) + RMS_EPS)
    for j in range(ROW_SUB):
        out_ref[:, j * LANES:(j + 1) * LANES] = (parts[j] * rs) * g_ref[:, j * LANES:(j + 1) * LANES]


def _combine(x1, wts, pos, y_sorted, g_final):
    t = x1.shape[0]
    assert t % COMBINE_ROWS == 0
    steps = t // COMBINE_ROWS
    pos2 = pos.reshape(steps, 1, TOP_K * COMBINE_ROWS)
    row = lambda nn: pl.BlockSpec((COMBINE_ROWS, nn), lambda i: (i, 0))
    return pl.pallas_call(
        functools.partial(_combine_kernel, n=steps),
        grid=(steps,),
        in_specs=[pl.BlockSpec((None, 1, TOP_K * COMBINE_ROWS), lambda i: (i, 0, 0), memory_space=pltpu.SMEM),
                  pl.BlockSpec((None, 1, TOP_K * COMBINE_ROWS), lambda i: (jnp.minimum(i + 1, steps - 1), 0, 0),
                               memory_space=pltpu.SMEM),
                  row(D_MODEL), row(LANES), pl.BlockSpec(g_final.shape, lambda i: (0, 0)),
                  pl.BlockSpec(memory_space=pl.ANY)],
        out_specs=row(D_MODEL),
        out_shape=jax.ShapeDtypeStruct((t, D_MODEL), F32),
        scratch_shapes=[pltpu.VMEM((2, TOP_K, COMBINE_ROWS, ROW_SUB, LANES), F32), pltpu.SemaphoreType.DMA((2,))],
        compiler_params=_cparams(("arbitrary",)),
        name="moe_combine",
    )(pos2, pos2, x1, wts, g_final, y_sorted)


QPAD = 8


def _select_blocks_rows(imp_sel, cur, nsel):
    tq = imp_sel.shape[0]
    blk = lax.broadcasted_iota(jnp.int32, (tq, LANES), 1)
    forced = (blk == 0) | (blk == cur) | (blk == cur - 1)
    score = jnp.where(blk <= cur, imp_sel + jnp.where(forced, FORCE_BONUS, 0.0), -1e9)
    cnt = jnp.zeros((tq, LANES), jnp.int32)
    for k in range(nsel):
        sk = score[:, k:k + 1]
        beats = (sk > score) | ((sk == score) & (blk > k))
        cnt = cnt + jnp.where(beats, 1, 0)
    return jnp.where((cnt < TOP_N) & (blk < nsel), 1.0, 0.0)


def _pad_rows(x, n):
    return jnp.concatenate([x, jnp.zeros((n - x.shape[0], x.shape[1]), x.dtype)], axis=0)


NSA_S_SEQS = 4
SB_S_SEQS = 2


def _nsa_s_kernel(pt_ref, qc_ref, qr_ref, xnew_ref, snew_ref, w1_ref, w2_ref, m_ref, *rest, n_pages, past, dec, seqs):
    ocmp_ref, oslc_ref = rest[2 * seqs * n_pages:]
    pages = [rest[2 * n_pages * s:2 * n_pages * (s + 1)] for s in range(seqs)]
    nc = n_pages * (PAGE_SIZE // CMP_STRIDE)
    xb = jnp.concatenate([p[...] for s in range(seqs) for p in pages[s][:n_pages]], axis=0).astype(BF16)
    y1 = jnp.dot(xb, w1_ref[...], preferred_element_type=F32)
    y2 = jnp.dot(xb, w2_ref[...], preferred_element_type=F32)
    xn = jnp.concatenate([xnew_ref[s] for s in range(seqs)] + [xnew_ref[0]] * (8 - seqs), axis=0)
    ynew = jnp.dot(xn.astype(BF16), w2_ref[...], preferred_element_type=F32)
    rid = lax.broadcasted_iota(jnp.int32, (nc, y2.shape[1]), 0)
    for s in range(seqs):
        y2s = pltpu.roll(y2[s * nc:(s + 1) * nc], nc - 1, 0)
        kcv = y1[s * nc:(s + 1) * nc] + jnp.where(rid < nc - 1, y2s, ynew[s:s + 1])
        _nsa_s_one(qc_ref.at[s], qr_ref.at[s], snew_ref.at[s], m_ref, kcv[:, :KV_W].astype(BF16),
                   kcv[:, KV_W:].astype(BF16), pages[s][n_pages:], ocmp_ref.at[s], oslc_ref.at[s], n_pages, past, dec)


def _nsa_s_one(qc_ref, qr_ref, snew_ref, m_ref, kc, vc, slc_pages, ocmp_ref, oslc_ref, n_pages, past, dec):
    tq = QPAD
    nsel = past // SEL_BLK + 1
    trow = lax.broadcasted_iota(jnp.int32, (NSA_REP * tq, 1), 0) & (tq - 1)
    rpos = past + jnp.minimum(trow, dec - 1)
    qpos = past + jnp.minimum(lax.broadcasted_iota(jnp.int32, (tq, 1), 0), dec - 1)
    lane = lax.broadcasted_iota(jnp.int32, (tq, LANES), 1)
    snew = snew_ref[...]
    k_all = jnp.concatenate([pg[:, :KV_W].astype(BF16) for pg in slc_pages]
                            + [_pad_rows(snew[:, :KV_W], LANES).astype(BF16)], axis=0)
    v_all = jnp.concatenate([pg[:, KV_W:].astype(BF16) for pg in slc_pages]
                            + [_pad_rows(snew[:, KV_W:], LANES).astype(BF16)], axis=0)
    o_c, o_s = [], []
    for g in range(NSA_KV):
        qcg = _stack_group(qc_ref, g, tq)
        qrg = _stack_group(qr_ref, g, tq)
        s = _dot_nt(qcg, kc)
        cidx = lax.broadcasted_iota(jnp.int32, s.shape, 1)
        cmask = cidx * CMP_STRIDE + (CMP_BLK - 1) <= rpos
        s = jnp.where(cmask, s, -1e30)
        e = jnp.where(cmask, jnp.exp(s - jnp.max(s, axis=-1, keepdims=True)), 0.0)
        p = e / jnp.maximum(jnp.sum(e, axis=-1, keepdims=True), 1e-30)
        o_c.append(jnp.dot(p.astype(BF16), vc, preferred_element_type=F32))
        imp = p[0:tq] + p[tq:2 * tq] + p[2 * tq:3 * tq] + p[3 * tq:4 * tq]
        sel = _select_blocks_rows(_dot3(imp, m_ref[...]), qpos >> 6, nsel)
        masks =[jnp.where(lane < SEL_BLK, sel[:, 2 * j:2 * j + 1], sel[:, 2 * j + 1:2 * j + 2]) > 0.5
                 for j in range(n_pages)]
        masks.append((sel[:, nsel - 1:nsel] > 0.5) & (lane < dec) & (past + lane <= qpos))
        valid = jnp.concatenate([jnp.concatenate(masks, axis=1)] * NSA_REP, axis=0)
        sc = jnp.where(valid, _dot_nt(qrg, k_all), -1e30)
        a = jnp.where(valid, jnp.exp(sc - jnp.max(sc, axis=-1, keepdims=True)), 0.0)
        l = jnp.sum(a, axis=-1, keepdims=True)
        o_s.append(jnp.dot(a.astype(BF16), v_all, preferred_element_type=F32) / jnp.maximum(l, 1e-30))
    ocmp_ref[...] = _unstack_pairs(o_c[0], o_c[1], tq)
    oslc_ref[...] = _unstack_pairs(o_s[0], o_s[1], tq)


def _nsa_s(page_table, qc, qr, xnew, snew, w1, w2, cache_cmp, cache_slc, past, dec):
    bd, n_pages = page_table.shape
    seqs = NSA_S_SEQS if bd % NSA_S_SEQS == 0 else 1
    cpp = PAGE_SIZE // CMP_STRIDE
    cmp_pages = cache_cmp.reshape(cache_cmp.shape[0], cpp, CMP_STRIDE * 2 * KV_W)
    m = _imp_matrix(n_pages * cpp, past // SEL_BLK + 1)
    per_b = lambda a: pl.BlockSpec((seqs,) + a.shape[1:], lambda b, pt: (b,) + (0,) * (a.ndim - 1))
    full = lambda a: pl.BlockSpec(a.shape, lambda b, pt: (0,) * a.ndim)
    page = lambda a, s, j: pl.BlockSpec((None,) + a.shape[1:], lambda b, pt: (pt[b * seqs + s, j], 0, 0))
    ospec = pl.BlockSpec((seqs, QPAD, NSA_W), lambda b, pt: (b, 0, 0))
    page_specs, page_args = [], []
    for s in range(seqs):
        page_specs += [page(cmp_pages, s, j) for j in range(n_pages)] + [page(cache_slc, s, j) for j in range(n_pages)]
        page_args += [cmp_pages] * n_pages + [cache_slc] * n_pages
    return pl.pallas_call(
        functools.partial(_nsa_s_kernel, n_pages=n_pages, past=past, dec=dec, seqs=seqs),
        grid_spec=pltpu.PrefetchScalarGridSpec(
            num_scalar_prefetch=1,
            grid=(bd // seqs,),
            in_specs=[per_b(qc), per_b(qr), per_b(xnew), per_b(snew), full(w1), full(w2), full(m)] + page_specs,
            out_specs=[ospec, ospec],
        ),
        out_shape=[jax.ShapeDtypeStruct((bd, QPAD, NSA_W), F32)] * 2,
        compiler_params=_cparams(("arbitrary",)),
        name="nsa_sample",
    )(page_table, qc, qr, xnew, snew, w1, w2, m, *page_args)


def _window_s_kernel(qr_ref, st_ref, wnew_ref, o_ref, *, past, dec):
    tq = QPAD
    wbuf = st_ref.shape[0]
    trow = jnp.minimum(lax.broadcasted_iota(jnp.int32, (NSA_REP * tq, 1), 0) & (tq - 1), dec - 1)
    k = st_ref[:, :KV_W].astype(BF16)
    v = st_ref[:, KV_W:].astype(BF16)
    wnew = wnew_ref[...]
    kn = _pad_rows(wnew[:, :KV_W], LANES).astype(BF16)
    vn = _pad_rows(wnew[:, KV_W:], LANES).astype(BF16)
    outs = []
    for g in range(NSA_KV):
        qrg = _stack_group(qr_ref, g, tq)
        s0 = _dot_nt(qrg, k)
        d0 = wbuf + trow - lax.broadcasted_iota(jnp.int32, s0.shape, 1)
        m0 = (d0 >= 0) & (d0 < WINDOW)
        s1 = _dot_nt(qrg, kn)
        i1 = lax.broadcasted_iota(jnp.int32, s1.shape, 1)
        m1 = (i1 < dec) & (trow - i1 >= 0) & (trow - i1 < WINDOW)
        s0 = jnp.where(m0, s0, -1e30)
        s1 = jnp.where(m1, s1, -1e30)
        mx = jnp.maximum(jnp.max(s0, axis=-1, keepdims=True), jnp.max(s1, axis=-1, keepdims=True))
        e0 = jnp.where(m0, jnp.exp(s0 - mx), 0.0)
        e1 = jnp.where(m1, jnp.exp(s1 - mx), 0.0)
        den = jnp.maximum(jnp.sum(e0, axis=-1, keepdims=True) + jnp.sum(e1, axis=-1, keepdims=True), 1e-30)
        o = (jnp.dot(e0.astype(BF16), v, preferred_element_type=F32)
             + jnp.dot(e1.astype(BF16), vn, preferred_element_type=F32))
        outs.append(o / den)
    o_ref[...] = _unstack_pairs(outs[0], outs[1], tq)


def _window_s(qr, state, wnew, past, dec):
    bd = qr.shape[0]
    per_b = lambda a: pl.BlockSpec((None,) + a.shape[1:], lambda b: (b,) + (0,) * (a.ndim - 1))
    return pl.pallas_call(
        functools.partial(_window_s_kernel, past=past, dec=dec),
        grid=(bd,),
        in_specs=[per_b(qr), per_b(state), per_b(wnew)],
        out_specs=pl.BlockSpec((None, QPAD, NSA_W), lambda b: (b, 0, 0)),
        out_shape=jax.ShapeDtypeStruct((bd, QPAD, NSA_W), F32),
        compiler_params=_cparams(("parallel",)),
        name="window_sample",
    )(qr, state, wnew)


def _sb_s_kernel(pt_ref, q_ref, new_ref, tri_ref, *rest, n_pages, seqs):
    o_ref = rest[seqs * n_pages]
    for s in range(seqs):
        _sb_s_one(q_ref.at[s], new_ref.at[s], tri_ref, rest[n_pages * s:n_pages * (s + 1)], o_ref.at[s], n_pages)


def _sb_s_one(q_ref, new_ref, tri_ref, pages, o_ref, n_pages):
    tq = QPAD
    nrow = SB_HEADS * tq
    row = lax.broadcasted_iota(jnp.int32, (nrow, SB_W), 0)
    col = lax.broadcasted_iota(jnp.int32, (nrow, SB_W), 1)
    own = (row >> 3) == (col >> 6)
    qbd = jnp.where(own, jnp.concatenate([q_ref[...]] * SB_HEADS, axis=0), 0.0).astype(BF16)
    new = new_ref[...]
    nblk = n_pages + 1
    k_all = jnp.concatenate([pg[:, :SB_W].astype(BF16) for pg in pages]
                            + [_pad_rows(new[:, :SB_W], PAGE_SIZE).astype(BF16)], axis=0)
    v_all = jnp.concatenate([pg[:, SB_W:].astype(BF16) for pg in pages]
                            + [_pad_rows(new[:, SB_W:], PAGE_SIZE).astype(BF16)], axis=0)
    z_all = _dot_nt(qbd, k_all)
    blk = lambda x, j: x[j * nrow:(j + 1) * nrow]
    zs = jnp.concatenate([z_all[:, j * PAGE_SIZE:(j + 1) * PAGE_SIZE] for j in range(nblk)], axis=0)
    lane = lax.broadcasted_iota(jnp.int32, (nrow, PAGE_SIZE), 1)
    trow = lax.broadcasted_iota(jnp.int32, (nrow, PAGE_SIZE), 0) & (tq - 1)
    causal_new = lane < trow
    sp = _softplus(zs)
    sp = jnp.concatenate([sp[:n_pages * nrow], jnp.where(causal_new, blk(sp, n_pages), 0.0)], axis=0)
    p = jnp.dot(sp.astype(BF16), tri_ref[...], preferred_element_type=F32)
    carry = jnp.zeros((nrow, 1), F32)
    carries = [None] * nblk
    for j in reversed(range(nblk)):
        carries[j] = jnp.broadcast_to(carry, (nrow, PAGE_SIZE))
        carry = carry + blk(p, j)[:, 0:1]
    a = jnp.exp(zs - (p + jnp.concatenate(carries, axis=0)))
    a_wide = jnp.concatenate([blk(a, j) for j in range(n_pages)] + [jnp.where(causal_new, blk(a, n_pages), 0.0)],
                             axis=1)
    acc = jnp.dot(a_wide.astype(BF16), v_all, preferred_element_type=F32)
    acc = jnp.where(own, acc, 0.0)
    out = acc[0:tq]
    for h in range(1, SB_HEADS):
        out = out + acc[h * tq:(h + 1) * tq]
    o_ref[...] = out


def _sb_s(page_table, sbq, sbnew, cache_sb):
    bd, n_pages = page_table.shape
    seqs = SB_S_SEQS if bd % SB_S_SEQS == 0 else 1
    tri = _tri_matrix(PAGE_SIZE)
    per_b = lambda a: pl.BlockSpec((seqs,) + a.shape[1:], lambda b, pt: (b,) + (0,) * (a.ndim - 1))
    page = lambda s, j: pl.BlockSpec((None,) + cache_sb.shape[1:], lambda b, pt: (pt[b * seqs + s, j], 0, 0))
    return pl.pallas_call(
        functools.partial(_sb_s_kernel, n_pages=n_pages, seqs=seqs),
        grid_spec=pltpu.PrefetchScalarGridSpec(
            num_scalar_prefetch=1,
            grid=(bd // seqs,),
            in_specs=[per_b(sbq), per_b(sbnew), pl.BlockSpec(tri.shape, lambda b, pt: (0, 0))]
            + [page(s, j) for s in range(seqs) for j in range(n_pages)],
            out_specs=pl.BlockSpec((seqs, QPAD, SB_W), lambda b, pt: (b, 0, 0)),
        ),
        out_shape=jax.ShapeDtypeStruct((bd, QPAD, SB_W), F32),
        compiler_params=_cparams(("arbitrary",)),
        name="sb_sample",
    )(page_table, sbq, sbnew, tri, *([cache_sb] * (seqs * n_pages)))


def kernel(x_prompt, x_sample, cache_nsa_cmp, cache_nsa_slc, cache_sb, state_nsa_win, page_table, g_mix, w_in, w_ck,
           w_cv, w_out, g_ffn, w_router, b_router, w_gu, b_gu, w_dn, b_dn, g_final):
    b, s, d = x_prompt.shape
    bd, ds, _ = x_sample.shape
    depth = g_mix.shape[0]
    assert depth == 1 and d == D_MODEL and ds <= QPAD
    n_pages = page_table.shape[1]
    past = n_pages * PAGE_SIZE
    wbuf = state_nsa_win.shape[2]
    n_phys = cache_nsa_cmp.shape[1]
    tp, ts = b * s, bd * ds

    w_perm = _prep_w_in(w_in[0])
    w1, w2 = _prep_w_cmp(w_ck[0], w_cv[0])
    prep = _prep_merge(w_out[0], w_router[0], b_router[0])
    g_mix2, g_ffn2, g_fin2 = g_mix[0].reshape(1, d), g_ffn[0].reshape(1, d), g_final.reshape(1, d)
    bg = b_gu[0][:, None, 0::2]
    bu = b_gu[0][:, None, 1::2]
    bdn = b_dn[0][:, None, :]

    pos_p = jnp.arange(s, dtype=jnp.int32)
    (qc, qr, sbq, cmp_p, slc_p, win_p, sbkv_p, kslc, vslc, kwin, vwin, sbk, sbv, gate_p, br_p) = _proj(
        x_prompt.reshape(tp, d), g_mix2, w_perm, _rope_tables(pos_p), 256)
    kc, vc = _compress(cmp_p, w1, w2, b, s)
    o_cmp, o_slc = _nsa(qc, qr, kc, vc, kslc, vslc, b, s)
    o_win = _window(qr, kwin, vwin, b, s)
    o_sb = _sb(sbq, sbk, sbv, b, s)
    x1_p, h_p, idx_p, wts_p = _merge(x_prompt.reshape(tp, d), o_cmp, o_slc, o_win, o_sb, gate_p, br_p, prep, g_ffn2)

    pos_s = past + jnp.arange(ds, dtype=jnp.int32)
    tabs_s = [jnp.tile(t, (bd, 1)) for t in _rope_tables(pos_s)]
    (qc_s, qr_s, sbq_s, cmp_s, slc_s, win_s, sbkv_s, _, _, _, _, _, _, gate_s, br_s) = _proj(
        x_sample.reshape(ts, d), g_mix2, w_perm, tabs_s, min(256, ts))
    pad_q = lambda a: jnp.pad(a.astype(F32).reshape(bd, ds, a.shape[1]), ((0, 0), (0, QPAD - ds), (0, 0)))
    xnew = jnp.pad(cmp_s.reshape(bd, 1, ds * 2 * KV_W), ((0, 0), (0, 0), (0, (CMP_STRIDE - ds) * 2 * KV_W)))
    o_cmp_s, o_slc_s = _nsa_s(page_table, pad_q(qc_s), pad_q(qr_s), xnew, pad_q(slc_s), w1, w2,
                              cache_nsa_cmp[0].reshape(n_phys, PAGE_SIZE, 2 * KV_W),
                              cache_nsa_slc[0].reshape(n_phys, PAGE_SIZE, 2 * KV_W), past, ds)
    o_win_s = _window_s(pad_q(qr_s), state_nsa_win[0].reshape(bd, wbuf, 2 * KV_W), pad_q(win_s), past, ds)
    o_sb_s = _sb_s(page_table, pad_q(sbq_s), pad_q(sbkv_s), cache_sb[0].reshape(n_phys, PAGE_SIZE, 2 * SB_W))
    unpad = lambda a: a[:, :ds].reshape(ts, a.shape[2])
    x1_s, h_s, idx_s, wts_s = _merge(x_sample.reshape(ts, d), unpad(o_cmp_s), unpad(o_slc_s), unpad(o_win_s),
                                     unpad(o_sb_s), gate_s, br_s, prep, g_ffn2)

    h_rows = jnp.concatenate([h_p, h_s], axis=0)
    idx = jnp.concatenate([idx_p[:, :TOP_K], idx_s[:, :TOP_K]], axis=0)
    row_token, pos, tile_expert, n_used = _route_tables(idx)
    y_sorted = _experts(h_rows, row_token, tile_expert, n_used, w_gu[0], w_dn[0], bg, bu, bdn)
    y_p = _combine(x1_p, wts_p, pos[:tp], y_sorted, g_fin2)
    y_s = _combine(x1_s, wts_s, pos[tp:], y_sorted, g_fin2)

    win_all = jnp.concatenate([state_nsa_win[0], win_s.reshape(bd, ds, 2, NSA_KV, HEAD_DIM)], axis=1)
    lead = lambda a, n, h: a.reshape(1, n, -1, 2, h, HEAD_DIM)
    return (y_p.reshape(b, s, d), y_s.reshape(bd, ds, d),
            lead(cmp_p, b, NSA_KV), lead(slc_p, b, NSA_KV), lead(sbkv_p, b, SB_HEADS),
            lead(win_p, b, NSA_KV)[:, :, -wbuf:] if s >= wbuf else
            jnp.pad(lead(win_p, b, NSA_KV), ((0, 0), (0, 0), (wbuf - s, 0), (0, 0), (0, 0), (0, 0))),
            lead(cmp_s, bd, NSA_KV), lead(slc_s, bd, NSA_KV), lead(sbkv_s, bd, SB_HEADS),
            win_all[None, :, ds:])
```

```python
import functools

import numpy as np
import jax
import jax.numpy as jnp
from jax import lax
from jax.experimental import pallas as pl
from jax.experimental.pallas import tpu as pltpu

D_MODEL = 1024
HEAD_DIM = 64
NSA_HEADS = 8
NSA_KV = 2
NSA_REP = NSA_HEADS // NSA_KV
SB_HEADS = 8
NSA_W = NSA_HEADS * HEAD_DIM
SB_W = SB_HEADS * HEAD_DIM
KV_W = NSA_KV * HEAD_DIM
CMP_BLK = 32
CMP_STRIDE = 16
SEL_BLK = 64
TOP_N = 16
WINDOW = 512
FORCE_BONUS = 1.0e4
ROPE_THETA = 500000.0
ROPE_DIMS = HEAD_DIM // 4
N_EXPERTS = 32
TOP_K = 4
D_FF = D_MODEL
SWIGLU_LIMIT = 7.0
SWIGLU_ALPHA = 1.702
RMS_EPS = 1e-5
PAGE_SIZE = 128
SCALE = HEAD_DIM ** -0.5

LANES = 128
VMEM_LIMIT_BYTES = 56 * 1024 * 1024

BF16 = jnp.bfloat16
F32 = jnp.float32

_OFF = dict(q=0, k_cmp=512, v_cmp=640, k_slc=768, v_slc=896, k_win=1024, v_win=1152, br=1280,
            sb_q=1304, sb_k=1816, sb_v=2328, gate_a=2840, gate_b=3352, end=3864)
_HEAD_PERM = [j + 4 * half for j in range(4) for half in range(2)]
_C_Q, _C_SBQ, _C_CMP, _C_SLC, _C_WIN, _C_SBKV, _C_GATE, _C_BR, _NP = 0, 512, 1024, 1280, 1536, 1792, 2816, 3840, 3968


def _cparams(sem):
    return pltpu.CompilerParams(dimension_semantics=sem, vmem_limit_bytes=VMEM_LIMIT_BYTES)


def _prep_w_in(w):
    def head_perm(base):
        return [w[:, base + h * HEAD_DIM: base + (h + 1) * HEAD_DIM] for h in _HEAD_PERM]
    parts = [c * SCALE for c in head_perm(_OFF['q'])]
    parts.append(w[:, _OFF['sb_q']:_OFF['sb_k']] * SCALE)
    parts.append(w[:, _OFF['k_cmp']:_OFF['br']])
    parts.append(w[:, _OFF['sb_k']:_OFF['gate_a']])
    parts.extend(head_perm(_OFF['gate_a']))
    parts.append(w[:, _OFF['gate_b']:_OFF['end']])
    parts.append(w[:, _OFF['br']:_OFF['sb_q']])
    parts.append(jnp.zeros((w.shape[0], LANES - 3 * NSA_HEADS), w.dtype))
    return jnp.concatenate(parts, axis=1).astype(BF16)


def _rope_tables(pos):
    half = ROPE_DIMS // 2
    inv = ROPE_THETA ** (-jnp.arange(half, dtype=F32) / half)
    ang = pos.astype(F32)[:, None] * inv[None, :]
    cos = jnp.cos(ang)
    sin = jnp.sin(ang)
    n = pos.shape[0]
    one = jnp.ones((n, HEAD_DIM - ROPE_DIMS), F32)
    zero = jnp.zeros((n, HEAD_DIM - ROPE_DIMS), F32)
    z8 = jnp.zeros((n, half), F32)
    c = jnp.concatenate([cos, cos, one], axis=1)
    s1 = jnp.concatenate([-sin, z8, zero], axis=1)
    s2 = jnp.concatenate([z8, sin, zero], axis=1)
    tile2 = lambda t: jnp.concatenate([t, t], axis=1)
    return tile2(c), tile2(s1), tile2(s2)


def _rope_apply(x, c, s1, s2):
    outs = []
    for j in range(x.shape[1] // LANES):
        xb = x[:, j * LANES:(j + 1) * LANES]
        outs.append(xb * c + pltpu.roll(xb, LANES - 8, 1) * s1 + pltpu.roll(xb, 8, 1) * s2)
    return outs[0] if len(outs) == 1 else jnp.concatenate(outs, axis=1)


def _proj_kernel(x_ref, g_ref, w_ref, c_ref, s1_ref, s2_ref,
                 qc_ref, qr_ref, sbq_ref, cmp_ref, slc_ref, win_ref, sbkv_ref,
                 kslc_ref, vslc_ref, kwin_ref, vwin_ref, sbk_ref, sbv_ref, gate_ref, br_ref):
    x = x_ref[...]
    ms = jnp.mean(x * x, axis=-1, keepdims=True)
    h = (x * lax.rsqrt(ms + RMS_EPS)) * g_ref[...]
    hb = h.astype(BF16)
    c, s1, s2 = c_ref[...], s1_ref[...], s2_ref[...]

    def mm(c0, c1):
        return jnp.dot(hb, w_ref[:, c0:c1], preferred_element_type=F32)

    q = mm(_C_Q, _C_SBQ)
    qc_ref[...] = q.astype(BF16)
    qr_ref[...] = _rope_apply(q, c, s1, s2).astype(BF16)
    sbq_ref[...] = mm(_C_SBQ, _C_CMP).astype(BF16)
    cmp_ref[...] = mm(_C_CMP, _C_SLC)
    for c0, cache_ref, k_ref, v_ref in ((_C_SLC, slc_ref, kslc_ref, vslc_ref), (_C_WIN, win_ref, kwin_ref, vwin_ref)):
        z = mm(c0, c0 + 2 * KV_W)
        k = _rope_apply(z[:, :KV_W], c, s1, s2)
        v = z[:, KV_W:]
        cache_ref[:, :KV_W] = k
        cache_ref[:, KV_W:] = v
        k_ref[...] = k.astype(BF16)
        v_ref[...] = v.astype(BF16)
    z = mm(_C_SBKV, _C_GATE)
    sbkv_ref[...] = z
    sbk_ref[...] = z[:, :SB_W].astype(BF16)
    sbv_ref[...] = z[:, SB_W:].astype(BF16)
    gate_ref[...] = jax.nn.sigmoid(mm(_C_GATE, _C_BR)).astype(BF16)
    br_ref[...] = jax.nn.sigmoid(mm(_C_BR, _NP))


def _proj(x2d, g, w_perm, tabs, tm):
    t = x2d.shape[0]
    tab_blocks = tabs[0].shape[0] // tm
    assert t % tm == 0 and tabs[0].shape[0] % tm == 0
    row = lambda n: pl.BlockSpec((tm, n), lambda i: (i, 0))
    tab = pl.BlockSpec((tm, LANES), lambda i: (i % tab_blocks, 0))
    full = lambda a: pl.BlockSpec(a.shape, lambda i: (0,) * a.ndim)
    outs = [(512, BF16), (512, BF16), (512, BF16), (256, F32), (256, F32), (256, F32), (1024, F32),
            (128, BF16), (128, BF16), (128, BF16), (128, BF16), (512, BF16), (512, BF16), (1024, BF16), (128, F32)]
    return pl.pallas_call(
        _proj_kernel,
        grid=(t // tm,),
        in_specs=[row(D_MODEL), full(g), full(w_perm), tab, tab, tab],
        out_specs=[row(n) for n, _ in outs],
        out_shape=[jax.ShapeDtypeStruct((t, n), dt) for n, dt in outs],
        compiler_params=_cparams(("parallel",)),
        name="proj",
    )(x2d, g, w_perm, *tabs)


def _prep_w_cmp(w_ck, w_cv):
    eye2 = jnp.eye(NSA_KV, dtype=w_ck.dtype)
    bd = lambda w: jnp.einsum('ab,ide->iadbe', eye2, w).reshape(CMP_BLK, KV_W, KV_W)
    z = jnp.zeros((CMP_BLK, KV_W, KV_W), w_ck.dtype)
    w_all = jnp.concatenate([jnp.concatenate([bd(w_ck), z], axis=2),
                             jnp.concatenate([z, bd(w_cv)], axis=2)], axis=1)
    w_all = w_all.astype(BF16)
    return (w_all[:CMP_STRIDE].reshape(CMP_STRIDE * 2 * KV_W, 2 * KV_W),
            w_all[CMP_STRIDE:].reshape(CMP_STRIDE * 2 * KV_W, 2 * KV_W))


def _compress_rows(xb, w1, w2):
    n = xb.shape[0]
    y1 = jnp.dot(xb, w1, preferred_element_type=F32)
    y2 = jnp.dot(xb, w2, preferred_element_type=F32)
    y2s = pltpu.roll(y2, n - 1, 0)
    rid = lax.broadcasted_iota(jnp.int32, y2.shape, 0)
    return y1 + jnp.where(rid < n - 1, y2s, 0.0)


def _compress_kernel(x_ref, w1_ref, w2_ref, k_ref, v_ref):
    out = _compress_rows(x_ref[...].astype(BF16), w1_ref[...], w2_ref[...])
    k_ref[...] = out[:, :KV_W].astype(BF16)
    v_ref[...] = out[:, KV_W:].astype(BF16)


def _compress(cache2d, w1, w2, b, s):
    nc = s // CMP_STRIDE
    x = cache2d.reshape(b, nc, CMP_STRIDE * 2 * KV_W)
    full = lambda a: pl.BlockSpec(a.shape, lambda i: (0,) * a.ndim)
    blk = pl.BlockSpec((None, nc, KV_W), lambda i: (i, 0, 0))
    return pl.pallas_call(
        _compress_kernel,
        grid=(b,),
        in_specs=[pl.BlockSpec((None, nc, x.shape[2]), lambda i: (i, 0, 0)), full(w1), full(w2)],
        out_specs=[blk, blk],
        out_shape=[jax.ShapeDtypeStruct((b, nc, KV_W), BF16)] * 2,
        compiler_params=_cparams(("parallel",)),
        name="compress",
    )(x, w1, w2)


_NT = (((1,), (1,)), ((), ()))


def _dot_nt(a, b):
    return lax.dot_general(a, b, _NT, preferred_element_type=F32)


def _dot3(x, m, passes=3):
    d = lambda a: jnp.dot(a, m, preferred_element_type=F32)
    hi = x.astype(BF16)
    r1 = x - hi.astype(F32)
    mid = r1.astype(BF16)
    out = d(hi) + d(mid)
    if passes == 3:
        out = out + d((r1 - mid.astype(F32)).astype(BF16))
    return out


def _stack_group(ref, g, tq):
    half = lax.broadcasted_iota(jnp.int32, (tq, LANES), 1) >> 6
    return jnp.concatenate([jnp.where(half == g, ref[:, j * LANES:(j + 1) * LANES], 0)
                            for j in range(NSA_REP)], axis=0).astype(BF16)


def _unstack_pairs(o0, o1, tq):
    half = lax.broadcasted_iota(jnp.int32, (tq, LANES), 1) >> 6
    return jnp.concatenate([jnp.where(half == 0, o0[j * tq:(j + 1) * tq], o1[j * tq:(j + 1) * tq])
                            for j in range(NSA_REP)], axis=1)


def _imp_matrix(nc, nsel=None):
    m = np.zeros((nc, LANES), np.float32)
    for j in range(nc * CMP_STRIDE // SEL_BLK if nsel is None else nsel):
        for c, wgt in ((4 * j - 1, 0.5), (4 * j, 1.0), (4 * j + 1, 1.0), (4 * j + 2, 1.0), (4 * j + 3, 0.5)):
            if 0 <= c < nc:
                m[c, j] += wgt
    return jnp.asarray(m, BF16)


def _select_blocks(imp_sel, cur, nsel):
    tq = imp_sel.shape[0]
    blk = lax.broadcasted_iota(jnp.int32, (tq, LANES), 1)
    forced = (blk == 0) | (blk == cur) | (blk == cur - 1)
    score = jnp.where(blk <= cur, imp_sel + jnp.where(forced, FORCE_BONUS, 0.0), -1e9)
    st = score.T
    nslab = -(-nsel // 8)
    slabs = [st[8 * v:8 * v + 8] for v in range(nslab)]
    sub = lax.broadcasted_iota(jnp.int32, (8, tq), 0)
    cnts = [jnp.zeros((8, tq), jnp.int32) for _ in range(nslab)]
    for k in range(nsel):
        sk = st[k:k + 1, :]
        for v in range(nslab):
            if 8 * v + 7 < k:
                inc = jnp.where(sk > slabs[v], 1, 0)
            elif 8 * v > k:
                inc = jnp.where(sk >= slabs[v], 1, 0)
            else:
                inc = jnp.where(sub + 8 * v > k, jnp.where(sk >= slabs[v], 1, 0), jnp.where(sk > slabs[v], 1, 0))
            cnts[v] = cnts[v] + inc
    parts = [jnp.where((c < TOP_N) & (sub + 8 * v < nsel), 1.0, 0.0) for v, c in enumerate(cnts)]
    if 8 * nslab < LANES:
        parts.append(jnp.zeros((LANES - 8 * nslab, tq), F32))
    return jnp.concatenate(parts, axis=0).T.astype(BF16)


def _nsa_kernel(qc_ref, qr_ref, kc_ref, vc_ref, ks_ref, vs_ref, m_ref, ocmp_ref, oslc_ref, *, tq, kt):
    q0 = pl.program_id(1) * tq
    nc = kc_ref.shape[0]
    nsel = nc * CMP_STRIDE // SEL_BLK
    qpos = q0 + lax.broadcasted_iota(jnp.int32, (tq, 1), 0)
    tile4 = lambda a: jnp.concatenate([a] * NSA_REP, axis=0)
    cidx = lax.broadcasted_iota(jnp.int32, (tq, nc), 1)
    cbias = tile4(jnp.where(cidx * CMP_STRIDE + (CMP_BLK - 1) <= qpos, 0.0, -1e30))
    row_ok = tile4(jnp.where(qpos >= CMP_BLK - 1, 1.0, 0.0))
    kcol = lax.broadcasted_iota(jnp.int32, (tq, kt), 1)
    eblk = (lax.broadcasted_iota(jnp.int32, (LANES, kt), 0)
            - (lax.broadcasted_iota(jnp.int32, (LANES, kt), 1) >> 6))
    n_full = q0 // kt
    o_c, o_s = [], []
    for g in range(NSA_KV):
        qcg = _stack_group(qc_ref, g, tq)
        qrg = _stack_group(qr_ref, g, tq)
        s = _dot_nt(qcg, kc_ref[...]) + cbias
        e = jnp.exp(s - jnp.max(s, axis=-1, keepdims=True))
        p = e * (row_ok / jnp.maximum(jnp.sum(e, axis=-1, keepdims=True), 1e-30))
        o_c.append(jnp.dot(p.astype(BF16), vc_ref[...], preferred_element_type=F32))
        imp = p[0:tq] + p[tq:2 * tq] + p[2 * tq:3 * tq] + p[3 * tq:4 * tq]
        sel = _select_blocks(_dot3(imp, m_ref[...]), qpos >> 6, nsel)

        def tile_step(t, carry, causal):
            m, l, acc = carry
            k0 = pl.multiple_of(t * kt, kt)
            sc = _dot_nt(qrg, ks_ref[pl.ds(k0, kt), :])
            expand = jnp.where(eblk == t * (kt // SEL_BLK), 1.0, 0.0).astype(BF16)
            bias = (jnp.dot(sel, expand, preferred_element_type=F32) - 1.0) * 1e30
            if causal:
                bias = jnp.where(k0 + kcol <= qpos, bias, -1e30)
            sc = sc + tile4(bias)
            m_new = jnp.maximum(m, jnp.max(sc, axis=-1, keepdims=True))
            a = jnp.exp(sc - m_new)
            alpha = jnp.exp(m - m_new)
            l = alpha * l + jnp.sum(a, axis=-1, keepdims=True)
            acc = alpha * acc + jnp.dot(a.astype(BF16), vs_ref[pl.ds(k0, kt), :], preferred_element_type=F32)
            return m_new, l, acc

        init = (jnp.full((NSA_REP * tq, 1), -1e30, F32), jnp.zeros((NSA_REP * tq, 1), F32),
                jnp.zeros((NSA_REP * tq, LANES), F32))
        carry = lax.fori_loop(0, n_full, lambda t, c: tile_step(t, c, False), init)
        _, l, acc = tile_step(n_full, carry, True)
        o_s.append(acc / jnp.maximum(l, 1e-30))
    ocmp_ref[...] = _unstack_pairs(o_c[0], o_c[1], tq)
    oslc_ref[...] = _unstack_pairs(o_s[0], o_s[1], tq)


def _nsa(qc, qr, kc, vc, ks, vs, b, s, tq=256, kt=512):
    nqb = s // tq
    kt = min(kt, s)
    m = _imp_matrix(s // CMP_STRIDE)
    qspec = pl.BlockSpec((tq, NSA_W), lambda bi, i: (bi * nqb + i, 0))
    cspec = pl.BlockSpec((None, s // CMP_STRIDE, KV_W), lambda bi, i: (bi, 0, 0))
    kspec = pl.BlockSpec((s, KV_W), lambda bi, i: (bi, 0))
    return pl.pallas_call(
        functools.partial(_nsa_kernel, tq=tq, kt=kt),
        grid=(b, nqb),
        in_specs=[qspec, qspec, cspec, cspec, kspec, kspec, pl.BlockSpec(m.shape, lambda bi, i: (0, 0))],
        out_specs=[qspec, qspec],
        out_shape=[jax.ShapeDtypeStruct((b * s, NSA_W), F32)] * 2,
        compiler_params=_cparams(("parallel", "parallel")),
        name="nsa",
    )(qc, qr, kc, vc, ks, vs, m)


def _window_kernel(qr_ref, k_ref, v_ref, o_ref, *, tq, span):
    q0 = pl.program_id(1) * tq
    s_len = k_ref.shape[0]
    start = pl.multiple_of(jnp.minimum(jnp.maximum(q0 - WINDOW, 0), s_len - span), tq)
    qpos = q0 + lax.broadcasted_iota(jnp.int32, (tq, 1), 0)
    k = k_ref[pl.ds(start, span), :]
    v = v_ref[pl.ds(start, span), :]
    diff = qpos - (start + lax.broadcasted_iota(jnp.int32, (tq, span), 1))
    bias = jnp.where((diff >= 0) & (diff < WINDOW), 0.0, -1e30)
    bias = jnp.concatenate([bias] * NSA_REP, axis=0)
    outs = []
    for g in range(NSA_KV):
        qrg = _stack_group(qr_ref, g, tq)
        s = _dot_nt(qrg, k) + bias
        e = jnp.exp(s - jnp.max(s, axis=-1, keepdims=True))
        den = jnp.maximum(jnp.sum(e, axis=-1, keepdims=True), 1e-30)
        outs.append(jnp.dot(e.astype(BF16), v, preferred_element_type=F32) / den)
    o_ref[...] = _unstack_pairs(outs[0], outs[1], tq)


def _window(qr, kw, vw, b, s, tq=128):
    nqb = s // tq
    span = min(WINDOW + tq, s)
    qspec = pl.BlockSpec((tq, NSA_W), lambda bi, i: (bi * nqb + i, 0))
    kspec = pl.BlockSpec((s, KV_W), lambda bi, i: (bi, 0))
    return pl.pallas_call(
        functools.partial(_window_kernel, tq=tq, span=span),
        grid=(b, nqb),
        in_specs=[qspec, kspec, kspec],
        out_specs=qspec,
        out_shape=jax.ShapeDtypeStruct((b * s, NSA_W), F32),
        compiler_params=_cparams(("parallel", "parallel")),
        name="window",
    )(qr, kw, vw)


def _softplus(z):
    return jnp.maximum(z, 0.0) + jnp.log(1.0 + jnp.exp(-jnp.abs(z)))


def _tri_matrix(n):
    return jnp.asarray(np.tril(np.ones((n, n), np.float32)), BF16)


def _sb_block(qh, k_t, v_t, tri, carry, acc, valid):
    z = _dot_nt(qh, k_t)
    sp = _softplus(z)
    if valid is not None:
        sp = jnp.where(valid, sp, 0.0)
    cum = jnp.dot(sp.astype(BF16), tri, preferred_element_type=F32) + carry
    a = jnp.exp(z - cum)
    if valid is not None:
        a = jnp.where(valid, a, 0.0)
    acc = acc + jnp.dot(a.astype(BF16), v_t, preferred_element_type=F32)
    return cum[:, 0:1], acc


def _sb_blocks(qh, k_ref, v_ref, tri, js, state, mask0, ok_last, tq):
    n = len(js)
    offs = [pl.multiple_of(j * tq, tq) for j in js]
    ks = [k_ref[pl.ds(o, tq), :] for o in offs]
    vs = [v_ref[pl.ds(o, tq), :] for o in offs]
    out = []
    for hh in range(2):
        carry, acc = state[2 * hh], state[2 * hh + 1]
        zs = [_dot_nt(qh[hh], k) for k in ks]
        sps = [_softplus(z) for z in zs]
        if mask0 is not None:
            sps[0] = jnp.where(mask0, sps[0], 0.0)
        if ok_last is not None:
            sps[-1] = jnp.where(ok_last, sps[-1], 0.0)
        ps = [jnp.dot(sp.astype(BF16), tri, preferred_element_type=F32) for sp in sps]
        for b in range(n):
            cum = ps[b] + carry
            a = jnp.exp(zs[b] - cum)
            if b == 0 and mask0 is not None:
                a = jnp.where(mask0, a, 0.0)
            if b == n - 1 and ok_last is not None:
                a = jnp.where(ok_last, a, 0.0)
            acc = acc + jnp.dot(a.astype(BF16), vs[b], preferred_element_type=F32)
            carry = cum[:, 0:1]
        out.extend((carry, acc))
    return tuple(out)


def _sb_kernel(q_ref, k_ref, v_ref, tri_ref, o_ref, *, tq):
    i = pl.program_id(2)
    half = lax.broadcasted_iota(jnp.int32, (tq, LANES), 1) >> 6
    q = q_ref[...]
    qh = [jnp.where(half == hh, q, 0) for hh in range(2)]
    tri = tri_ref[...]
    odd = (i & 1) == 1
    causal = lax.broadcasted_iota(jnp.int32, (tq, tq), 1) < lax.broadcasted_iota(jnp.int32, (tq, tq), 0)
    zero = (jnp.zeros((tq, 1), F32), jnp.zeros((tq, LANES), F32))
    state = _sb_blocks(qh, k_ref, v_ref, tri, [i, jnp.maximum(i - 1, 0)], zero + zero, causal, odd, tq)
    first = i - 1 - (i & 1)
    n_two = (first + 1 >> 1) & 1

    def two(p, st):
        return _sb_blocks(qh, k_ref, v_ref, tri, [first, first - 1], st, None, None, tq)

    def four(p, st):
        ja = first - 2 * n_two - 4 * p
        return _sb_blocks(qh, k_ref, v_ref, tri, [ja, ja - 1, ja - 2, ja - 3], st, None, None, tq)

    st = lax.fori_loop(0, n_two, two, state)
    st = lax.fori_loop(0, first + 1 >> 2, four, st)
    o_ref[...] = jnp.where(half == 0, st[1], st[3])


def _sb(sbq, sbk, sbv, b, s, tq=256):
    tq = min(tq, s)
    nqb = s // tq
    tri = _tri_matrix(tq)
    npair = SB_HEADS // 2
    qspec = pl.BlockSpec((tq, LANES), lambda bi, p, i: (bi * nqb + i, p))
    kspec = pl.BlockSpec((s, LANES), lambda bi, p, i: (bi, p))
    return pl.pallas_call(
        functools.partial(_sb_kernel, tq=tq),
        grid=(b, npair, nqb),
        in_specs=[qspec, kspec, kspec, pl.BlockSpec(tri.shape, lambda bi, p, i: (0, 0))],
        out_specs=qspec,
        out_shape=jax.ShapeDtypeStruct((b * s, SB_W), F32),
        compiler_params=_cparams(("parallel", "parallel", "parallel")),
        name="sb",
    )(sbq, sbk, sbv, tri)


def _prep_merge(w_out, w_router, b_router):
    w_a = [w_out[h * HEAD_DIM:(h + 1) * HEAD_DIM] for h in _HEAD_PERM]
    w_out_p = jnp.concatenate(w_a + [w_out[NSA_W:]], axis=0).astype(BF16)
    e = np.zeros((LANES, 3 * NSA_W), np.float32)
    for br in range(3):
        for p, h in enumerate(_HEAD_PERM):
            e[br * NSA_HEADS + h, br * NSA_W + p * HEAD_DIM: br * NSA_W + (p + 1) * HEAD_DIM] = 1.0
    w_r = jnp.concatenate([w_router, jnp.zeros((D_MODEL, LANES - N_EXPERTS), w_router.dtype)], axis=1)
    b_r = jnp.concatenate([b_router, jnp.full((LANES - N_EXPERTS,), -1e30, b_router.dtype)]).reshape(1, LANES)
    return w_out_p, jnp.asarray(e, BF16), w_r, b_r


def _merge_kernel(x_ref, oc_ref, os_ref, ow_ref, ob_ref, gate_ref, br_ref, e_ref, wo_ref, g_ref, wr_ref, brt_ref,
                  x1_ref, hp_ref, idx_ref, wts_ref):
    gexp = _dot3(br_ref[...], e_ref[...], passes=2)
    o_a = gexp[:, :NSA_W] * oc_ref[...] + gexp[:, NSA_W:2 * NSA_W] * os_ref[...] + gexp[:, 2 * NSA_W:] * ow_ref[...]
    gate = gate_ref[...].astype(F32)
    cat = jnp.concatenate([gate[:, :NSA_W] * o_a, gate[:, NSA_W:] * ob_ref[...]], axis=1).astype(BF16)
    x1 = x_ref[...] + jnp.dot(cat, wo_ref[...], preferred_element_type=F32)
    x1_ref[...] = x1
    ms = jnp.mean(x1 * x1, axis=-1, keepdims=True)
    h = (x1 * lax.rsqrt(ms + RMS_EPS)) * g_ref[...]
    hb = h.astype(BF16).astype(F32)
    for j in range(D_MODEL // LANES):
        hp_ref[:, j, :] = hb[:, j * LANES:(j + 1) * LANES]
    lg = jnp.dot(h, wr_ref[...], preferred_element_type=F32, precision=lax.Precision.HIGHEST) + brt_ref[...]
    lane = lax.broadcasted_iota(jnp.int32, lg.shape, 1)
    vals, ids = [], []
    for _ in range(TOP_K):
        mx = jnp.max(lg, axis=-1, keepdims=True)
        first = jnp.min(jnp.where(lg == mx, lane, LANES), axis=-1, keepdims=True)
        vals.append(mx)
        ids.append(first)
        lg = jnp.where(lane == first, -jnp.inf, lg)
    es = [jnp.exp(v - vals[0]) for v in vals]
    tot = es[0] + es[1] + es[2] + es[3]
    idx = jnp.zeros(lg.shape, jnp.int32)
    wts = jnp.zeros(lg.shape, F32)
    for k in range(TOP_K):
        idx = jnp.where(lane == k, ids[k], idx)
        wts = jnp.where(lane == k, es[k] / tot, wts)
    idx_ref[...] = idx
    wts_ref[...] = wts


def _merge(x2d, oc, os_, ow, ob, gate, br, prep, g_ffn, tm=256):
    t = x2d.shape[0]
    tm = min(tm, t)
    assert t % tm == 0
    w_out_p, e, w_r, b_r = prep
    row = lambda n: pl.BlockSpec((tm, n), lambda i: (i, 0))
    full = lambda a: pl.BlockSpec(a.shape, lambda i: (0,) * a.ndim)
    return pl.pallas_call(
        _merge_kernel,
        grid=(t // tm,),
        in_specs=[row(D_MODEL), row(NSA_W), row(NSA_W), row(NSA_W), row(SB_W), row(NSA_W + SB_W), row(LANES),
                  full(e), full(w_out_p), full(g_ffn), full(w_r), full(b_r)],
        out_specs=[row(D_MODEL), pl.BlockSpec((tm, ROW_SUB, LANES), lambda i: (i, 0, 0)), row(LANES), row(LANES)],
        out_shape=[jax.ShapeDtypeStruct((t, D_MODEL), F32), jax.ShapeDtypeStruct((t, ROW_SUB, LANES), F32),
                   jax.ShapeDtypeStruct((t, LANES), jnp.int32), jax.ShapeDtypeStruct((t, LANES), F32)],
        compiler_params=_cparams(("parallel",)),
        name="merge",
    )(x2d, oc, os_, ow, ob, gate, br, e, w_out_p, g_ffn, w_r, b_r)


MOE_TILE = 256
COMBINE_ROWS = 128
ROW_SUB = D_MODEL // LANES


def _route_tables(idx):
    t = idx.shape[0]
    a = t * TOP_K
    assert a % MOE_TILE == 0
    n_tiles = a // MOE_TILE + N_EXPERTS
    e = idx.reshape(a)
    onehot = (e[:, None] == jnp.arange(N_EXPERTS, dtype=jnp.int32)[None, :]).astype(jnp.int32)
    csum = jnp.cumsum(onehot, axis=0)
    rank = jnp.take_along_axis(csum, e[:, None], axis=1)[:, 0] - 1
    counts = csum[-1]
    tiles_e = (counts + MOE_TILE - 1) // MOE_TILE
    tile_end = jnp.cumsum(tiles_e)
    tile_start = tile_end - tiles_e
    dest = tile_start[e] * MOE_TILE + rank
    row_token = jnp.zeros((n_tiles * MOE_TILE,), jnp.int32).at[dest].set(jnp.arange(a, dtype=jnp.int32) // TOP_K)
    tile_ids = jnp.arange(n_tiles, dtype=jnp.int32)
    tile_expert = jnp.minimum(jnp.sum((tile_end[None, :] <= tile_ids[:, None]).astype(jnp.int32), axis=1),
                              N_EXPERTS - 1).astype(jnp.int32)
    return (row_token.reshape(n_tiles, 1, MOE_TILE), dest.reshape(t, TOP_K), tile_expert,
            tile_end[-1:].astype(jnp.int32))


def _deinterleave_matrix():
    m = np.zeros((2 * LANES, 2 * LANES), np.float32)
    m[2 * np.arange(LANES), np.arange(LANES)] = 1.0
    m[2 * np.arange(LANES) + 1, LANES + np.arange(LANES)] = 1.0
    return jnp.asarray(m, BF16)


def _row_tiles_to_2d(load):
    return jnp.concatenate([load(j) for j in range(ROW_SUB)], axis=1)


def _expert_kernel(te_ref, nu_ref, rt_ref, rtn_ref, h_ref, wgu_ref, wdn_ref, bg_ref, bu_ref, bd_ref, seo_ref,
                   y_ref, xbuf, sem, wg_s, wu_s, wd_s):
    i = pl.program_id(0)
    n_used = nu_ref[0]
    slot = i % 2

    def copy(idx_ref, r, s):
        return pltpu.make_async_copy(h_ref.at[pl.ds(idx_ref[0, r], 1)], xbuf.at[s, pl.ds(r, 1)], sem.at[s])

    def issue(idx_ref, s):
        def body(r, c):
            copy(idx_ref, r, s).start()
            return c
        lax.fori_loop(0, MOE_TILE, body, 0, unroll=8)

    def wait_rows():
        def wait(r, c):
            copy(rt_ref, r, slot).wait()
            return c
        lax.fori_loop(0, MOE_TILE, wait, 0, unroll=8)

    @pl.when(i == 0)
    def _():
        issue(rt_ref, 0)

    @pl.when(i + 1 < n_used)
    def _():
        issue(rtn_ref, 1 - slot)

    @pl.when(i < n_used)
    def _():
        @pl.when((i == 0) | (te_ref[i] != te_ref[jnp.maximum(i - 1, 0)]))
        def _():
            for k in range(D_FF // LANES):
                wblk = wgu_ref[:, 2 * LANES * k:2 * LANES * (k + 1)].astype(BF16)
                split = jnp.dot(wblk, seo_ref[...], preferred_element_type=F32).astype(BF16)
                wg_s[:, LANES * k:LANES * (k + 1)] = split[:, :LANES]
                wu_s[:, LANES * k:LANES * (k + 1)] = split[:, LANES:]
            wd_s[...] = wdn_ref[...].astype(BF16)

        wait_rows()
        x = _row_tiles_to_2d(lambda j: xbuf[slot, :, j, :]).astype(BF16)
        g = jnp.minimum(jnp.dot(x, wg_s[...], preferred_element_type=F32) + bg_ref[...], SWIGLU_LIMIT)
        u = jnp.clip(jnp.dot(x, wu_s[...], preferred_element_type=F32) + bu_ref[...], -SWIGLU_LIMIT, SWIGLU_LIMIT)
        act = (u + 1.0) * g * jax.nn.sigmoid(SWIGLU_ALPHA * g)
        y = jnp.dot(act.astype(BF16), wd_s[...], preferred_element_type=F32) + bd_ref[...]
        for j in range(ROW_SUB):
            y_ref[:, j, :] = y[:, j * LANES:(j + 1) * LANES]

    @pl.when(i >= n_used)
    def _():
        y_ref[...] = jnp.zeros(y_ref.shape, F32)


def _experts(h_rows, row_token, tile_expert, n_used, w_gu, w_dn, bg, bu, bd):
    n_tiles = row_token.shape[0]
    seo = _deinterleave_matrix()
    wspec = lambda shape: pl.BlockSpec((None,) + shape, lambda i, te, nu: (te[i], 0, 0))
    full = lambda a: pl.BlockSpec(a.shape, lambda i, te, nu: (0,) * a.ndim)
    rt_spec = lambda off: pl.BlockSpec((None, 1, MOE_TILE), lambda i, te, nu: (jnp.minimum(i + off, n_tiles - 1), 0, 0),
                                       memory_space=pltpu.SMEM)
    return pl.pallas_call(
        _expert_kernel,
        grid_spec=pltpu.PrefetchScalarGridSpec(
            num_scalar_prefetch=2,
            grid=(n_tiles,),
            in_specs=[rt_spec(0), rt_spec(1), pl.BlockSpec(memory_space=pl.ANY),
                      wspec((D_MODEL, 2 * D_FF)), wspec((D_FF, D_MODEL)), wspec((1, D_FF)), wspec((1, D_FF)),
                      wspec((1, D_MODEL)), full(seo)],
            out_specs=pl.BlockSpec((MOE_TILE, ROW_SUB, LANES), lambda i, te, nu: (i, 0, 0)),
            scratch_shapes=[pltpu.VMEM((2, MOE_TILE, ROW_SUB, LANES), F32), pltpu.SemaphoreType.DMA((2,)),
                            pltpu.VMEM((D_MODEL, D_FF), BF16), pltpu.VMEM((D_MODEL, D_FF), BF16),
                            pltpu.VMEM((D_FF, D_MODEL), BF16)],
        ),
        out_shape=jax.ShapeDtypeStruct((n_tiles * MOE_TILE, ROW_SUB, LANES), F32),
        compiler_params=_cparams(("arbitrary",)),
        name="moe_experts",
    )(tile_expert, n_used, row_token, row_token, h_rows, w_gu, w_dn, bg, bu, bd, seo)


def _combine_kernel(pos_ref, posn_ref, x_ref, wts_ref, g_ref, y_ref, out_ref, buf, sem, *, n):
    i = pl.program_id(0)
    n_rows = TOP_K * COMBINE_ROWS

    def copy(p_ref, r, slot):
        return pltpu.make_async_copy(y_ref.at[pl.ds(p_ref[0, r], 1)],
                                     buf.at[slot, r & (TOP_K - 1), pl.ds(r >> 2, 1)], sem.at[slot])

    def issue(p_ref, slot):
        def body(r2, c):
            copy(p_ref, 2 * r2, slot).start(priority=0)
            copy(p_ref, 2 * r2 + 1, slot).start(priority=1)
            return c
        lax.fori_loop(0, n_rows // 2, body, 0, unroll=4)

    slot = i % 2

    @pl.when(i == 0)
    def _():
        issue(pos_ref, 0)

    @pl.when(i + 1 < n)
    def _():
        issue(posn_ref, 1 - slot)

    def wait(r, c):
        copy(pos_ref, r, slot).wait()
        return c
    lax.fori_loop(0, n_rows, wait, 0, unroll=8)

    w = wts_ref[...]
    parts = []
    ssq = jnp.zeros((COMBINE_ROWS, 1), F32)
    for j in range(ROW_SUB):
        xs = x_ref[:, j * LANES:(j + 1) * LANES]
        for k in range(TOP_K):
            xs = xs + w[:, k:k + 1] * buf[slot, k, :, j, :]
        parts.append(xs)
        ssq = ssq + jnp.sum(xs * xs, axis=-1, keepdims=True)
    rs = lax.rsqrt(ssq * (1.0 / D_MODEL) + RMS_EPS)
    for j in range(ROW_SUB):
        out_ref[:, j * LANES:(j + 1) * LANES] = (parts[j] * rs) * g_ref[:, j * LANES:(j + 1) * LANES]


def _combine(x1, wts, pos, y_sorted, g_final):
    t = x1.shape[0]
    assert t % COMBINE_ROWS == 0
    steps = t // COMBINE_ROWS
    pos2 = pos.reshape(steps, 1, TOP_K * COMBINE_ROWS)
    row = lambda nn: pl.BlockSpec((COMBINE_ROWS, nn), lambda i: (i, 0))
    return pl.pallas_call(
        functools.partial(_combine_kernel, n=steps),
        grid=(steps,),
        in_specs=[pl.BlockSpec((None, 1, TOP_K * COMBINE_ROWS), lambda i: (i, 0, 0), memory_space=pltpu.SMEM),
                  pl.BlockSpec((None, 1, TOP_K * COMBINE_ROWS), lambda i: (jnp.minimum(i + 1, steps - 1), 0, 0),
                               memory_space=pltpu.SMEM),
                  row(D_MODEL), row(LANES), pl.BlockSpec(g_final.shape, lambda i: (0, 0)),
                  pl.BlockSpec(memory_space=pl.ANY)],
        out_specs=row(D_MODEL),
        out_shape=jax.ShapeDtypeStruct((t, D_MODEL), F32),
        scratch_shapes=[pltpu.VMEM((2, TOP_K, COMBINE_ROWS, ROW_SUB, LANES), F32), pltpu.SemaphoreType.DMA((2,))],
        compiler_params=_cparams(("arbitrary",)),
        name="moe_combine",
    )(pos2, pos2, x1, wts, g_final, y_sorted)


QPAD = 8


def _select_blocks_rows(imp_sel, cur, nsel):
    tq = imp_sel.shape[0]
    blk = lax.broadcasted_iota(jnp.int32, (tq, LANES), 1)
    forced = (blk == 0) | (blk == cur) | (blk == cur - 1)
    score = jnp.where(blk <= cur, imp_sel + jnp.where(forced, FORCE_BONUS, 0.0), -1e9)
    cnt = jnp.zeros((tq, LANES), jnp.int32)
    for k in range(nsel):
        sk = score[:, k:k + 1]
        beats = (sk > score) | ((sk == score) & (blk > k))
        cnt = cnt + jnp.where(beats, 1, 0)
    return jnp.where((cnt < TOP_N) & (blk < nsel), 1.0, 0.0)


def _pad_rows(x, n):
    return jnp.concatenate([x, jnp.zeros((n - x.shape[0], x.shape[1]), x.dtype)], axis=0)


NSA_S_SEQS = 4
SB_S_SEQS = 2


def _nsa_s_kernel(pt_ref, qc_ref, qr_ref, xnew_ref, snew_ref, w1_ref, w2_ref, m_ref, *rest, n_pages, past, dec, seqs):
    ocmp_ref, oslc_ref = rest[2 * seqs * n_pages:]
    pages = [rest[2 * n_pages * s:2 * n_pages * (s + 1)] for s in range(seqs)]
    nc = n_pages * (PAGE_SIZE // CMP_STRIDE)
    xb = jnp.concatenate([p[...] for s in range(seqs) for p in pages[s][:n_pages]], axis=0).astype(BF16)
    y1 = jnp.dot(xb, w1_ref[...], preferred_element_type=F32)
    y2 = jnp.dot(xb, w2_ref[...], preferred_element_type=F32)
    xn = jnp.concatenate([xnew_ref[s] for s in range(seqs)] + [xnew_ref[0]] * (8 - seqs), axis=0)
    ynew = jnp.dot(xn.astype(BF16), w2_ref[...], preferred_element_type=F32)
    rid = lax.broadcasted_iota(jnp.int32, (nc, y2.shape[1]), 0)
    for s in range(seqs):
        y2s = pltpu.roll(y2[s * nc:(s + 1) * nc], nc - 1, 0)
        kcv = y1[s * nc:(s + 1) * nc] + jnp.where(rid < nc - 1, y2s, ynew[s:s + 1])
        _nsa_s_one(qc_ref.at[s], qr_ref.at[s], snew_ref.at[s], m_ref, kcv[:, :KV_W].astype(BF16),
                   kcv[:, KV_W:].astype(BF16), pages[s][n_pages:], ocmp_ref.at[s], oslc_ref.at[s], n_pages, past, dec)


def _nsa_s_one(qc_ref, qr_ref, snew_ref, m_ref, kc, vc, slc_pages, ocmp_ref, oslc_ref, n_pages, past, dec):
    tq = QPAD
    nsel = past // SEL_BLK + 1
    trow = lax.broadcasted_iota(jnp.int32, (NSA_REP * tq, 1), 0) & (tq - 1)
    rpos = past + jnp.minimum(trow, dec - 1)
    qpos = past + jnp.minimum(lax.broadcasted_iota(jnp.int32, (tq, 1), 0), dec - 1)
    lane = lax.broadcasted_iota(jnp.int32, (tq, LANES), 1)
    snew = snew_ref[...]
    k_all = jnp.concatenate([pg[:, :KV_W].astype(BF16) for pg in slc_pages]
                            + [_pad_rows(snew[:, :KV_W], LANES).astype(BF16)], axis=0)
    v_all = jnp.concatenate([pg[:, KV_W:].astype(BF16) for pg in slc_pages]
                            + [_pad_rows(snew[:, KV_W:], LANES).astype(BF16)], axis=0)
    o_c, o_s = [], []
    for g in range(NSA_KV):
        qcg = _stack_group(qc_ref, g, tq)
        qrg = _stack_group(qr_ref, g, tq)
        s = _dot_nt(qcg, kc)
        cidx = lax.broadcasted_iota(jnp.int32, s.shape, 1)
        cmask = cidx * CMP_STRIDE + (CMP_BLK - 1) <= rpos
        s = jnp.where(cmask, s, -1e30)
        e = jnp.where(cmask, jnp.exp(s - jnp.max(s, axis=-1, keepdims=True)), 0.0)
        p = e / jnp.maximum(jnp.sum(e, axis=-1, keepdims=True), 1e-30)
        o_c.append(jnp.dot(p.astype(BF16), vc, preferred_element_type=F32))
        imp = p[0:tq] + p[tq:2 * tq] + p[2 * tq:3 * tq] + p[3 * tq:4 * tq]
        sel = _select_blocks_rows(_dot3(imp, m_ref[...]), qpos >> 6, nsel)
        masks =[jnp.where(lane < SEL_BLK, sel[:, 2 * j:2 * j + 1], sel[:, 2 * j + 1:2 * j + 2]) > 0.5
                 for j in range(n_pages)]
        masks.append((sel[:, nsel - 1:nsel] > 0.5) & (lane < dec) & (past + lane <= qpos))
        valid = jnp.concatenate([jnp.concatenate(masks, axis=1)] * NSA_REP, axis=0)
        sc = jnp.where(valid, _dot_nt(qrg, k_all), -1e30)
        a = jnp.where(valid, jnp.exp(sc - jnp.max(sc, axis=-1, keepdims=True)), 0.0)
        l = jnp.sum(a, axis=-1, keepdims=True)
        o_s.append(jnp.dot(a.astype(BF16), v_all, preferred_element_type=F32) / jnp.maximum(l, 1e-30))
    ocmp_ref[...] = _unstack_pairs(o_c[0], o_c[1], tq)
    oslc_ref[...] = _unstack_pairs(o_s[0], o_s[1], tq)


def _nsa_s(page_table, qc, qr, xnew, snew, w1, w2, cache_cmp, cache_slc, past, dec):
    bd, n_pages = page_table.shape
    seqs = NSA_S_SEQS if bd % NSA_S_SEQS == 0 else 1
    cpp = PAGE_SIZE // CMP_STRIDE
    cmp_pages = cache_cmp.reshape(cache_cmp.shape[0], cpp, CMP_STRIDE * 2 * KV_W)
    m = _imp_matrix(n_pages * cpp, past // SEL_BLK + 1)
    per_b = lambda a: pl.BlockSpec((seqs,) + a.shape[1:], lambda b, pt: (b,) + (0,) * (a.ndim - 1))
    full = lambda a: pl.BlockSpec(a.shape, lambda b, pt: (0,) * a.ndim)
    page = lambda a, s, j: pl.BlockSpec((None,) + a.shape[1:], lambda b, pt: (pt[b * seqs + s, j], 0, 0))
    ospec = pl.BlockSpec((seqs, QPAD, NSA_W), lambda b, pt: (b, 0, 0))
    page_specs, page_args = [], []
    for s in range(seqs):
        page_specs += [page(cmp_pages, s, j) for j in range(n_pages)] + [page(cache_slc, s, j) for j in range(n_pages)]
        page_args += [cmp_pages] * n_pages + [cache_slc] * n_pages
    return pl.pallas_call(
        functools.partial(_nsa_s_kernel, n_pages=n_pages, past=past, dec=dec, seqs=seqs),
        grid_spec=pltpu.PrefetchScalarGridSpec(
            num_scalar_prefetch=1,
            grid=(bd // seqs,),
            in_specs=[per_b(qc), per_b(qr), per_b(xnew), per_b(snew), full(w1), full(w2), full(m)] + page_specs,
            out_specs=[ospec, ospec],
        ),
        out_shape=[jax.ShapeDtypeStruct((bd, QPAD, NSA_W), F32)] * 2,
        compiler_params=_cparams(("arbitrary",)),
        name="nsa_sample",
    )(page_table, qc, qr, xnew, snew, w1, w2, m, *page_args)


def _window_s_kernel(qr_ref, st_ref, wnew_ref, o_ref, *, past, dec):
    tq = QPAD
    wbuf = st_ref.shape[0]
    trow = jnp.minimum(lax.broadcasted_iota(jnp.int32, (NSA_REP * tq, 1), 0) & (tq - 1), dec - 1)
    k = st_ref[:, :KV_W].astype(BF16)
    v = st_ref[:, KV_W:].astype(BF16)
    wnew = wnew_ref[...]
    kn = _pad_rows(wnew[:, :KV_W], LANES).astype(BF16)
    vn = _pad_rows(wnew[:, KV_W:], LANES).astype(BF16)
    outs = []
    for g in range(NSA_KV):
        qrg = _stack_group(qr_ref, g, tq)
        s0 = _dot_nt(qrg, k)
        d0 = wbuf + trow - lax.broadcasted_iota(jnp.int32, s0.shape, 1)
        m0 = (d0 >= 0) & (d0 < WINDOW)
        s1 = _dot_nt(qrg, kn)
        i1 = lax.broadcasted_iota(jnp.int32, s1.shape, 1)
        m1 = (i1 < dec) & (trow - i1 >= 0) & (trow - i1 < WINDOW)
        s0 = jnp.where(m0, s0, -1e30)
        s1 = jnp.where(m1, s1, -1e30)
        mx = jnp.maximum(jnp.max(s0, axis=-1, keepdims=True), jnp.max(s1, axis=-1, keepdims=True))
        e0 = jnp.where(m0, jnp.exp(s0 - mx), 0.0)
        e1 = jnp.where(m1, jnp.exp(s1 - mx), 0.0)
        den = jnp.maximum(jnp.sum(e0, axis=-1, keepdims=True) + jnp.sum(e1, axis=-1, keepdims=True), 1e-30)
        o = (jnp.dot(e0.astype(BF16), v, preferred_element_type=F32)
             + jnp.dot(e1.astype(BF16), vn, preferred_element_type=F32))
        outs.append(o / den)
    o_ref[...] = _unstack_pairs(outs[0], outs[1], tq)


def _window_s(qr, state, wnew, past, dec):
    bd = qr.shape[0]
    per_b = lambda a: pl.BlockSpec((None,) + a.shape[1:], lambda b: (b,) + (0,) * (a.ndim - 1))
    return pl.pallas_call(
        functools.partial(_window_s_kernel, past=past, dec=dec),
        grid=(bd,),
        in_specs=[per_b(qr), per_b(state), per_b(wnew)],
        out_specs=pl.BlockSpec((None, QPAD, NSA_W), lambda b: (b, 0, 0)),
        out_shape=jax.ShapeDtypeStruct((bd, QPAD, NSA_W), F32),
        compiler_params=_cparams(("parallel",)),
        name="window_sample",
    )(qr, state, wnew)


def _sb_s_kernel(pt_ref, q_ref, new_ref, tri_ref, *rest, n_pages, seqs):
    o_ref = rest[seqs * n_pages]
    for s in range(seqs):
        _sb_s_one(q_ref.at[s], new_ref.at[s], tri_ref, rest[n_pages * s:n_pages * (s + 1)], o_ref.at[s], n_pages)


def _sb_s_one(q_ref, new_ref, tri_ref, pages, o_ref, n_pages):
    tq = QPAD
    nrow = SB_HEADS * tq
    row = lax.broadcasted_iota(jnp.int32, (nrow, SB_W), 0)
    col = lax.broadcasted_iota(jnp.int32, (nrow, SB_W), 1)
    own = (row >> 3) == (col >> 6)
    qbd = jnp.where(own, jnp.concatenate([q_ref[...]] * SB_HEADS, axis=0), 0.0).astype(BF16)
    new = new_ref[...]
    nblk = n_pages + 1
    k_all = jnp.concatenate([pg[:, :SB_W].astype(BF16) for pg in pages]
                            + [_pad_rows(new[:, :SB_W], PAGE_SIZE).astype(BF16)], axis=0)
    v_all = jnp.concatenate([pg[:, SB_W:].astype(BF16) for pg in pages]
                            + [_pad_rows(new[:, SB_W:], PAGE_SIZE).astype(BF16)], axis=0)
    z_all = _dot_nt(qbd, k_all)
    blk = lambda x, j: x[j * nrow:(j + 1) * nrow]
    zs = jnp.concatenate([z_all[:, j * PAGE_SIZE:(j + 1) * PAGE_SIZE] for j in range(nblk)], axis=0)
    lane = lax.broadcasted_iota(jnp.int32, (nrow, PAGE_SIZE), 1)
    trow = lax.broadcasted_iota(jnp.int32, (nrow, PAGE_SIZE), 0) & (tq - 1)
    causal_new = lane < trow
    sp = _softplus(zs)
    sp = jnp.concatenate([sp[:n_pages * nrow], jnp.where(causal_new, blk(sp, n_pages), 0.0)], axis=0)
    p = jnp.dot(sp.astype(BF16), tri_ref[...], preferred_element_type=F32)
    carry = jnp.zeros((nrow, 1), F32)
    carries = [None] * nblk
    for j in reversed(range(nblk)):
        carries[j] = jnp.broadcast_to(carry, (nrow, PAGE_SIZE))
        carry = carry + blk(p, j)[:, 0:1]
    a = jnp.exp(zs - (p + jnp.concatenate(carries, axis=0)))
    a_wide = jnp.concatenate([blk(a, j) for j in range(n_pages)] + [jnp.where(causal_new, blk(a, n_pages), 0.0)],
                             axis=1)
    acc = jnp.dot(a_wide.astype(BF16), v_all, preferred_element_type=F32)
    acc = jnp.where(own, acc, 0.0)
    out = acc[0:tq]
    for h in range(1, SB_HEADS):
        out = out + acc[h * tq:(h + 1) * tq]
    o_ref[...] = out


def _sb_s(page_table, sbq, sbnew, cache_sb):
    bd, n_pages = page_table.shape
    seqs = SB_S_SEQS if bd % SB_S_SEQS == 0 else 1
    tri = _tri_matrix(PAGE_SIZE)
    per_b = lambda a: pl.BlockSpec((seqs,) + a.shape[1:], lambda b, pt: (b,) + (0,) * (a.ndim - 1))
    page = lambda s, j: pl.BlockSpec((None,) + cache_sb.shape[1:], lambda b, pt: (pt[b * seqs + s, j], 0, 0))
    return pl.pallas_call(
        functools.partial(_sb_s_kernel, n_pages=n_pages, seqs=seqs),
        grid_spec=pltpu.PrefetchScalarGridSpec(
            num_scalar_prefetch=1,
            grid=(bd // seqs,),
            in_specs=[per_b(sbq), per_b(sbnew), pl.BlockSpec(tri.shape, lambda b, pt: (0, 0))]
            + [page(s, j) for s in range(seqs) for j in range(n_pages)],
            out_specs=pl.BlockSpec((seqs, QPAD, SB_W), lambda b, pt: (b, 0, 0)),
        ),
        out_shape=jax.ShapeDtypeStruct((bd, QPAD, SB_W), F32),
        compiler_params=_cparams(("arbitrary",)),
        name="sb_sample",
    )(page_table, sbq, sbnew, tri, *([cache_sb] * (seqs * n_pages)))


def kernel(x_prompt, x_sample, cache_nsa_cmp, cache_nsa_slc, cache_sb, state_nsa_win, page_table, g_mix, w_in, w_ck,
           w_cv, w_out, g_ffn, w_router, b_router, w_gu, b_gu, w_dn, b_dn, g_final):
    b, s, d = x_prompt.shape
    bd, ds, _ = x_sample.shape
    depth = g_mix.shape[0]
    assert depth == 1 and d == D_MODEL and ds <= QPAD
    n_pages = page_table.shape[1]
    past = n_pages * PAGE_SIZE
    wbuf = state_nsa_win.shape[2]
    n_phys = cache_nsa_cmp.shape[1]
    tp, ts = b * s, bd * ds

    w_perm = _prep_w_in(w_in[0])
    w1, w2 = _prep_w_cmp(w_ck[0], w_cv[0])
    prep = _prep_merge(w_out[0], w_router[0], b_router[0])
    g_mix2, g_ffn2, g_fin2 = g_mix[0].reshape(1, d), g_ffn[0].reshape(1, d), g_final.reshape(1, d)
    bg = b_gu[0][:, None, 0::2]
    bu = b_gu[0][:, None, 1::2]
    bdn = b_dn[0][:, None, :]

    pos_p = jnp.arange(s, dtype=jnp.int32)
    (qc, qr, sbq, cmp_p, slc_p, win_p, sbkv_p, kslc, vslc, kwin, vwin, sbk, sbv, gate_p, br_p) = _proj(
        x_prompt.reshape(tp, d), g_mix2, w_perm, _rope_tables(pos_p), 256)
    kc, vc = _compress(cmp_p, w1, w2, b, s)
    o_cmp, o_slc = _nsa(qc, qr, kc, vc, kslc, vslc, b, s)
    o_win = _window(qr, kwin, vwin, b, s)
    o_sb = _sb(sbq, sbk, sbv, b, s)
    x1_p, h_p, idx_p, wts_p = _merge(x_prompt.reshape(tp, d), o_cmp, o_slc, o_win, o_sb, gate_p, br_p, prep, g_ffn2)

    pos_s = past + jnp.arange(ds, dtype=jnp.int32)
    tabs_s = [jnp.tile(t, (bd, 1)) for t in _rope_tables(pos_s)]
    (qc_s, qr_s, sbq_s, cmp_s, slc_s, win_s, sbkv_s, _, _, _, _, _, _, gate_s, br_s) = _proj(
        x_sample.reshape(ts, d), g_mix2, w_perm, tabs_s, min(256, ts))
    pad_q = lambda a: jnp.pad(a.astype(F32).reshape(bd, ds, a.shape[1]), ((0, 0), (0, QPAD - ds), (0, 0)))
    xnew = jnp.pad(cmp_s.reshape(bd, 1, ds * 2 * KV_W), ((0, 0), (0, 0), (0, (CMP_STRIDE - ds) * 2 * KV_W)))
    o_cmp_s, o_slc_s = _nsa_s(page_table, pad_q(qc_s), pad_q(qr_s), xnew, pad_q(slc_s), w1, w2,
                              cache_nsa_cmp[0].reshape(n_phys, PAGE_SIZE, 2 * KV_W),
                              cache_nsa_slc[0].reshape(n_phys, PAGE_SIZE, 2 * KV_W), past, ds)
    o_win_s = _window_s(pad_q(qr_s), state_nsa_win[0].reshape(bd, wbuf, 2 * KV_W), pad_q(win_s), past, ds)
    o_sb_s = _sb_s(page_table, pad_q(sbq_s), pad_q(sbkv_s), cache_sb[0].reshape(n_phys, PAGE_SIZE, 2 * SB_W))
    unpad = lambda a: a[:, :ds].reshape(ts, a.shape[2])
    x1_s, h_s, idx_s, wts_s = _merge(x_sample.reshape(ts, d), unpad(o_cmp_s), unpad(o_slc_s), unpad(o_win_s),
                                     unpad(o_sb_s), gate_s, br_s, prep, g_ffn2)

    h_rows = jnp.concatenate([h_p, h_s], axis=0)
    idx = jnp.concatenate([idx_p[:, :TOP_K], idx_s[:, :TOP_K]], axis=0)
    row_token, pos, tile_expert, n_used = _route_tables(idx)
    y_sorted = _experts(h_rows, row_token, tile_expert, n_used, w_gu[0], w_dn[0], bg, bu, bdn)
    y_p = _combine(x1_p, wts_p, pos[:tp], y_sorted, g_fin2)
    y_s = _combine(x1_s, wts_s, pos[tp:], y_sorted, g_fin2)

    win_all = jnp.concatenate([state_nsa_win[0], win_s.reshape(bd, ds, 2, NSA_KV, HEAD_DIM)], axis=1)
    lead = lambda a, n, h: a.reshape(1, n, -1, 2, h, HEAD_DIM)
    return (y_p.reshape(b, s, d), y_s.reshape(bd, ds, d),
            lead(cmp_p, b, NSA_KV), lead(slc_p, b, NSA_KV), lead(sbkv_p, b, SB_HEADS),
            lead(win_p, b, NSA_KV)[:, :, -wbuf:] if s >= wbuf else
            jnp.pad(lead(win_p, b, NSA_KV), ((0, 0), (0, 0), (wbuf - s, 0), (0, 0), (0, 0), (0, 0))),
            lead(cmp_s, bd, NSA_KV), lead(slc_s, bd, NSA_KV), lead(sbkv_s, bd, SB_HEADS),
            win_all[None, :, ds:])
```
